```python
import jax, jax.numpy as jnp
from jax import lax
import numpy as np

D_MODEL = 1024
BATCH = 2
SEQ = 8192
DEPTH = 1

GRID_W = 64
CTX_LEN = 256
CHUNK_A = 128
A_GROUPS = 4
D_A = D_MODEL
GLA_HEADS = 4
DK_TOT = D_MODEL // 2
DV_TOT = D_MODEL
DK = DK_TOT // GLA_HEADS
DV = DV_TOT // GLA_HEADS
GATE_RANK = 16
GATE_TAU = 16.0
GLA_CHUNK = 64
D_FF = 4 * D_MODEL
EPS = 1e-6
IN_COLS = 2 * D_A + 2 * DK_TOT + 2 * DV_TOT + 2 * GATE_RANK + 2 * D_MODEL

kernel_name = "hybrid_gmlp_gla_prefix_dit_block"


def rms_norm(x, g):
    xf = x.astype(jnp.float32)
    y = xf * lax.rsqrt(jnp.mean(xf * xf, -1, keepdims=True) + EPS)
    return (y * g.astype(jnp.float32)).astype(x.dtype)


def modulate(x, g, shift, scale):
    return rms_norm(x, g) * (1 + scale) + shift


def grid_pos_embed(n_tokens, dim, dtype):
    rows = n_tokens // GRID_W
    t = jnp.arange(rows * GRID_W)
    row = (t // GRID_W).astype(jnp.float32)
    col = (t % GRID_W).astype(jnp.float32)
    quarter = dim // 4
    freqs = 1.0 / (10000.0 ** (jnp.arange(quarter, dtype=jnp.float32) / quarter))

    def emb(p):
        a = p[:, None] * freqs[None, :]
        return jnp.concatenate([jnp.sin(a), jnp.cos(a)], -1)

    return jnp.concatenate([emb(row), emb(col)], -1).astype(dtype)


def chunk_mlp(u, v, ln_g, ln_b, w_s, b_s):
    B, L, _ = v.shape
    vf = v.astype(jnp.float32)
    mu = jnp.mean(vf, -1, keepdims=True)
    var = jnp.mean(jnp.square(vf - mu), -1, keepdims=True)
    vn = ((vf - mu) * lax.rsqrt(var + EPS) * ln_g.astype(jnp.float32) + ln_b.astype(jnp.float32)).astype(v.dtype)
    vc = vn.reshape(B, L // CHUNK_A, CHUNK_A, A_GROUPS, D_A // A_GROUPS)
    s = jnp.einsum('gpq,bnqgc->bnpgc', w_s, vc) + b_s.T[None, None, :, :, None]
    return u * s.reshape(B, L, D_A)


def gla_scan(q, k, v, logdecay, s0):
    B, L, H, _ = q.shape
    n = L // GLA_CHUNK

    def to_chunks(a):
        return a.reshape(B, n, GLA_CHUNK, H, a.shape[-1]).transpose(1, 0, 3, 2, 4)

    mask = jnp.tril(jnp.ones((GLA_CHUNK, GLA_CHUNK), dtype=bool))[:, :, None]

    def body(S, inp):
        qc, kc, vc, gc = inp
        b = jnp.cumsum(gc, axis=2)
        diff = b[:, :, :, None, :] - b[:, :, None, :, :]
        dec = jnp.exp(jnp.where(mask, diff, -jnp.inf))
        attn = jnp.einsum('bhik,bhjk,bhijk->bhij', qc, kc, dec)
        o = jnp.einsum('bhij,bhjv->bhiv', attn, vc) + jnp.einsum('bhik,bhkv->bhiv', qc * jnp.exp(b), S)
        b_last = b[:, :, -1]
        S_new = jnp.exp(b_last)[..., None] * S + jnp.einsum(
            'bhjk,bhjv->bhkv', kc * jnp.exp(b_last[:, :, None, :] - b), vc)
        return S_new, o

    S_fin, o = lax.scan(body, s0, (to_chunks(q), to_chunks(k), to_chunks(v), to_chunks(logdecay)))
    return o.transpose(1, 0, 3, 2, 4).reshape(B, L, H, v.shape[-1]), S_fin


def bidir_gla(q, k, v, lg_f, lg_b, s_f0, s_b0):
    o_f, s_f = gla_scan(q, k, v, lg_f, s_f0)
    flip = lambda a: a[:, ::-1]
    o_b, s_b = gla_scan(flip(q), flip(k), flip(v), flip(lg_b), s_b0)
    return o_f + flip(o_b), s_f, s_b


def mixer(h, s_f0, s_b0, lp, with_output):
    B, T, _ = h.shape
    f32 = jnp.float32
    z = h @ lp['w_in']
    splits = list(np.cumsum([D_A, D_A, DK_TOT, DK_TOT, DV_TOT, DV_TOT, 2 * GATE_RANK]))
    u_a, v_a, q, k, v, r, dec, gates = jnp.split(z, splits, axis=-1)

    heads = lambda a, d: a.astype(f32).reshape(B, T, GLA_HEADS, d)
    w_dec2 = lp['w_dec2'].astype(f32)
    b_dec = lp['b_dec'].astype(f32)
    dec_f, dec_b = jnp.split(dec.astype(f32), 2, axis=-1)
    lg_f = jax.nn.log_sigmoid(dec_f @ w_dec2[0] + b_dec[0]) / GATE_TAU
    lg_b = jax.nn.log_sigmoid(dec_b @ w_dec2[1] + b_dec[1]) / GATE_TAU
    o, s_f, s_b = bidir_gla(heads(q, DK) * DK ** -0.5, heads(k, DK), heads(v, DV),
                            heads(lg_f, DK), heads(lg_b, DK), s_f0, s_b0)
    if not with_output:
        return None, s_f, s_b

    on = o * lax.rsqrt(jnp.mean(o * o, -1, keepdims=True) + EPS) * lp['onorm_g'].astype(f32).reshape(GLA_HEADS, DV)
    y_b = on.reshape(B, T, DV_TOT).astype(h.dtype) * jax.nn.silu(r)

    y_a = chunk_mlp(jax.nn.gelu(u_a), jax.nn.gelu(v_a), lp['lnv_g'], lp['lnv_b'], lp['w_s'], lp['b_s'])

    g_a, g_b = jnp.split(jax.nn.sigmoid(gates), 2, axis=-1)
    merged = g_a * (y_a @ lp['proj_a']) + g_b * (y_b @ lp['proj_b'])
    return merged @ lp['w_out'], s_f, s_b


def sq_relu_mlp(h, w1, w2):
    return jnp.square(jax.nn.relu(h @ w1)) @ w2


def setup_inputs(seed: int = 0) -> dict:
    key = jax.random.key(seed)
    ks = jax.random.split(key, 24)
    nrm = lambda k, shape, s: jax.random.normal(k, shape, jnp.float32) * s
    return {
        "x": nrm(ks[0], (BATCH, SEQ, D_MODEL), 1.0),
        "c": nrm(ks[1], (BATCH, D_MODEL), 1.0),
        "ctx": nrm(ks[2], (BATCH, CTX_LEN, D_MODEL), 1.0),
        "c_ctx": nrm(ks[3], (D_MODEL,), 1.0),
        "w_ada": nrm(ks[4], (DEPTH, D_MODEL, 6 * D_MODEL), 0.5 * D_MODEL ** -0.5),
        "b_ada": nrm(ks[5], (DEPTH, 6 * D_MODEL), 0.02),
        "norm1_g": 1.0 + nrm(ks[6], (DEPTH, D_MODEL), 0.02),
        "w_in": nrm(ks[7], (DEPTH, D_MODEL, IN_COLS), D_MODEL ** -0.5),
        "lnv_g": 1.0 + nrm(ks[8], (DEPTH, D_A), 0.02),
        "lnv_b": nrm(ks[9], (DEPTH, D_A), 0.02),
        "w_s": nrm(ks[10], (DEPTH, A_GROUPS, CHUNK_A, CHUNK_A), CHUNK_A ** -0.5),
        "b_s": 1.0 + nrm(ks[11], (DEPTH, A_GROUPS, CHUNK_A), 0.02),
        "w_dec2": nrm(ks[12], (DEPTH, 2, GATE_RANK, DK_TOT), GATE_RANK ** -0.5),
        "b_dec": nrm(ks[13], (DEPTH, 2, DK_TOT), 0.1),
        "onorm_g": 1.0 + nrm(ks[14], (DEPTH, DV_TOT), 0.02),
        "proj_a": nrm(ks[15], (DEPTH, D_A, D_MODEL), D_A ** -0.5),
        "proj_b": nrm(ks[16], (DEPTH, DV_TOT, D_MODEL), DV_TOT ** -0.5),
        "w_out": nrm(ks[17], (DEPTH, D_MODEL, D_MODEL), D_MODEL ** -0.5),
        "norm2_g": 1.0 + nrm(ks[18], (DEPTH, D_MODEL), 0.02),
        "w_ff1": nrm(ks[19], (DEPTH, D_MODEL, D_FF), D_MODEL ** -0.5),
        "w_ff2": nrm(ks[20], (DEPTH, D_FF, D_MODEL), D_FF ** -0.5),
        "final_g": 1.0 + nrm(ks[21], (D_MODEL,), 0.02),
    }


def reference(x, c, ctx, c_ctx, w_ada, b_ada, norm1_g, w_in, lnv_g, lnv_b, w_s, b_s, w_dec2, b_dec,
              onorm_g, proj_a, proj_b, w_out, norm2_g, w_ff1, w_ff2, final_g):
    B, T, D = x.shape
    x = x + grid_pos_embed(T, D, x.dtype)[None]
    for i in range(DEPTH):
        last = i == DEPTH - 1
        lp = dict(w_in=w_in[i], lnv_g=lnv_g[i], lnv_b=lnv_b[i], w_s=w_s[i], b_s=b_s[i],
                  w_dec2=w_dec2[i], b_dec=b_dec[i], onorm_g=onorm_g[i],
                  proj_a=proj_a[i], proj_b=proj_b[i], w_out=w_out[i])
        mod_x = (jax.nn.silu(c) @ w_ada[i] + b_ada[i])[:, None, :]
        mod_c = (jax.nn.silu(c_ctx) @ w_ada[i] + b_ada[i])[None, None, :]
        sh1x, sc1x, g1x, sh2x, sc2x, g2x = jnp.split(mod_x, 6, axis=-1)
        sh1c, sc1c, g1c, sh2c, sc2c, g2c = jnp.split(mod_c, 6, axis=-1)

        zero = jnp.zeros((B, GLA_HEADS, DK, DV), jnp.float32)
        hc = modulate(ctx, norm1_g[i], sh1c, sc1c)
        yc, s_f, s_b = mixer(hc, zero, zero, lp, not last)
        hx = modulate(x, norm1_g[i], sh1x, sc1x)
        yx, _, _ = mixer(hx, s_f, s_b, lp, True)

        x = x + g1x * yx
        x = x + g2x * sq_relu_mlp(modulate(x, norm2_g[i], sh2x, sc2x), w_ff1[i], w_ff2[i])
        if not last:
            ctx = ctx + g1c * yc
            ctx = ctx + g2c * sq_relu_mlp(modulate(ctx, norm2_g[i], sh2c, sc2c), w_ff1[i], w_ff2[i])
    return rms_norm(x, final_g)
```

```python
import functools
import math

import jax
import jax.numpy as jnp
from jax import lax
from jax.experimental import pallas as pl
from jax.experimental.pallas import tpu as pltpu

F32 = jnp.float32
BF16 = jnp.bfloat16

GRID_W = 64
GLA_HEADS = 4
GATE_TAU = 16.0
EPS = 1e-6

TOKEN_TILE = 256
SCAN_CHUNK = 128
MOD_ROWS = 8
MOD_COL_TILE = 1024
FF_COL_TILE = 1024
SAFE_DECAY = 60.0
VMEM_LIMIT = 56 * 1024 * 1024


def _sigmoid(x):
    return 1.0 / (1.0 + jnp.exp(-x))


def _gelu_tanh(x):
    c = math.sqrt(2.0 / math.pi)
    return 0.5 * x * (1.0 + jnp.tanh(c * (x + 0.044715 * (x * x * x))))


def _log_sigmoid(z):
    return jnp.minimum(z, 0.0) - jnp.log1p(jnp.exp(-jnp.abs(z)))


def _rms(x, g):
    return x * lax.rsqrt(jnp.mean(x * x, axis=-1, keepdims=True) + EPS) * g


def _split_bf16(a):
    hi = a.astype(BF16)
    lo = (a - hi.astype(F32)).astype(BF16)
    return hi, lo


def _dot(a, b):
    return jnp.dot(a, b, preferred_element_type=F32)


def _dot_nt(a, b):
    return lax.dot_general(a, b, (((1,), (1,)), ((), ())), preferred_element_type=F32)


def _dot_tn(a, b):
    return lax.dot_general(a, b, (((0,), (0,)), ((), ())), preferred_element_type=F32)


def _const_spec(shape):
    nd = len(shape)
    return pl.BlockSpec(shape, lambda *_: (0,) * nd, pipeline_mode=pl.Buffered(1))


def _mod_kernel(c_ref, w_ref, b_ref, o_ref):
    c = c_ref[...]
    a_hi, a_lo = _split_bf16(c * _sigmoid(c))
    w_hi, w_lo = _split_bf16(w_ref[...])
    o_ref[...] = _dot(a_hi, w_hi) + _dot(a_lo, w_hi) + _dot(a_hi, w_lo) + b_ref[...]


def _mod_call(c_rows, w_ada, b_ada):
    d, n = w_ada.shape
    return pl.pallas_call(
        _mod_kernel,
        out_shape=jax.ShapeDtypeStruct((MOD_ROWS, n), F32),
        grid=(n // MOD_COL_TILE,),
        in_specs=[pl.BlockSpec((MOD_ROWS, d), lambda j: (0, 0)),
                  pl.BlockSpec((d, MOD_COL_TILE), lambda j: (0, j)),
                  pl.BlockSpec((1, MOD_COL_TILE), lambda j: (0, j))],
        out_specs=pl.BlockSpec((MOD_ROWS, MOD_COL_TILE), lambda j: (0, j)),
        compiler_params=pltpu.CompilerParams(dimension_semantics=("arbitrary",)),
        name="adaln_mod",
    )(c_rows, w_ada, b_ada.reshape(1, n))


def _load_tokens(x_ref, row_ref, col_ref):
    x = x_ref[0]
    if row_ref is not None:
        half = x.shape[-1] // 2
        x = jnp.concatenate([x[..., :half] + row_ref[...], x[..., half:] + col_ref[...]], axis=-1)
    return x.reshape(x.shape[0] * x.shape[1], x.shape[2])


def _inproj_kernel(*refs, add_pos, d_a, dk_tot, dv_tot, rank, groups, chunk_a):
    if add_pos:
        x_ref, row_ref, col_ref, *refs = refs
    else:
        x_ref, *refs = refs
        row_ref = col_ref = None
    (sh_ref, sc_ref, g1_ref, wm_ref, wd_ref, wga_ref, wgb_ref, wdec2_ref, bdec_ref,
     lng_ref, lnb_ref, ws_ref, bs_ref, pa_ref,
     ma_ref, gb_ref, q_ref, k_ref, v_ref, sr_ref, lgf_ref, lgb_ref, ya_scr) = refs

    xp = _load_tokens(x_ref, row_ref, col_ref)
    tm = xp.shape[0]
    hb = (_rms(xp, g1_ref[...]) * (1.0 + sc_ref[0]) + sh_ref[0]).astype(BF16)

    c0 = 0
    gu = _gelu_tanh(_dot(hb, wm_ref[:, c0:c0 + d_a]))
    c0 += d_a
    gv = _gelu_tanh(_dot(hb, wm_ref[:, c0:c0 + d_a]))
    c0 += d_a
    mu = jnp.mean(gv, axis=-1, keepdims=True)
    gc = gv - mu
    var = jnp.mean(gc * gc, axis=-1, keepdims=True)
    vn = (gc * lax.rsqrt(var + EPS) * lng_ref[...] + lnb_ref[...]).astype(BF16)

    gw = d_a // groups
    for ci in range(tm // chunk_a):
        r0 = ci * chunk_a
        for g in range(groups):
            s = _dot(ws_ref[g], vn[r0:r0 + chunk_a, g * gw:(g + 1) * gw]) + bs_ref[g]
            ya_scr[r0:r0 + chunk_a, g * gw:(g + 1) * gw] = (
                gu[r0:r0 + chunk_a, g * gw:(g + 1) * gw] * s).astype(BF16)

    ma_ref[0] = (_sigmoid(_dot(hb, wga_ref[...])) * _dot(ya_scr[...], pa_ref[...])).astype(BF16)
    gb_ref[0] = _sigmoid(_dot(hb, wgb_ref[...])).astype(BF16)

    dk = dk_tot // GLA_HEADS
    q_ref[0] = (_dot(hb, wm_ref[:, c0:c0 + dk_tot]) * dk ** -0.5).astype(BF16)
    c0 += dk_tot
    k_ref[0] = _dot(hb, wm_ref[:, c0:c0 + dk_tot]).astype(BF16)
    c0 += dk_tot
    v_ref[0] = _dot(hb, wm_ref[:, c0:c0 + dv_tot]).astype(BF16)
    c0 += dv_tot
    r = _dot(hb, wm_ref[:, c0:c0 + dv_tot])
    sr_ref[0] = (r * _sigmoid(r)).astype(BF16)

    dec = _dot(hb, wd_ref[...])
    for di, out_ref in enumerate((lgf_ref, lgb_ref)):
        z = _dot(dec[:, di * rank:(di + 1) * rank].astype(BF16), wdec2_ref[di]) + bdec_ref[di]
        out_ref[0] = _log_sigmoid(z) * (1.0 / GATE_TAU)


def _inproj_call(x4, pos, sh, sc, g1, wts, tm):
    bsz, nrow, gw_, d = x4.shape
    t = nrow * gw_
    rows_per_tile = tm // GRID_W
    (w_main, w_dec, w_ga, w_gb, w_dec2, b_dec, lng, lnb, w_s, b_s, proj_a) = wts
    d_a = proj_a.shape[0]
    dk_tot = w_dec2.shape[-1]
    dv_tot = w_main.shape[1] - 2 * d_a - 2 * dk_tot
    dv_tot //= 2
    rank = w_dec2.shape[1]
    groups, chunk_a = w_s.shape[0], w_s.shape[1]
    assert tm % chunk_a == 0 and tm % GRID_W == 0 and t % tm == 0

    tok = lambda w: pl.BlockSpec((1, tm, w), lambda b, i: (b, i, 0))
    in_specs = [pl.BlockSpec((1, rows_per_tile, GRID_W, d), lambda b, i: (b, i, 0, 0))]
    args = [x4]
    if pos is not None:
        rowemb, colemb = pos
        in_specs += [pl.BlockSpec((rows_per_tile, 1, d // 2), lambda b, i: (i, 0, 0)),
                     _const_spec(colemb.shape)]
        args += [rowemb, colemb]
    in_specs += [pl.BlockSpec((1, 1, d), lambda b, i: (b, 0, 0)),
                 pl.BlockSpec((1, 1, d), lambda b, i: (b, 0, 0))]
    args += [sh, sc]
    consts = [g1, w_main, w_dec, w_ga, w_gb, w_dec2, b_dec, lng, lnb, w_s, b_s, proj_a]
    in_specs += [_const_spec(a.shape) for a in consts]
    args += consts

    widths = [d, d, dk_tot, dk_tot, dv_tot, dv_tot, dk_tot, dk_tot]
    dtypes = [BF16] * 6 + [F32] * 2
    kern = functools.partial(_inproj_kernel, add_pos=pos is not None, d_a=d_a, dk_tot=dk_tot,
                             dv_tot=dv_tot, rank=rank, groups=groups, chunk_a=chunk_a)
    return pl.pallas_call(
        kern,
        out_shape=[jax.ShapeDtypeStruct((bsz, t, w), dt) for w, dt in zip(widths, dtypes)],
        grid=(bsz, t // tm),
        in_specs=in_specs,
        out_specs=[tok(w) for w in widths],
        scratch_shapes=[pltpu.VMEM((tm, d_a), BF16)],
        compiler_params=pltpu.CompilerParams(dimension_semantics=("arbitrary", "arbitrary"),
                                             vmem_limit_bytes=VMEM_LIMIT),
        name="inproj",
    )(*args)


def _scan_kernel(qf_ref, kf_ref, vf_ref, lgf_ref, qb_ref, kb_ref, vb_ref, lgb_ref, s0_ref,
                 of_ref, ob_ref, sfin_ref, st_scr, q32_scr, k32_scr, v32_scr, b_scr, o_scr,
                 *, dk, dv):
    t = pl.program_id(1)
    c = qf_ref.shape[1]

    @pl.when(t == 0)
    def _():
        st_scr[...] = s0_ref[0]

    row = lax.broadcasted_iota(jnp.int32, (c, c), 0)
    col = lax.broadcasted_iota(jnp.int32, (c, c), 1)
    dirs = ((qf_ref, kf_ref, vf_ref, lgf_ref, of_ref, row >= col, c - 1),
            (qb_ref, kb_ref, vb_ref, lgb_ref, ob_ref, row <= col, 0))

    cums, totals = [], []
    for (_, _, _, lg_ref, _, keep, end) in dirs:
        hi, lo = _split_bf16(lg_ref[0])
        tri = jnp.where(keep, 1.0, 0.0).astype(BF16)
        b = _dot(tri, hi) + _dot(tri, lo)
        cums.append(b)
        totals.append(b[end:end + 1, :])
    worst = jnp.maximum(jnp.max(-totals[0]), jnp.max(-totals[1]))

    def inter_and_state(di, h, q, k, v, bh, tot):
        st = st_scr[di, h]
        o = _dot_nt((q * jnp.exp(bh)).astype(BF16), st.astype(BF16))
        kd = (k * jnp.exp(tot - bh)).astype(BF16)
        st_scr[di, h] = st * jnp.exp(tot) + _dot_tn(v, kd)
        return o

    def fast_step():
        for di, (q_ref, k_ref, v_ref, _, o_ref, keep, _) in enumerate(dirs):
            for h in range(GLA_HEADS):
                bh = cums[di][:, h * dk:(h + 1) * dk]
                tot = totals[di][:, h * dk:(h + 1) * dk]
                q = q_ref[0, :, h * dk:(h + 1) * dk].astype(F32)
                k = k_ref[0, :, h * dk:(h + 1) * dk].astype(F32)
                v = v_ref[0, :, h * dv:(h + 1) * dv]
                mid = 0.5 * tot
                qe = (q * jnp.exp(bh - mid)).astype(BF16)
                ke = (k * jnp.exp(mid - bh)).astype(BF16)
                a = jnp.where(keep, _dot_nt(qe, ke), 0.0).astype(BF16)
                o = _dot(a, v) + inter_and_state(di, h, q, k, v, bh, tot)
                o_ref[0, :, h * dv:(h + 1) * dv] = o.astype(o_ref.dtype)

    def safe_step():
        rows = lax.broadcasted_iota(jnp.int32, (c, 1), 0)
        for di, (q_ref, k_ref, v_ref, _, o_ref, _, _) in enumerate(dirs):
            for h in range(GLA_HEADS):
                bh = cums[di][:, h * dk:(h + 1) * dk]
                tot = totals[di][:, h * dk:(h + 1) * dk]
                q = q_ref[0, :, h * dk:(h + 1) * dk].astype(F32)
                k = k_ref[0, :, h * dk:(h + 1) * dk].astype(F32)
                v = v_ref[0, :, h * dv:(h + 1) * dv]
                q32_scr[...] = q
                k32_scr[...] = k
                v32_scr[...] = v.astype(F32)
                b_scr[...] = bh
                o_scr[...] = inter_and_state(di, h, q, k, v, bh, tot)

                def body(j, carry):
                    bj = b_scr[pl.ds(j, 1), :]
                    valid = (rows >= j) if di == 0 else (rows <= j)
                    e = jnp.exp(jnp.where(valid, b_scr[...] - bj, -jnp.inf))
                    w = jnp.sum(q32_scr[...] * e * k32_scr[pl.ds(j, 1), :], axis=-1, keepdims=True)
                    o_scr[...] += w * v32_scr[pl.ds(j, 1), :]
                    return carry

                lax.fori_loop(0, c, body, 0)
                o_ref[0, :, h * dv:(h + 1) * dv] = o_scr[...].astype(o_ref.dtype)

    lax.cond(worst <= SAFE_DECAY, fast_step, safe_step)

    @pl.when(t == pl.num_programs(1) - 1)
    def _():
        sfin_ref[0] = st_scr[...]


def _scan_call(q, k, v, lgf, lgb, s0):
    bsz, t, dk_tot = q.shape
    dv_tot = v.shape[-1]
    dk, dv = dk_tot // GLA_HEADS, dv_tot // GLA_HEADS
    c = SCAN_CHUNK
    nt = t // c
    assert t % c == 0
    fwd = lambda w: pl.BlockSpec((1, c, w), lambda b, i: (b, i, 0))
    bwd = lambda w: pl.BlockSpec((1, c, w), lambda b, i: (b, nt - 1 - i, 0))
    st_spec = pl.BlockSpec((1, 2, GLA_HEADS, dv, dk), lambda b, i: (b, 0, 0, 0, 0))
    kern = functools.partial(_scan_kernel, dk=dk, dv=dv)
    return pl.pallas_call(
        kern,
        out_shape=[jax.ShapeDtypeStruct((bsz, t, dv_tot), BF16),
                   jax.ShapeDtypeStruct((bsz, t, dv_tot), BF16),
                   jax.ShapeDtypeStruct((bsz, 2, GLA_HEADS, dv, dk), F32)],
        grid=(bsz, nt),
        in_specs=[fwd(dk_tot), fwd(dk_tot), fwd(dv_tot), fwd(dk_tot),
                  bwd(dk_tot), bwd(dk_tot), bwd(dv_tot), bwd(dk_tot), st_spec],
        out_specs=[fwd(dv_tot), bwd(dv_tot), st_spec],
        scratch_shapes=[pltpu.VMEM((2, GLA_HEADS, dv, dk), F32),
                        pltpu.VMEM((c, dk), F32), pltpu.VMEM((c, dk), F32),
                        pltpu.VMEM((c, dv), F32), pltpu.VMEM((c, dk), F32),
                        pltpu.VMEM((c, dv), F32)],
        compiler_params=pltpu.CompilerParams(dimension_semantics=("arbitrary", "arbitrary"),
                                             vmem_limit_bytes=VMEM_LIMIT),
        name="gla_scan",
    )(q, k, v, lgf, q, k, v, lgb, s0)


def _tail_kernel(x_ref, row_ref, col_ref, ma_ref, gb_ref, of_ref, ob_ref, sr_ref,
                 g1_ref, sh2_ref, sc2_ref, g2_ref, og_ref, n2_ref, fg_ref,
                 pb_ref, wo_ref, w1_ref, w2_ref, out_ref, *, dv):
    xp = _load_tokens(x_ref, row_ref, col_ref)
    o = of_ref[0].astype(F32) + ob_ref[0].astype(F32)
    heads = []
    for h in range(GLA_HEADS):
        sl = slice(h * dv, (h + 1) * dv)
        heads.append(_rms(o[:, sl], og_ref[:, sl]))
    yb = (jnp.concatenate(heads, axis=-1) * sr_ref[0].astype(F32)).astype(BF16)
    merged = ma_ref[0].astype(F32) + gb_ref[0].astype(F32) * _dot(yb, pb_ref[...])
    x1 = xp + g1_ref[0] * _dot(merged.astype(BF16), wo_ref[...])
    h2 = (_rms(x1, n2_ref[...]) * (1.0 + sc2_ref[0]) + sh2_ref[0]).astype(BF16)
    acc = jnp.zeros_like(x1)
    d_ff = w1_ref.shape[1]
    for j in range(d_ff // FF_COL_TILE):
        sl = slice(j * FF_COL_TILE, (j + 1) * FF_COL_TILE)
        f = jnp.maximum(_dot(h2, w1_ref[:, sl]), 0.0)
        acc = acc + _dot((f * f).astype(BF16), w2_ref[sl, :])
    x2 = x1 + g2_ref[0] * acc
    out_ref[0] = _rms(x2, fg_ref[...])


def _tail_call(x4, pos, acts, mods, consts, tm):
    bsz, nrow, gw_, d = x4.shape
    t = nrow * gw_
    rows_per_tile = tm // GRID_W
    rowemb, colemb = pos
    dv = acts[2].shape[-1] // GLA_HEADS
    tok = lambda w: pl.BlockSpec((1, tm, w), lambda b, i: (b, i, 0))
    per_batch = pl.BlockSpec((1, 1, d), lambda b, i: (b, 0, 0))
    in_specs = ([pl.BlockSpec((1, rows_per_tile, GRID_W, d), lambda b, i: (b, i, 0, 0)),
                 pl.BlockSpec((rows_per_tile, 1, d // 2), lambda b, i: (i, 0, 0)),
                 _const_spec(colemb.shape)]
                + [tok(a.shape[-1]) for a in acts]
                + [per_batch] * len(mods)
                + [_const_spec(a.shape) for a in consts])
    return pl.pallas_call(
        functools.partial(_tail_kernel, dv=dv),
        out_shape=jax.ShapeDtypeStruct((bsz, t, d), F32),
        grid=(bsz, t // tm),
        in_specs=in_specs,
        out_specs=tok(d),
        compiler_params=pltpu.CompilerParams(dimension_semantics=("arbitrary", "arbitrary"),
                                             vmem_limit_bytes=VMEM_LIMIT),
        name="tail",
    )(x4, rowemb, colemb, *acts, *mods, *consts)


def _pos_tables(t, d):
    quarter = d // 4
    freqs = 1.0 / (10000.0 ** (jnp.arange(quarter, dtype=F32) / quarter))

    def emb(n):
        a = jnp.arange(n, dtype=F32)[:, None] * freqs[None, :]
        return jnp.concatenate([jnp.sin(a), jnp.cos(a)], -1)

    return emb(t // GRID_W)[:, None, :], emb(GRID_W)[None, :, :]


def kernel(x, c, ctx, c_ctx, w_ada, b_ada, norm1_g, w_in, lnv_g, lnv_b, w_s, b_s, w_dec2, b_dec,
           onorm_g, proj_a, proj_b, w_out, norm2_g, w_ff1, w_ff2, final_g):
    bsz, t, d = x.shape
    depth = w_in.shape[0]
    assert depth == 1, "single-layer problem: the context stream only feeds the scan states"
    d_a = proj_a.shape[1]
    dv_tot = proj_b.shape[1]
    dk_tot = w_dec2.shape[-1]
    rank = w_dec2.shape[2]
    n_main = 2 * d_a + 2 * dk_tot + 2 * dv_tot

    c_rows = jnp.zeros((MOD_ROWS, d), F32).at[:bsz].set(c).at[bsz].set(c_ctx)
    mod = _mod_call(c_rows, w_ada[0], b_ada[0])
    sh1, sc1, g1, sh2, sc2, g2 = [m[:, None, :] for m in jnp.split(mod, 6, axis=-1)]
    latent = lambda m: m[:bsz]
    context = lambda m: jnp.broadcast_to(m[bsz:bsz + 1], (bsz, 1, d))

    wi = w_in[0]
    row = lambda a: a.reshape(1, -1)
    wts = (wi[:, :n_main].astype(BF16), wi[:, n_main:n_main + 2 * rank].astype(BF16),
           wi[:, n_main + 2 * rank:n_main + 2 * rank + d].astype(BF16),
           wi[:, n_main + 2 * rank + d:].astype(BF16),
           w_dec2[0].astype(BF16), b_dec[0][:, None, :], row(lnv_g[0]), row(lnv_b[0]),
           w_s[0].astype(BF16), b_s[0][:, :, None], proj_a[0].astype(BF16))
    g1n = row(norm1_g[0])

    ctx4 = ctx.reshape(bsz, ctx.shape[1] // GRID_W, GRID_W, d)
    cact = _inproj_call(ctx4, None, context(sh1), context(sc1), g1n, wts, TOKEN_TILE)
    zero = jnp.zeros((bsz, 2, GLA_HEADS, dv_tot // GLA_HEADS, dk_tot // GLA_HEADS), F32)
    _, _, s_ctx = _scan_call(cact[2], cact[3], cact[4], cact[6], cact[7], zero)

    pos = _pos_tables(t, d)
    x4 = x.reshape(bsz, t // GRID_W, GRID_W, d)
    ma, gb, q, k, v, sr, lgf, lgb = _inproj_call(x4, pos, latent(sh1), latent(sc1), g1n, wts,
                                                 TOKEN_TILE)
    o_f, o_b, _ = _scan_call(q, k, v, lgf, lgb, s_ctx)

    consts = (row(onorm_g[0]), row(norm2_g[0]), row(final_g), proj_b[0].astype(BF16),
              w_out[0].astype(BF16), w_ff1[0].astype(BF16), w_ff2[0].astype(BF16))
    mods = (latent(g1), latent(sh2), latent(sc2), latent(g2))
    return _tail_call(x4, pos, (ma, gb, o_f, o_b, sr), mods, consts, TOKEN_TILE)
```

```python
import functools
import math

import jax
import jax.numpy as jnp
from jax import lax
from jax.experimental import pallas as pl
from jax.experimental.pallas import tpu as pltpu

F32 = jnp.float32
BF16 = jnp.bfloat16

GRID_W = 64
GLA_HEADS = 4
GATE_TAU = 16.0
EPS = 1e-6

TOKEN_TILE = 512
SCAN_CHUNK = 128
MOD_ROWS = 8
MOD_COL_TILE = 1024
PROJ_COL_TILE = 512
FF_COL_TILE = 1024
SAFE_DECAY = 60.0
VMEM_LIMIT = 56 * 1024 * 1024


def _sigmoid(x):
    return 1.0 / (1.0 + jnp.exp(-x))


def _gelu_tanh(x):
    c = math.sqrt(2.0 / math.pi)
    return 0.5 * x * (1.0 + jnp.tanh(c * (x + 0.044715 * (x * x * x))))


def _log_sigmoid(z):
    return jnp.minimum(z, 0.0) - jnp.log(1.0 + jnp.exp(-jnp.abs(z)))


def _rms(x, g):
    return x * lax.rsqrt(jnp.mean(x * x, axis=-1, keepdims=True) + EPS) * g


def _split_bf16(a):
    hi = a.astype(BF16)
    lo = (a - hi.astype(F32)).astype(BF16)
    return hi, lo


def _dot(a, b):
    return jnp.dot(a, b, preferred_element_type=F32)


def _dot_nt(a, b):
    return lax.dot_general(a, b, (((1,), (1,)), ((), ())), preferred_element_type=F32)


def _dot_tn(a, b):
    return lax.dot_general(a, b, (((0,), (0,)), ((), ())), preferred_element_type=F32)


def _const_spec(shape):
    nd = len(shape)
    return pl.BlockSpec(shape, lambda *_: (0,) * nd, pipeline_mode=pl.Buffered(1))


def _mod_kernel(c_ref, w_ref, b_ref, o_ref):
    c = c_ref[...]
    a_hi, a_lo = _split_bf16(c * _sigmoid(c))
    w_hi, w_lo = _split_bf16(w_ref[...])
    o_ref[...] = _dot(a_hi, w_hi) + _dot(a_lo, w_hi) + _dot(a_hi, w_lo) + b_ref[...]


def _mod_call(c_rows, w_ada, b_ada):
    d, n = w_ada.shape
    return pl.pallas_call(
        _mod_kernel,
        out_shape=jax.ShapeDtypeStruct((MOD_ROWS, n), F32),
        grid=(n // MOD_COL_TILE,),
        in_specs=[pl.BlockSpec((MOD_ROWS, d), lambda j: (0, 0)),
                  pl.BlockSpec((d, MOD_COL_TILE), lambda j: (0, j)),
                  pl.BlockSpec((1, MOD_COL_TILE), lambda j: (0, j))],
        out_specs=pl.BlockSpec((MOD_ROWS, MOD_COL_TILE), lambda j: (0, j)),
        compiler_params=pltpu.CompilerParams(dimension_semantics=("arbitrary",)),
        name="adaln_mod",
    )(c_rows, w_ada, b_ada.reshape(1, n))


def _load_tokens(x_ref, row_ref, col_ref):
    x = x_ref[0]
    if row_ref is not None:
        half = x.shape[-1] // 2
        x = jnp.concatenate([x[..., :half] + row_ref[...], x[..., half:] + col_ref[...]], axis=-1)
    return x.reshape(x.shape[0] * x.shape[1], x.shape[2])


def _inproj_kernel(*refs, add_pos, d_a, dk_tot, dv_tot, rank, groups, chunk_a):
    if add_pos:
        x_ref, row_ref, col_ref, *refs = refs
    else:
        x_ref, *refs = refs
        row_ref = col_ref = None
    (sh_ref, sc_ref, g1_ref, wt_ref, wdec2_ref, bdec_ref,
     lng_ref, lnb_ref, ws_ref, bs_ref, pa_ref,
     ma_ref, gb_ref, q_ref, k_ref, v_ref, sr_ref, lgf_ref, lgb_ref, ya_scr) = refs

    xp = _load_tokens(x_ref, row_ref, col_ref)
    tm = xp.shape[0]
    d = xp.shape[1]
    hb = (_rms(xp, g1_ref[...]) * (1.0 + sc_ref[0]) + sh_ref[0]).astype(BF16)

    def proj(r0, width):
        return _dot_nt(hb, wt_ref[r0:r0 + width, :])

    r_u, r_va = 0, d_a
    r_q = 2 * d_a
    r_k = r_q + dk_tot
    r_v = r_k + dk_tot
    r_r = r_v + dv_tot
    r_dec = r_r + dv_tot
    r_ga = r_dec + 2 * rank
    r_gb = r_ga + d
    dk = dk_tot // GLA_HEADS
    gw = d_a // groups
    blk = PROJ_COL_TILE
    cols = lambda j: slice(j * blk, (j + 1) * blk)
    gu, gv, vn = {}, {}, {}

    stages = []

    def ep_dec(dec):
        for di, out_ref in enumerate((lgf_ref, lgb_ref)):
            z = _dot(dec[:, di * rank:(di + 1) * rank].astype(BF16), wdec2_ref[di]) + bdec_ref[di]
            out_ref[0] = _log_sigmoid(z) * (1.0 / GATE_TAU)
    stages.append((lambda: proj(r_dec, 2 * rank), ep_dec))

    def store(out_ref, j, fn):
        def ep(y):
            out_ref[0, :, cols(j)] = fn(y).astype(out_ref.dtype)
        return ep

    def keep(table, j, fn):
        def ep(y):
            table[j] = fn(y)
        return ep

    for j in range(d_a // blk):
        stages.append((lambda j=j: proj(r_u + j * blk, blk), keep(gu, j, _gelu_tanh)))
    for j in range(d_a // blk):
        stages.append((lambda j=j: proj(r_va + j * blk, blk), keep(gv, j, _gelu_tanh)))

    def ep_q_and_layernorm(y):
        q_ref[0] = (y * dk ** -0.5).astype(BF16)
        nb = d_a // blk
        mu = sum(jnp.sum(gv[j], axis=-1, keepdims=True) for j in range(nb)) * (1.0 / d_a)
        cen = [gv[j] - mu for j in range(nb)]
        var = sum(jnp.sum(c * c, axis=-1, keepdims=True) for c in cen) * (1.0 / d_a)
        inv = lax.rsqrt(var + EPS)
        for j in range(nb):
            vn[j] = (cen[j] * inv * lng_ref[:, cols(j)] + lnb_ref[:, cols(j)]).astype(BF16)
    stages.append((lambda: proj(r_q, dk_tot), ep_q_and_layernorm))
    stages.append((lambda: proj(r_k, dk_tot), store(k_ref, 0, lambda y: y)))
    for j in range(dv_tot // blk):
        stages.append((lambda j=j: proj(r_r + j * blk, blk),
                       store(sr_ref, j, lambda y: y * _sigmoid(y))))

    def mm_gate(ci, g):
        r0, c0 = ci * chunk_a, g * gw
        return _dot(ws_ref[g], vn[c0 // blk][r0:r0 + chunk_a, c0 % blk:c0 % blk + gw])

    def ep_gate(ci, g):
        r0, c0 = ci * chunk_a, g * gw

        def ep(s):
            u = gu[c0 // blk][r0:r0 + chunk_a, c0 % blk:c0 % blk + gw]
            ya_scr[r0:r0 + chunk_a, c0:c0 + gw] = (u * (s + bs_ref[g])).astype(BF16)
        return ep
    for ci in range(tm // chunk_a):
        for g in range(groups):
            stages.append((lambda ci=ci, g=g: mm_gate(ci, g), ep_gate(ci, g)))

    for j in range(d // blk):
        stages.append((lambda j=j: proj(r_gb + j * blk, blk), store(gb_ref, j, _sigmoid)))
    for j in range(d // blk):
        stages.append((lambda j=j: (proj(r_ga + j * blk, blk),
                                    _dot(ya_scr[...], pa_ref[:, cols(j)])),
                       store(ma_ref, j, lambda y: _sigmoid(y[0]) * y[1])))
    for j in range(dv_tot // blk):
        stages.append((lambda j=j: proj(r_v + j * blk, blk), store(v_ref, j, lambda y: y)))

    pending = stages[0][0]()
    for i, (_, epilogue) in enumerate(stages):
        nxt = stages[i + 1][0]() if i + 1 < len(stages) else None
        epilogue(pending)
        pending = nxt


def _inproj_call(x4, pos, sh, sc, g1, wts, tm):
    bsz, nrow, gw_, d = x4.shape
    t = nrow * gw_
    rows_per_tile = tm // GRID_W
    (w_in_t, w_dec2, b_dec, lng, lnb, w_s, b_s, proj_a) = wts
    d_a = proj_a.shape[0]
    dk_tot = w_dec2.shape[-1]
    rank = w_dec2.shape[1]
    dv_tot = (w_in_t.shape[0] - 2 * d_a - 2 * dk_tot - 2 * rank - 2 * d) // 2
    groups, chunk_a = w_s.shape[0], w_s.shape[1]
    assert tm % chunk_a == 0 and tm % GRID_W == 0 and t % tm == 0

    tok = lambda w: pl.BlockSpec((1, tm, w), lambda b, i: (b, i, 0))
    in_specs = [pl.BlockSpec((1, rows_per_tile, GRID_W, d), lambda b, i: (b, i, 0, 0))]
    args = [x4]
    if pos is not None:
        rowemb, colemb = pos
        in_specs += [pl.BlockSpec((rows_per_tile, 1, d // 2), lambda b, i: (i, 0, 0)),
                     _const_spec(colemb.shape)]
        args += [rowemb, colemb]
    in_specs += [pl.BlockSpec((1, 1, d), lambda b, i: (b, 0, 0)),
                 pl.BlockSpec((1, 1, d), lambda b, i: (b, 0, 0))]
    args += [sh, sc]
    consts = [g1, w_in_t, w_dec2, b_dec, lng, lnb, w_s, b_s, proj_a]
    in_specs += [_const_spec(a.shape) for a in consts]
    args += consts

    widths = [d, d, dk_tot, dk_tot, dv_tot, dv_tot, dk_tot, dk_tot]
    dtypes = [BF16] * 6 + [F32] * 2
    kern = functools.partial(_inproj_kernel, add_pos=pos is not None, d_a=d_a, dk_tot=dk_tot,
                             dv_tot=dv_tot, rank=rank, groups=groups, chunk_a=chunk_a)
    return pl.pallas_call(
        kern,
        out_shape=[jax.ShapeDtypeStruct((bsz, t, w), dt) for w, dt in zip(widths, dtypes)],
        grid=(bsz, t // tm),
        in_specs=in_specs,
        out_specs=[tok(w) for w in widths],
        scratch_shapes=[pltpu.VMEM((tm, d_a), BF16)],
        compiler_params=pltpu.CompilerParams(dimension_semantics=("arbitrary", "arbitrary"),
                                             vmem_limit_bytes=VMEM_LIMIT),
        name="inproj",
    )(*args)


def _scan_kernel(qf_ref, kf_ref, vf_ref, lgf_ref, qb_ref, kb_ref, vb_ref, lgb_ref, s0_ref,
                 of_ref, ob_ref, sfin_ref, st_scr, q32_scr, k32_scr, v32_scr, b_scr, o_scr,
                 *, dk, dv):
    t = pl.program_id(1)
    c = qf_ref.shape[1]

    @pl.when(t == 0)
    def _():
        st_scr[...] = s0_ref[0]

    row = lax.broadcasted_iota(jnp.int32, (c, c), 0)
    col = lax.broadcasted_iota(jnp.int32, (c, c), 1)
    dirs = ((qf_ref, kf_ref, vf_ref, lgf_ref, of_ref, row >= col, c - 1),
            (qb_ref, kb_ref, vb_ref, lgb_ref, ob_ref, row <= col, 0))

    cums, totals = [], []
    for (_, _, _, lg_ref, _, keep, end) in dirs:
        hi, lo = _split_bf16(lg_ref[0])
        tri = jnp.where(keep, 1.0, 0.0).astype(BF16)
        b = _dot(tri, hi) + _dot(tri, lo)
        cums.append(b)
        totals.append(b[end:end + 1, :])
    worst = jnp.maximum(jnp.max(-totals[0]), jnp.max(-totals[1]))

    def inter_and_state(di, h, q, k, v, bh, tot):
        st = st_scr[di, h]
        o = _dot_nt((q * jnp.exp(bh)).astype(BF16), st.astype(BF16))
        kd = (k * jnp.exp(tot - bh)).astype(BF16)
        st_scr[di, h] = st * jnp.exp(tot) + _dot_tn(v, kd)
        return o

    def fast_step():
        for di, (q_ref, k_ref, v_ref, _, o_ref, keep, _) in enumerate(dirs):
            for h in range(GLA_HEADS):
                bh = cums[di][:, h * dk:(h + 1) * dk]
                tot = totals[di][:, h * dk:(h + 1) * dk]
                q = q_ref[0, :, h * dk:(h + 1) * dk].astype(F32)
                k = k_ref[0, :, h * dk:(h + 1) * dk].astype(F32)
                v = v_ref[0, :, h * dv:(h + 1) * dv]
                mid = 0.5 * tot
                qe = (q * jnp.exp(bh - mid)).astype(BF16)
                ke = (k * jnp.exp(mid - bh)).astype(BF16)
                a = jnp.where(keep, _dot_nt(qe, ke), 0.0).astype(BF16)
                o = _dot(a, v) + inter_and_state(di, h, q, k, v, bh, tot)
                o_ref[0, :, h * dv:(h + 1) * dv] = o.astype(o_ref.dtype)

    def safe_step():
        rows = lax.broadcasted_iota(jnp.int32, (c, 1), 0)
        for di, (q_ref, k_ref, v_ref, _, o_ref, _, _) in enumerate(dirs):
            for h in range(GLA_HEADS):
                bh = cums[di][:, h * dk:(h + 1) * dk]
                tot = totals[di][:, h * dk:(h + 1) * dk]
                q = q_ref[0, :, h * dk:(h + 1) * dk].astype(F32)
                k = k_ref[0, :, h * dk:(h + 1) * dk].astype(F32)
                v = v_ref[0, :, h * dv:(h + 1) * dv]
                q32_scr[...] = q
                k32_scr[...] = k
                v32_scr[...] = v.astype(F32)
                b_scr[...] = bh
                o_scr[...] = inter_and_state(di, h, q, k, v, bh, tot)

                def body(j, carry):
                    bj = b_scr[pl.ds(j, 1), :]
                    valid = (rows >= j) if di == 0 else (rows <= j)
                    e = jnp.exp(jnp.where(valid, b_scr[...] - bj, -jnp.inf))
                    w = jnp.sum(q32_scr[...] * e * k32_scr[pl.ds(j, 1), :], axis=-1, keepdims=True)
                    o_scr[...] += w * v32_scr[pl.ds(j, 1), :]
                    return carry

                lax.fori_loop(0, c, body, 0)
                o_ref[0, :, h * dv:(h + 1) * dv] = o_scr[...].astype(o_ref.dtype)

    lax.cond(worst <= SAFE_DECAY, fast_step, safe_step)

    @pl.when(t == pl.num_programs(1) - 1)
    def _():
        sfin_ref[0] = st_scr[...]


def _scan_call(q, k, v, lgf, lgb, s0):
    bsz, t, dk_tot = q.shape
    dv_tot = v.shape[-1]
    dk, dv = dk_tot // GLA_HEADS, dv_tot // GLA_HEADS
    c = SCAN_CHUNK
    nt = t // c
    assert t % c == 0
    fwd = lambda w: pl.BlockSpec((1, c, w), lambda b, i: (b, i, 0))
    bwd = lambda w: pl.BlockSpec((1, c, w), lambda b, i: (b, nt - 1 - i, 0))
    st_spec = pl.BlockSpec((1, 2, GLA_HEADS, dv, dk), lambda b, i: (b, 0, 0, 0, 0))
    kern = functools.partial(_scan_kernel, dk=dk, dv=dv)
    return pl.pallas_call(
        kern,
        out_shape=[jax.ShapeDtypeStruct((bsz, t, dv_tot), BF16),
                   jax.ShapeDtypeStruct((bsz, t, dv_tot), BF16),
                   jax.ShapeDtypeStruct((bsz, 2, GLA_HEADS, dv, dk), F32)],
        grid=(bsz, nt),
        in_specs=[fwd(dk_tot), fwd(dk_tot), fwd(dv_tot), fwd(dk_tot),
                  bwd(dk_tot), bwd(dk_tot), bwd(dv_tot), bwd(dk_tot), st_spec],
        out_specs=[fwd(dv_tot), bwd(dv_tot), st_spec],
        scratch_shapes=[pltpu.VMEM((2, GLA_HEADS, dv, dk), F32),
                        pltpu.VMEM((c, dk), F32), pltpu.VMEM((c, dk), F32),
                        pltpu.VMEM((c, dv), F32), pltpu.VMEM((c, dk), F32),
                        pltpu.VMEM((c, dv), F32)],
        compiler_params=pltpu.CompilerParams(dimension_semantics=("arbitrary", "arbitrary"),
                                             vmem_limit_bytes=VMEM_LIMIT),
        name="gla_scan",
    )(q, k, v, lgf, q, k, v, lgb, s0)


def _tail_kernel(x_ref, row_ref, col_ref, ma_ref, gb_ref, of_ref, ob_ref, sr_ref,
                 g1_ref, sh2_ref, sc2_ref, g2_ref, og_ref, n2_ref, fg_ref,
                 pb_ref, wo_ref, w1_ref, w2_ref, out_ref, *, dv):
    xp = _load_tokens(x_ref, row_ref, col_ref)
    o = of_ref[0].astype(F32) + ob_ref[0].astype(F32)
    heads = []
    for h in range(GLA_HEADS):
        sl = slice(h * dv, (h + 1) * dv)
        heads.append(_rms(o[:, sl], og_ref[:, sl]))
    yb = (jnp.concatenate(heads, axis=-1) * sr_ref[0].astype(F32)).astype(BF16)
    merged = ma_ref[0].astype(F32) + gb_ref[0].astype(F32) * _dot(yb, pb_ref[...])
    x1 = xp + g1_ref[0] * _dot(merged.astype(BF16), wo_ref[...])
    h2 = (_rms(x1, n2_ref[...]) * (1.0 + sc2_ref[0]) + sh2_ref[0]).astype(BF16)
    acc = jnp.zeros_like(x1)
    d_ff = w1_ref.shape[1]
    for j in range(d_ff // FF_COL_TILE):
        sl = slice(j * FF_COL_TILE, (j + 1) * FF_COL_TILE)
        f = jnp.maximum(_dot(h2, w1_ref[:, sl]), 0.0)
        acc = acc + _dot((f * f).astype(BF16), w2_ref[sl, :])
    x2 = x1 + g2_ref[0] * acc
    out_ref[0] = _rms(x2, fg_ref[...])


def _tail_call(x4, pos, acts, mods, consts, tm):
    bsz, nrow, gw_, d = x4.shape
    t = nrow * gw_
    rows_per_tile = tm // GRID_W
    rowemb, colemb = pos
    dv = acts[2].shape[-1] // GLA_HEADS
    tok = lambda w: pl.BlockSpec((1, tm, w), lambda b, i: (b, i, 0))
    per_batch = pl.BlockSpec((1, 1, d), lambda b, i: (b, 0, 0))
    in_specs = ([pl.BlockSpec((1, rows_per_tile, GRID_W, d), lambda b, i: (b, i, 0, 0)),
                 pl.BlockSpec((rows_per_tile, 1, d // 2), lambda b, i: (i, 0, 0)),
                 _const_spec(colemb.shape)]
                + [tok(a.shape[-1]) for a in acts]
                + [per_batch] * len(mods)
                + [_const_spec(a.shape) for a in consts])
    return pl.pallas_call(
        functools.partial(_tail_kernel, dv=dv),
        out_shape=jax.ShapeDtypeStruct((bsz, t, d), F32),
        grid=(bsz, t // tm),
        in_specs=in_specs,
        out_specs=tok(d),
        compiler_params=pltpu.CompilerParams(dimension_semantics=("arbitrary", "arbitrary"),
                                             vmem_limit_bytes=VMEM_LIMIT),
        name="tail",
    )(x4, rowemb, colemb, *acts, *mods, *consts)


def _pos_tables(t, d):
    quarter = d // 4
    freqs = 1.0 / (10000.0 ** (jnp.arange(quarter, dtype=F32) / quarter))

    def emb(n):
        a = jnp.arange(n, dtype=F32)[:, None] * freqs[None, :]
        return jnp.concatenate([jnp.sin(a), jnp.cos(a)], -1)

    return emb(t // GRID_W)[:, None, :], emb(GRID_W)[None, :, :]


def kernel(x, c, ctx, c_ctx, w_ada, b_ada, norm1_g, w_in, lnv_g, lnv_b, w_s, b_s, w_dec2, b_dec,
           onorm_g, proj_a, proj_b, w_out, norm2_g, w_ff1, w_ff2, final_g):
    bsz, t, d = x.shape
    depth = w_in.shape[0]
    assert depth == 1, "single-layer problem: the context stream only feeds the scan states"
    dv_tot = proj_b.shape[1]
    dk_tot = w_dec2.shape[-1]

    c_rows = jnp.zeros((MOD_ROWS, d), F32).at[:bsz].set(c).at[bsz].set(c_ctx)
    mod = _mod_call(c_rows, w_ada[0], b_ada[0])
    sh1, sc1, g1, sh2, sc2, g2 = [m[:, None, :] for m in jnp.split(mod, 6, axis=-1)]
    latent = lambda m: m[:bsz]
    context = lambda m: jnp.broadcast_to(m[bsz:bsz + 1], (bsz, 1, d))

    row = lambda a: a.reshape(1, -1)
    wts = (jnp.swapaxes(w_in[0], 0, 1).astype(BF16),
           w_dec2[0].astype(BF16), b_dec[0][:, None, :], row(lnv_g[0]), row(lnv_b[0]),
           w_s[0].astype(BF16), b_s[0][:, :, None], proj_a[0].astype(BF16))
    g1n = row(norm1_g[0])

    ctx4 = ctx.reshape(bsz, ctx.shape[1] // GRID_W, GRID_W, d)
    cact = _inproj_call(ctx4, None, context(sh1), context(sc1), g1n, wts,
                        min(TOKEN_TILE, ctx.shape[1]))
    zero = jnp.zeros((bsz, 2, GLA_HEADS, dv_tot // GLA_HEADS, dk_tot // GLA_HEADS), F32)
    _, _, s_ctx = _scan_call(cact[2], cact[3], cact[4], cact[6], cact[7], zero)

    pos = _pos_tables(t, d)
    x4 = x.reshape(bsz, t // GRID_W, GRID_W, d)
    ma, gb, q, k, v, sr, lgf, lgb = _inproj_call(x4, pos, latent(sh1), latent(sc1), g1n, wts,
                                                 TOKEN_TILE)
    o_f, o_b, _ = _scan_call(q, k, v, lgf, lgb, s_ctx)

    consts = (row(onorm_g[0]), row(norm2_g[0]), row(final_g), proj_b[0].astype(BF16),
              w_out[0].astype(BF16), w_ff1[0].astype(BF16), w_ff2[0].astype(BF16))
    mods = (latent(g1), latent(sh2), latent(sc2), latent(g2))
    return _tail_call(x4, pos, (ma, gb, o_f, o_b, sr), mods, consts, TOKEN_TILE)
```

```python
import functools
import math

import jax
import jax.numpy as jnp
from jax import lax
from jax.experimental import pallas as pl
from jax.experimental.pallas import tpu as pltpu

F32 = jnp.float32
BF16 = jnp.bfloat16

GRID_W = 64
GLA_HEADS = 4
GATE_TAU = 16.0
EPS = 1e-6

TOKEN_TILE = 512
SCAN_CHUNK = 128
SCAN_TILE = 256
MOD_ROWS = 8
MOD_COL_TILE = 1024
PROJ_COL_TILE = 512
FF_COL_TILE = 1024
SAFE_DECAY = 60.0
VMEM_LIMIT = 56 * 1024 * 1024


def _sigmoid(x):
    return 1.0 / (1.0 + jnp.exp(-x))


def _gelu_tanh(x):
    c = math.sqrt(2.0 / math.pi)
    return 0.5 * x * (1.0 + jnp.tanh(c * (x + 0.044715 * (x * x * x))))


def _log_sigmoid(z):
    return jnp.minimum(z, 0.0) - jnp.log(1.0 + jnp.exp(-jnp.abs(z)))


def _rms(x, g):
    return x * lax.rsqrt(jnp.mean(x * x, axis=-1, keepdims=True) + EPS) * g


def _split_bf16(a):
    hi = a.astype(BF16)
    lo = (a - hi.astype(F32)).astype(BF16)
    return hi, lo


def _dot(a, b):
    return jnp.dot(a, b, preferred_element_type=F32)


def _dot_nt(a, b):
    return lax.dot_general(a, b, (((1,), (1,)), ((), ())), preferred_element_type=F32)


def _dot_tn(a, b):
    return lax.dot_general(a, b, (((0,), (0,)), ((), ())), preferred_element_type=F32)


def _const_spec(shape):
    nd = len(shape)
    return pl.BlockSpec(shape, lambda *_: (0,) * nd, pipeline_mode=pl.Buffered(1))


def _mod_kernel(c_ref, w_ref, b_ref, o_ref):
    c = c_ref[...]
    a_hi, a_lo = _split_bf16(c * _sigmoid(c))
    w_hi, w_lo = _split_bf16(w_ref[...])
    o_ref[...] = _dot(a_hi, w_hi) + _dot(a_lo, w_hi) + _dot(a_hi, w_lo) + b_ref[...]


def _mod_call(c_rows, w_ada, b_ada):
    d, n = w_ada.shape
    return pl.pallas_call(
        _mod_kernel,
        out_shape=jax.ShapeDtypeStruct((MOD_ROWS, n), F32),
        grid=(n // MOD_COL_TILE,),
        in_specs=[pl.BlockSpec((MOD_ROWS, d), lambda j: (0, 0)),
                  pl.BlockSpec((d, MOD_COL_TILE), lambda j: (0, j)),
                  pl.BlockSpec((1, MOD_COL_TILE), lambda j: (0, j))],
        out_specs=pl.BlockSpec((MOD_ROWS, MOD_COL_TILE), lambda j: (0, j)),
        compiler_params=pltpu.CompilerParams(dimension_semantics=("arbitrary",)),
        name="adaln_mod",
    )(c_rows, w_ada, b_ada.reshape(1, n))


def _load_tokens(x_ref, row_ref, col_ref):
    x = x_ref[0]
    if row_ref is not None:
        half = x.shape[-1] // 2
        x = jnp.concatenate([x[..., :half] + row_ref[...], x[..., half:] + col_ref[...]], axis=-1)
    return x.reshape(x.shape[0] * x.shape[1], x.shape[2])


def _inproj_kernel(*refs, add_pos, d_a, dk_tot, dv_tot, rank, groups, chunk_a):
    if add_pos:
        x_ref, row_ref, col_ref, *refs = refs
    else:
        x_ref, *refs = refs
        row_ref = col_ref = None
    (sh_ref, sc_ref, g1_ref, wt_ref, wdec2_ref, bdec_ref,
     lng_ref, lnb_ref, ws_ref, bs_ref, pa_ref,
     ma_ref, gb_ref, q_ref, k_ref, v_ref, sr_ref, lgf_ref, lgb_ref, ya_scr) = refs

    xp = _load_tokens(x_ref, row_ref, col_ref)
    tm = xp.shape[0]
    d = xp.shape[1]
    hb = (_rms(xp, g1_ref[...]) * (1.0 + sc_ref[0]) + sh_ref[0]).astype(BF16)

    def proj(r0, width):
        return _dot_nt(hb, wt_ref[r0:r0 + width, :])

    r_u, r_va = 0, d_a
    r_q = 2 * d_a
    r_k = r_q + dk_tot
    r_v = r_k + dk_tot
    r_r = r_v + dv_tot
    r_dec = r_r + dv_tot
    r_ga = r_dec + 2 * rank
    r_gb = r_ga + d
    dk = dk_tot // GLA_HEADS
    gw = d_a // groups
    blk = PROJ_COL_TILE
    cols = lambda j: slice(j * blk, (j + 1) * blk)
    gu, gv, vn = {}, {}, {}

    stages = []

    def ep_dec(dec):
        for di, out_ref in enumerate((lgf_ref, lgb_ref)):
            z = _dot(dec[:, di * rank:(di + 1) * rank].astype(BF16), wdec2_ref[di]) + bdec_ref[di]
            out_ref[0] = _log_sigmoid(z) * (1.0 / GATE_TAU)
    stages.append((lambda: proj(r_dec, 2 * rank), ep_dec))

    def store(out_ref, j, fn):
        def ep(y):
            out_ref[0, :, cols(j)] = fn(y).astype(out_ref.dtype)
        return ep

    def keep(table, j, fn):
        def ep(y):
            table[j] = fn(y)
        return ep

    for j in range(d_a // blk):
        stages.append((lambda j=j: proj(r_u + j * blk, blk), keep(gu, j, _gelu_tanh)))
    for j in range(d_a // blk):
        stages.append((lambda j=j: proj(r_va + j * blk, blk), keep(gv, j, _gelu_tanh)))

    def ep_q_and_layernorm(y):
        q_ref[0] = (y * dk ** -0.5).astype(BF16)
        nb = d_a // blk
        mu = sum(jnp.sum(gv[j], axis=-1, keepdims=True) for j in range(nb)) * (1.0 / d_a)
        cen = [gv[j] - mu for j in range(nb)]
        var = sum(jnp.sum(c * c, axis=-1, keepdims=True) for c in cen) * (1.0 / d_a)
        inv = lax.rsqrt(var + EPS)
        for j in range(nb):
            vn[j] = (cen[j] * inv * lng_ref[:, cols(j)] + lnb_ref[:, cols(j)]).astype(BF16)
    stages.append((lambda: proj(r_q, dk_tot), ep_q_and_layernorm))
    stages.append((lambda: proj(r_k, dk_tot), store(k_ref, 0, lambda y: y)))
    for j in range(dv_tot // blk):
        stages.append((lambda j=j: proj(r_r + j * blk, blk),
                       store(sr_ref, j, lambda y: y * _sigmoid(y))))

    def mm_gate(ci, g):
        r0, c0 = ci * chunk_a, g * gw
        return _dot(ws_ref[g], vn[c0 // blk][r0:r0 + chunk_a, c0 % blk:c0 % blk + gw])

    def ep_gate(ci, g):
        r0, c0 = ci * chunk_a, g * gw

        def ep(s):
            u = gu[c0 // blk][r0:r0 + chunk_a, c0 % blk:c0 % blk + gw]
            ya_scr[r0:r0 + chunk_a, c0:c0 + gw] = (u * (s + bs_ref[g])).astype(BF16)
        return ep
    for ci in range(tm // chunk_a):
        for g in range(groups):
            stages.append((lambda ci=ci, g=g: mm_gate(ci, g), ep_gate(ci, g)))

    for j in range(d // blk):
        stages.append((lambda j=j: proj(r_gb + j * blk, blk), store(gb_ref, j, _sigmoid)))
    for j in range(d // blk):
        stages.append((lambda j=j: (proj(r_ga + j * blk, blk),
                                    _dot(ya_scr[...], pa_ref[:, cols(j)])),
                       store(ma_ref, j, lambda y: _sigmoid(y[0]) * y[1])))
    for j in range(dv_tot // blk):
        stages.append((lambda j=j: proj(r_v + j * blk, blk), store(v_ref, j, lambda y: y)))

    pending = stages[0][0]()
    for i, (_, epilogue) in enumerate(stages):
        nxt = stages[i + 1][0]() if i + 1 < len(stages) else None
        epilogue(pending)
        pending = nxt


def _inproj_call(x4, pos, sh, sc, g1, wts, tm):
    bsz, nrow, gw_, d = x4.shape
    t = nrow * gw_
    rows_per_tile = tm // GRID_W
    (w_in_t, w_dec2, b_dec, lng, lnb, w_s, b_s, proj_a) = wts
    d_a = proj_a.shape[0]
    dk_tot = w_dec2.shape[-1]
    rank = w_dec2.shape[1]
    dv_tot = (w_in_t.shape[0] - 2 * d_a - 2 * dk_tot - 2 * rank - 2 * d) // 2
    groups, chunk_a = w_s.shape[0], w_s.shape[1]
    assert tm % chunk_a == 0 and tm % GRID_W == 0 and t % tm == 0

    tok = lambda w: pl.BlockSpec((1, tm, w), lambda b, i: (b, i, 0))
    in_specs = [pl.BlockSpec((1, rows_per_tile, GRID_W, d), lambda b, i: (b, i, 0, 0))]
    args = [x4]
    if pos is not None:
        rowemb, colemb = pos
        in_specs += [pl.BlockSpec((rows_per_tile, 1, d // 2), lambda b, i: (i, 0, 0)),
                     _const_spec(colemb.shape)]
        args += [rowemb, colemb]
    in_specs += [pl.BlockSpec((1, 1, d), lambda b, i: (b, 0, 0)),
                 pl.BlockSpec((1, 1, d), lambda b, i: (b, 0, 0))]
    args += [sh, sc]
    consts = [g1, w_in_t, w_dec2, b_dec, lng, lnb, w_s, b_s, proj_a]
    in_specs += [_const_spec(a.shape) for a in consts]
    args += consts

    widths = [d, d, dk_tot, dk_tot, dv_tot, dv_tot, dk_tot, dk_tot]
    dtypes = [BF16] * 6 + [F32] * 2
    kern = functools.partial(_inproj_kernel, add_pos=pos is not None, d_a=d_a, dk_tot=dk_tot,
                             dv_tot=dv_tot, rank=rank, groups=groups, chunk_a=chunk_a)
    return pl.pallas_call(
        kern,
        out_shape=[jax.ShapeDtypeStruct((bsz, t, w), dt) for w, dt in zip(widths, dtypes)],
        grid=(bsz, t // tm),
        in_specs=in_specs,
        out_specs=[tok(w) for w in widths],
        scratch_shapes=[pltpu.VMEM((tm, d_a), BF16)],
        compiler_params=pltpu.CompilerParams(dimension_semantics=("arbitrary", "arbitrary"),
                                             vmem_limit_bytes=VMEM_LIMIT),
        name="inproj",
    )(*args)


def _scan_kernel(qf_ref, kf_ref, vf_ref, lgf_ref, qb_ref, kb_ref, vb_ref, lgb_ref, s0_ref,
                 of_ref, ob_ref, sfin_ref, st_scr, q32_scr, k32_scr, v32_scr, b_scr, o_scr,
                 *, dk, dv):
    t = pl.program_id(0)
    bsz, c = qf_ref.shape[0], SCAN_CHUNK
    n_sub = qf_ref.shape[1] // c

    def rows(di, sub):
        first = (sub if di == 0 else n_sub - 1 - sub) * c
        return slice(first, first + c)

    @pl.when(t == 0)
    def _():
        st_scr[...] = s0_ref[...]

    row = lax.broadcasted_iota(jnp.int32, (c, c), 0)
    col = lax.broadcasted_iota(jnp.int32, (c, c), 1)
    dirs = ((qf_ref, kf_ref, vf_ref, lgf_ref, of_ref, row >= col, c - 1),
            (qb_ref, kb_ref, vb_ref, lgb_ref, ob_ref, row <= col, 0))

    worst = None
    for lg_ref in (lgf_ref, lgb_ref):
        for sub in range(n_sub):
            w = jnp.max(-jnp.sum(lg_ref[:, sub * c:(sub + 1) * c, :], axis=1))
            worst = w if worst is None else jnp.maximum(worst, w)

    cums = {}

    def cumulate(bi, di, sub):
        lg_ref, keep, end = dirs[di][3], dirs[di][5], dirs[di][6]
        tri = jnp.where(keep, 1.0, 0.0).astype(BF16)
        hi, lo = _split_bf16(lg_ref[bi, rows(di, sub), :])
        b = _dot(jnp.concatenate([tri, tri], axis=1), jnp.concatenate([hi, lo], axis=0))
        cums[bi, di, sub] = (b, b[end:end + 1, :])

    def load(bi, di, h, sub):
        q_ref, k_ref, v_ref = dirs[di][:3]
        b, tot = cums[bi, di, sub]
        sl = slice(h * dk, (h + 1) * dk)
        return (q_ref[bi, rows(di, sub), sl].astype(F32), k_ref[bi, rows(di, sub), sl].astype(F32),
                v_ref[bi, rows(di, sub), h * dv:(h + 1) * dv], b[:, sl], tot[:, sl])

    def decayed_state(bi, di, h, tot):
        dcol = jnp.transpose(jnp.broadcast_to(jnp.exp(tot), (dk, dk)))
        return st_scr[bi, di, h] * jnp.concatenate([dcol] * (dv // dk), axis=1)

    def fast_step():
        chains = [(bi, di, h, sub) for sub in range(n_sub) for bi in range(bsz)
                  for di in range(2) for h in range(GLA_HEADS)]
        live = {}

        def factors(n):
            q, k, v, bh, tot = load(*chains[n])
            half = 0.5 * tot
            eh = jnp.exp(half)
            qe = q * jnp.exp(bh - half)
            ke = k * jnp.exp(half - bh)
            live[n] = dict(v=v, tot=tot, qe=qe.astype(BF16), ke=ke.astype(BF16),
                           qd=(qe * eh).astype(BF16), kd=(ke * eh).astype(BF16))

        def scores(n):
            live[n]["a"] = _dot_nt(live[n].pop("qe"), live[n].pop("ke"))

        def outputs(n):
            bi, di, h, _ = chains[n]
            c_ = live[n]
            a = jnp.where(dirs[di][5], c_.pop("a"), 0.0).astype(BF16)
            lhs = jnp.concatenate([a, c_.pop("qd")], axis=1)
            rhs = jnp.concatenate([c_["v"], st_scr[bi, di, h].astype(BF16)], axis=0)
            c_["o"] = _dot(lhs, rhs)
            c_["u"] = _dot_tn(c_.pop("kd"), c_.pop("v"))

        def finish(n):
            bi, di, h, sub = chains[n]
            c_ = live.pop(n)
            o_ref = dirs[di][4]
            o_ref[bi, rows(di, sub), h * dv:(h + 1) * dv] = c_["o"].astype(o_ref.dtype)
            st_scr[bi, di, h] = decayed_state(bi, di, h, c_["tot"]) + c_["u"]

        def cumulate_for(n):
            bi, di, _, sub = chains[n]
            cumulate(bi, di, sub)

        phases = ((factors, 0), (scores, 1), (outputs, 3), (finish, 5))
        cumulate_for(0)
        for i in range(len(chains) + phases[-1][1]):
            ahead = i + GLA_HEADS
            if i % GLA_HEADS == 0 and ahead < len(chains):
                cumulate_for(ahead)
            for phase, lag in phases:
                if 0 <= i - lag < len(chains):
                    phase(i - lag)

    def safe_step():
        ridx = lax.broadcasted_iota(jnp.int32, (c, 1), 0)
        groups = [(s, b, d) for s in range(n_sub) for b in range(bsz) for d in range(2)]
        for sub, bi, di in groups:
            o_ref = dirs[di][4]
            cumulate(bi, di, sub)
            for h in range(GLA_HEADS):
                q, k, v, bh, tot = load(bi, di, h, sub)
                q32_scr[...] = q
                k32_scr[...] = k
                v32_scr[...] = v.astype(F32)
                b_scr[...] = bh
                o_scr[...] = _dot((q * jnp.exp(bh)).astype(BF16), st_scr[bi, di, h].astype(BF16))
                st_scr[bi, di, h] = (decayed_state(bi, di, h, tot)
                                     + _dot_tn((k * jnp.exp(tot - bh)).astype(BF16), v))

                def body(j, carry):
                    bj = b_scr[pl.ds(j, 1), :]
                    valid = (ridx >= j) if di == 0 else (ridx <= j)
                    e = jnp.exp(jnp.where(valid, b_scr[...] - bj, -jnp.inf))
                    w = jnp.sum(q32_scr[...] * e * k32_scr[pl.ds(j, 1), :], axis=-1, keepdims=True)
                    o_scr[...] += w * v32_scr[pl.ds(j, 1), :]
                    return carry

                lax.fori_loop(0, c, body, 0)
                o_ref[bi, rows(di, sub), h * dv:(h + 1) * dv] = o_scr[...].astype(o_ref.dtype)

    lax.cond(worst <= SAFE_DECAY, fast_step, safe_step)

    @pl.when(t == pl.num_programs(0) - 1)
    def _():
        sfin_ref[...] = st_scr[...]


def _scan_call(q, k, v, lgf, lgb, s0):
    bsz, t, dk_tot = q.shape
    dv_tot = v.shape[-1]
    dk, dv = dk_tot // GLA_HEADS, dv_tot // GLA_HEADS
    c = SCAN_CHUNK
    tile = SCAN_TILE
    nt = t // tile
    assert t % tile == 0 and tile % c == 0 and dv % dk == 0
    fwd = lambda w: pl.BlockSpec((bsz, tile, w), lambda i: (0, i, 0))
    bwd = lambda w: pl.BlockSpec((bsz, tile, w), lambda i: (0, nt - 1 - i, 0))
    st_spec = pl.BlockSpec(s0.shape, lambda i: (0,) * s0.ndim)
    kern = functools.partial(_scan_kernel, dk=dk, dv=dv)
    return pl.pallas_call(
        kern,
        out_shape=[jax.ShapeDtypeStruct((bsz, t, dv_tot), BF16),
                   jax.ShapeDtypeStruct((bsz, t, dv_tot), BF16),
                   jax.ShapeDtypeStruct(s0.shape, F32)],
        grid=(nt,),
        in_specs=[fwd(dk_tot), fwd(dk_tot), fwd(dv_tot), fwd(dk_tot),
                  bwd(dk_tot), bwd(dk_tot), bwd(dv_tot), bwd(dk_tot), st_spec],
        out_specs=[fwd(dv_tot), bwd(dv_tot), st_spec],
        scratch_shapes=[pltpu.VMEM(s0.shape, F32),
                        pltpu.VMEM((c, dk), F32), pltpu.VMEM((c, dk), F32),
                        pltpu.VMEM((c, dv), F32), pltpu.VMEM((c, dk), F32),
                        pltpu.VMEM((c, dv), F32)],
        compiler_params=pltpu.CompilerParams(dimension_semantics=("arbitrary",),
                                             vmem_limit_bytes=VMEM_LIMIT),
        name="gla_scan",
    )(q, k, v, lgf, q, k, v, lgb, s0)


def _tail_kernel(x_ref, row_ref, col_ref, ma_ref, gb_ref, of_ref, ob_ref, sr_ref,
                 g1_ref, sh2_ref, sc2_ref, g2_ref, og_ref, n2_ref, fg_ref,
                 pb_ref, wo_ref, w1_ref, w2_ref, out_ref, *, dv):
    xp = _load_tokens(x_ref, row_ref, col_ref)
    o = of_ref[0].astype(F32) + ob_ref[0].astype(F32)
    heads = []
    for h in range(GLA_HEADS):
        sl = slice(h * dv, (h + 1) * dv)
        heads.append(_rms(o[:, sl], og_ref[:, sl]))
    yb = (jnp.concatenate(heads, axis=-1) * sr_ref[0].astype(F32)).astype(BF16)
    merged = ma_ref[0].astype(F32) + gb_ref[0].astype(F32) * _dot(yb, pb_ref[...])
    x1 = xp + g1_ref[0] * _dot(merged.astype(BF16), wo_ref[...])
    h2 = (_rms(x1, n2_ref[...]) * (1.0 + sc2_ref[0]) + sh2_ref[0]).astype(BF16)
    acc = jnp.zeros_like(x1)
    d_ff = w1_ref.shape[1]
    for j in range(d_ff // FF_COL_TILE):
        sl = slice(j * FF_COL_TILE, (j + 1) * FF_COL_TILE)
        f = jnp.maximum(_dot(h2, w1_ref[:, sl]), 0.0)
        acc = acc + _dot((f * f).astype(BF16), w2_ref[sl, :])
    x2 = x1 + g2_ref[0] * acc
    out_ref[0] = _rms(x2, fg_ref[...])


def _tail_call(x4, pos, acts, mods, consts, tm):
    bsz, nrow, gw_, d = x4.shape
    t = nrow * gw_
    rows_per_tile = tm // GRID_W
    rowemb, colemb = pos
    dv = acts[2].shape[-1] // GLA_HEADS
    tok = lambda w: pl.BlockSpec((1, tm, w), lambda b, i: (b, i, 0))
    per_batch = pl.BlockSpec((1, 1, d), lambda b, i: (b, 0, 0))
    in_specs = ([pl.BlockSpec((1, rows_per_tile, GRID_W, d), lambda b, i: (b, i, 0, 0)),
                 pl.BlockSpec((rows_per_tile, 1, d // 2), lambda b, i: (i, 0, 0)),
                 _const_spec(colemb.shape)]
                + [tok(a.shape[-1]) for a in acts]
                + [per_batch] * len(mods)
                + [_const_spec(a.shape) for a in consts])
    return pl.pallas_call(
        functools.partial(_tail_kernel, dv=dv),
        out_shape=jax.ShapeDtypeStruct((bsz, t, d), F32),
        grid=(bsz, t // tm),
        in_specs=in_specs,
        out_specs=tok(d),
        compiler_params=pltpu.CompilerParams(dimension_semantics=("arbitrary", "arbitrary"),
                                             vmem_limit_bytes=VMEM_LIMIT),
        name="tail",
    )(x4, rowemb, colemb, *acts, *mods, *consts)


def _pos_tables(t, d):
    quarter = d // 4
    freqs = 1.0 / (10000.0 ** (jnp.arange(quarter, dtype=F32) / quarter))

    def emb(n):
        a = jnp.arange(n, dtype=F32)[:, None] * freqs[None, :]
        return jnp.concatenate([jnp.sin(a), jnp.cos(a)], -1)

    return emb(t // GRID_W)[:, None, :], emb(GRID_W)[None, :, :]


def kernel(x, c, ctx, c_ctx, w_ada, b_ada, norm1_g, w_in, lnv_g, lnv_b, w_s, b_s, w_dec2, b_dec,
           onorm_g, proj_a, proj_b, w_out, norm2_g, w_ff1, w_ff2, final_g):
    bsz, t, d = x.shape
    depth = w_in.shape[0]
    assert depth == 1, "single-layer problem: the context stream only feeds the scan states"
    dv_tot = proj_b.shape[1]
    dk_tot = w_dec2.shape[-1]

    c_rows = jnp.zeros((MOD_ROWS, d), F32).at[:bsz].set(c).at[bsz].set(c_ctx)
    mod = _mod_call(c_rows, w_ada[0], b_ada[0])
    sh1, sc1, g1, sh2, sc2, g2 = [m[:, None, :] for m in jnp.split(mod, 6, axis=-1)]
    latent = lambda m: m[:bsz]
    context = lambda m: jnp.broadcast_to(m[bsz:bsz + 1], (bsz, 1, d))

    row = lambda a: a.reshape(1, -1)
    wts = (jnp.swapaxes(w_in[0], 0, 1).astype(BF16),
           w_dec2[0].astype(BF16), b_dec[0][:, None, :], row(lnv_g[0]), row(lnv_b[0]),
           w_s[0].astype(BF16), b_s[0][:, :, None], proj_a[0].astype(BF16))
    g1n = row(norm1_g[0])

    ctx4 = ctx.reshape(bsz, ctx.shape[1] // GRID_W, GRID_W, d)
    cact = _inproj_call(ctx4, None, context(sh1), context(sc1), g1n, wts,
                        min(TOKEN_TILE, ctx.shape[1]))
    zero = jnp.zeros((bsz, 2, GLA_HEADS, dk_tot // GLA_HEADS, dv_tot // GLA_HEADS), F32)
    _, _, s_ctx = _scan_call(cact[2], cact[3], cact[4], cact[6], cact[7], zero)

    pos = _pos_tables(t, d)
    x4 = x.reshape(bsz, t // GRID_W, GRID_W, d)
    ma, gb, q, k, v, sr, lgf, lgb = _inproj_call(x4, pos, latent(sh1), latent(sc1), g1n, wts,
                                                 TOKEN_TILE)
    o_f, o_b, _ = _scan_call(q, k, v, lgf, lgb, s_ctx)

    consts = (row(onorm_g[0]), row(norm2_g[0]), row(final_g), proj_b[0].astype(BF16),
              w_out[0].astype(BF16), w_ff1[0].astype(BF16), w_ff2[0].astype(BF16))
    mods = (latent(g1), latent(sh2), latent(sc2), latent(g2))
    return _tail_call(x4, pos, (ma, gb, o_f, o_b, sr), mods, consts, TOKEN_TILE)
```

```python
import functools
import math

import jax
import jax.numpy as jnp
from jax import lax
from jax.experimental import pallas as pl
from jax.experimental.pallas import tpu as pltpu

F32 = jnp.float32
BF16 = jnp.bfloat16

GRID_W = 64
GLA_HEADS = 4
GATE_TAU = 16.0
EPS = 1e-6

TOKEN_TILE = 512
SCAN_CHUNK = 128
SCAN_TILE = 256
MOD_ROWS = 8
MOD_COL_TILE = 1024
PROJ_COL_TILE = 512
FF_COL_TILE = 1024
SAFE_DECAY = 60.0
VMEM_LIMIT = 56 * 1024 * 1024


def _sigmoid(x):
    return 1.0 / (1.0 + jnp.exp(-x))


def _gelu_tanh(x):
    c = math.sqrt(2.0 / math.pi)
    return 0.5 * x * (1.0 + jnp.tanh(c * (x + 0.044715 * (x * x * x))))


def _log_sigmoid(z):
    return jnp.minimum(z, 0.0) - jnp.log(1.0 + jnp.exp(-jnp.abs(z)))


def _rms(x, g):
    return x * lax.rsqrt(jnp.mean(x * x, axis=-1, keepdims=True) + EPS) * g


def _split_bf16(a):
    hi = a.astype(BF16)
    lo = (a - hi.astype(F32)).astype(BF16)
    return hi, lo


def _dot(a, b):
    return jnp.dot(a, b, preferred_element_type=F32)


def _dot_nt(a, b):
    return lax.dot_general(a, b, (((1,), (1,)), ((), ())), preferred_element_type=F32)


def _dot_tn(a, b):
    return lax.dot_general(a, b, (((0,), (0,)), ((), ())), preferred_element_type=F32)


def _const_spec(shape):
    nd = len(shape)
    return pl.BlockSpec(shape, lambda *_: (0,) * nd, pipeline_mode=pl.Buffered(1))


def _mod_kernel(c_ref, w_ref, b_ref, o_ref):
    c = c_ref[...]
    a_hi, a_lo = _split_bf16(c * _sigmoid(c))
    w_hi, w_lo = _split_bf16(w_ref[...])
    o_ref[...] = _dot(a_hi, w_hi) + _dot(a_lo, w_hi) + _dot(a_hi, w_lo) + b_ref[...]


def _mod_call(c_rows, w_ada, b_ada):
    d, n = w_ada.shape
    return pl.pallas_call(
        _mod_kernel,
        out_shape=jax.ShapeDtypeStruct((MOD_ROWS, n), F32),
        grid=(n // MOD_COL_TILE,),
        in_specs=[pl.BlockSpec((MOD_ROWS, d), lambda j: (0, 0)),
                  pl.BlockSpec((d, MOD_COL_TILE), lambda j: (0, j)),
                  pl.BlockSpec((1, MOD_COL_TILE), lambda j: (0, j))],
        out_specs=pl.BlockSpec((MOD_ROWS, MOD_COL_TILE), lambda j: (0, j)),
        compiler_params=pltpu.CompilerParams(dimension_semantics=("arbitrary",)),
        name="adaln_mod",
    )(c_rows, w_ada, b_ada.reshape(1, n))


def _load_tokens(x_ref, row_ref, col_ref):
    x = x_ref[0]
    half = x.shape[-1] // 2
    x = jnp.concatenate([x[..., :half] + row_ref[...], x[..., half:] + col_ref[...]], axis=-1)
    return x.reshape(x.shape[0] * x.shape[1], x.shape[2])


def _segment_rows(d, d_a, dk_tot, dv_tot, rank):
    names = ("u", "va", "q", "k", "v", "r", "dec", "ga", "gb")
    widths = (d_a, d_a, dk_tot, dk_tot, dv_tot, dv_tot, 2 * rank, d, d)
    offsets, r0 = {}, 0
    for name, w in zip(names, widths):
        offsets[name] = r0
        r0 += w
    return offsets


def _store_log_decays(dec, wdec2_ref, bdec_ref, out_refs):
    rank = wdec2_ref.shape[1]
    for di, out_ref in enumerate(out_refs):
        z = _dot(dec[:, di * rank:(di + 1) * rank].astype(BF16), wdec2_ref[di]) + bdec_ref[di]
        out_ref[0] = _log_sigmoid(z) * (1.0 / GATE_TAU)


def _ctx_proj_kernel(x_ref, sh_ref, sc_ref, g1_ref, wt_ref, wdec2_ref, bdec_ref,
                     k_ref, v_ref, lgf_ref, lgb_ref, *, seg, dk_tot, dv_tot):
    hb = (_rms(x_ref[0], g1_ref[...]) * (1.0 + sc_ref[...]) + sh_ref[...]).astype(BF16)
    rank = wdec2_ref.shape[1]
    proj = lambda r0, width: _dot_nt(hb, wt_ref[r0:r0 + width, :])
    _store_log_decays(proj(seg["dec"], 2 * rank), wdec2_ref, bdec_ref, (lgf_ref, lgb_ref))
    k_ref[0] = proj(seg["k"], dk_tot).astype(BF16)
    v_ref[0] = proj(seg["v"], dv_tot).astype(BF16)


def _ctx_proj_call(ctx, sh, sc, g1, w_in_t, w_dec2, b_dec, seg, dk_tot, dv_tot):
    bsz, t, d = ctx.shape
    tm = min(TOKEN_TILE, t)
    assert t % tm == 0
    tok = lambda w: pl.BlockSpec((1, tm, w), lambda b, i: (b, i, 0))
    consts = [sh, sc, g1, w_in_t, w_dec2, b_dec]
    widths = [dk_tot, dv_tot, dk_tot, dk_tot]
    dtypes = [BF16, BF16, F32, F32]
    return pl.pallas_call(
        functools.partial(_ctx_proj_kernel, seg=seg, dk_tot=dk_tot, dv_tot=dv_tot),
        out_shape=[jax.ShapeDtypeStruct((bsz, t, w), dt) for w, dt in zip(widths, dtypes)],
        grid=(bsz, t // tm),
        in_specs=[tok(d)] + [_const_spec(a.shape) for a in consts],
        out_specs=[tok(w) for w in widths],
        compiler_params=pltpu.CompilerParams(dimension_semantics=("arbitrary", "arbitrary"),
                                             vmem_limit_bytes=VMEM_LIMIT),
        name="ctx_proj",
    )(ctx, *consts)


def _inproj_kernel(x_ref, row_ref, col_ref, sh_ref, sc_ref, g1_ref, wt_ref, wdec2_ref, bdec_ref,
                   lng_ref, lnb_ref, ws_ref, bs_ref, pa_ref,
                   ma_ref, gb_ref, q_ref, k_ref, v_ref, sr_ref, lgf_ref, lgb_ref, ya_scr,
                   *, seg, d_a, dk_tot, dv_tot):
    xp = _load_tokens(x_ref, row_ref, col_ref)
    tm = xp.shape[0]
    d = xp.shape[1]
    hb = (_rms(xp, g1_ref[...]) * (1.0 + sc_ref[0]) + sh_ref[0]).astype(BF16)

    def proj(r0, width):
        return _dot_nt(hb, wt_ref[r0:r0 + width, :])

    r_u, r_va, r_q, r_k, r_v, r_r = (seg[n] for n in ("u", "va", "q", "k", "v", "r"))
    r_ga, r_gb = seg["ga"], seg["gb"]
    rank = wdec2_ref.shape[1]
    groups, chunk_a = ws_ref.shape[0], ws_ref.shape[1]
    dk = dk_tot // GLA_HEADS
    gw = d_a // groups
    blk = PROJ_COL_TILE
    cols = lambda j: slice(j * blk, (j + 1) * blk)
    gu, gv, vn = {}, {}, {}

    stages = []
    stages.append((lambda: proj(seg["dec"], 2 * rank),
                   lambda dec: _store_log_decays(dec, wdec2_ref, bdec_ref, (lgf_ref, lgb_ref))))

    def store(out_ref, j, fn):
        def ep(y):
            out_ref[0, :, cols(j)] = fn(y).astype(out_ref.dtype)
        return ep

    def keep(table, j, fn):
        def ep(y):
            table[j] = fn(y)
        return ep

    for j in range(d_a // blk):
        stages.append((lambda j=j: proj(r_u + j * blk, blk), keep(gu, j, _gelu_tanh)))
    for j in range(d_a // blk):
        stages.append((lambda j=j: proj(r_va + j * blk, blk), keep(gv, j, _gelu_tanh)))

    def ep_q_and_layernorm(y):
        q_ref[0] = (y * dk ** -0.5).astype(BF16)
        nb = d_a // blk
        mu = sum(jnp.sum(gv[j], axis=-1, keepdims=True) for j in range(nb)) * (1.0 / d_a)
        cen = [gv[j] - mu for j in range(nb)]
        var = sum(jnp.sum(c * c, axis=-1, keepdims=True) for c in cen) * (1.0 / d_a)
        inv = lax.rsqrt(var + EPS)
        for j in range(nb):
            vn[j] = (cen[j] * inv * lng_ref[:, cols(j)] + lnb_ref[:, cols(j)]).astype(BF16)
    stages.append((lambda: proj(r_q, dk_tot), ep_q_and_layernorm))
    stages.append((lambda: proj(r_k, dk_tot), store(k_ref, 0, lambda y: y)))
    for j in range(dv_tot // blk):
        stages.append((lambda j=j: proj(r_r + j * blk, blk),
                       store(sr_ref, j, lambda y: y * _sigmoid(y))))

    def mm_gate(ci, g):
        r0, c0 = ci * chunk_a, g * gw
        return _dot(ws_ref[g], vn[c0 // blk][r0:r0 + chunk_a, c0 % blk:c0 % blk + gw])

    def ep_gate(ci, g):
        r0, c0 = ci * chunk_a, g * gw

        def ep(s):
            u = gu[c0 // blk][r0:r0 + chunk_a, c0 % blk:c0 % blk + gw]
            ya_scr[r0:r0 + chunk_a, c0:c0 + gw] = (u * (s + bs_ref[g])).astype(BF16)
        return ep
    for ci in range(tm // chunk_a):
        for g in range(groups):
            stages.append((lambda ci=ci, g=g: mm_gate(ci, g), ep_gate(ci, g)))

    for j in range(d // blk):
        stages.append((lambda j=j: proj(r_gb + j * blk, blk), store(gb_ref, j, _sigmoid)))
    for j in range(d // blk):
        stages.append((lambda j=j: (proj(r_ga + j * blk, blk),
                                    _dot(ya_scr[...], pa_ref[:, cols(j)])),
                       store(ma_ref, j, lambda y: _sigmoid(y[0]) * y[1])))
    for j in range(dv_tot // blk):
        stages.append((lambda j=j: proj(r_v + j * blk, blk), store(v_ref, j, lambda y: y)))

    pending = stages[0][0]()
    for i, (_, epilogue) in enumerate(stages):
        nxt = stages[i + 1][0]() if i + 1 < len(stages) else None
        epilogue(pending)
        pending = nxt


def _inproj_call(x4, pos, sh, sc, consts, seg, d_a, dk_tot, dv_tot):
    bsz, nrow, gw_, d = x4.shape
    t = nrow * gw_
    tm = TOKEN_TILE
    rows_per_tile = tm // GRID_W
    rowemb, colemb = pos
    chunk_a = consts[6].shape[1]
    assert tm % chunk_a == 0 and tm % GRID_W == 0 and t % tm == 0

    tok = lambda w: pl.BlockSpec((1, tm, w), lambda b, i: (b, i, 0))
    per_batch = pl.BlockSpec((1, 1, d), lambda b, i: (b, 0, 0))
    in_specs = ([pl.BlockSpec((1, rows_per_tile, GRID_W, d), lambda b, i: (b, i, 0, 0)),
                 pl.BlockSpec((rows_per_tile, 1, d // 2), lambda b, i: (i, 0, 0)),
                 _const_spec(colemb.shape), per_batch, per_batch]
                + [_const_spec(a.shape) for a in consts])
    widths = [d, d, dk_tot, dk_tot, dv_tot, dv_tot, dk_tot, dk_tot]
    dtypes = [BF16] * 6 + [F32] * 2
    kern = functools.partial(_inproj_kernel, seg=seg, d_a=d_a, dk_tot=dk_tot, dv_tot=dv_tot)
    return pl.pallas_call(
        kern,
        out_shape=[jax.ShapeDtypeStruct((bsz, t, w), dt) for w, dt in zip(widths, dtypes)],
        grid=(bsz, t // tm),
        in_specs=in_specs,
        out_specs=[tok(w) for w in widths],
        scratch_shapes=[pltpu.VMEM((tm, d_a), BF16)],
        compiler_params=pltpu.CompilerParams(dimension_semantics=("arbitrary", "arbitrary"),
                                             vmem_limit_bytes=VMEM_LIMIT),
        name="inproj",
    )(x4, rowemb, colemb, sh, sc, *consts)


def _scan_masks(c):
    row = lax.broadcasted_iota(jnp.int32, (c, c), 0)
    col = lax.broadcasted_iota(jnp.int32, (c, c), 1)
    return ((row >= col, c - 1), (row <= col, 0))


def _chunk_cumsum(lg, keep, end):
    tri = jnp.where(keep, 1.0, 0.0).astype(BF16)
    hi, lo = _split_bf16(lg)
    b = _dot(jnp.concatenate([tri, tri], axis=1), jnp.concatenate([hi, lo], axis=0))
    return b, b[end:end + 1, :]


def _decay_rows(st, tot):
    dk, dv = st.shape
    dcol = jnp.transpose(jnp.broadcast_to(jnp.exp(tot), (dk, dk)))
    return st * jnp.concatenate([dcol] * (dv // dk), axis=1)


def _state_kernel(kf_ref, vf_ref, lgf_ref, kb_ref, vb_ref, lgb_ref, s_ref, st_scr, *, dk, dv):
    t = pl.program_id(0)
    bsz, c = kf_ref.shape[0], SCAN_CHUNK
    n_sub = kf_ref.shape[1] // c

    @pl.when(t == 0)
    def _():
        st_scr[...] = jnp.zeros_like(st_scr)

    masks = _scan_masks(c)
    dirs = ((kf_ref, vf_ref, lgf_ref), (kb_ref, vb_ref, lgb_ref))
    for sub in range(n_sub):
        for bi in range(bsz):
            for di, (k_ref, v_ref, lg_ref) in enumerate(dirs):
                first = (sub if di == 0 else n_sub - 1 - sub) * c
                b, tot = _chunk_cumsum(lg_ref[bi, first:first + c, :], *masks[di])
                for h in range(GLA_HEADS):
                    sl = slice(h * dk, (h + 1) * dk)
                    k = k_ref[bi, first:first + c, sl].astype(F32)
                    kd = (k * jnp.exp(tot[:, sl] - b[:, sl])).astype(BF16)
                    v = v_ref[bi, first:first + c, h * dv:(h + 1) * dv]
                    st_scr[bi, di, h] = _decay_rows(st_scr[bi, di, h], tot[:, sl]) + _dot_tn(kd, v)

    @pl.when(t == pl.num_programs(0) - 1)
    def _():
        s_ref[...] = st_scr[...]


def _state_call(k, v, lgf, lgb):
    bsz, t, dk_tot = k.shape
    dv_tot = v.shape[-1]
    dk, dv = dk_tot // GLA_HEADS, dv_tot // GLA_HEADS
    tile = min(SCAN_TILE, t)
    nt = t // tile
    assert t % tile == 0 and tile % SCAN_CHUNK == 0 and dv % dk == 0
    fwd = lambda w: pl.BlockSpec((bsz, tile, w), lambda i: (0, i, 0))
    bwd = lambda w: pl.BlockSpec((bsz, tile, w), lambda i: (0, nt - 1 - i, 0))
    shape = (bsz, 2, GLA_HEADS, dk, dv)
    return pl.pallas_call(
        functools.partial(_state_kernel, dk=dk, dv=dv),
        out_shape=jax.ShapeDtypeStruct(shape, F32),
        grid=(nt,),
        in_specs=[fwd(dk_tot), fwd(dv_tot), fwd(dk_tot), bwd(dk_tot), bwd(dv_tot), bwd(dk_tot)],
        out_specs=pl.BlockSpec(shape, lambda i: (0,) * len(shape)),
        scratch_shapes=[pltpu.VMEM(shape, F32)],
        compiler_params=pltpu.CompilerParams(dimension_semantics=("arbitrary",),
                                             vmem_limit_bytes=VMEM_LIMIT),
        name="gla_ctx_state",
    )(k, v, lgf, k, v, lgb)


def _scan_kernel(qf_ref, kf_ref, vf_ref, lgf_ref, qb_ref, kb_ref, vb_ref, lgb_ref, s0_ref,
                 of_ref, ob_ref, st_scr, q32_scr, k32_scr, v32_scr, b_scr, o_scr,
                 *, dk, dv):
    t = pl.program_id(0)
    bsz, c = qf_ref.shape[0], SCAN_CHUNK
    n_sub = qf_ref.shape[1] // c

    def rows(di, sub):
        first = (sub if di == 0 else n_sub - 1 - sub) * c
        return slice(first, first + c)

    @pl.when(t == 0)
    def _():
        st_scr[...] = s0_ref[...]

    masks = _scan_masks(c)
    dirs = ((qf_ref, kf_ref, vf_ref, lgf_ref, of_ref) + masks[0],
            (qb_ref, kb_ref, vb_ref, lgb_ref, ob_ref) + masks[1])

    worst = None
    for lg_ref in (lgf_ref, lgb_ref):
        for sub in range(n_sub):
            w = jnp.max(-jnp.sum(lg_ref[:, sub * c:(sub + 1) * c, :], axis=1))
            worst = w if worst is None else jnp.maximum(worst, w)

    cums = {}

    def cumulate(bi, di, sub):
        lg_ref, keep, end = dirs[di][3], dirs[di][5], dirs[di][6]
        cums[bi, di, sub] = _chunk_cumsum(lg_ref[bi, rows(di, sub), :], keep, end)

    def load(bi, di, h, sub):
        q_ref, k_ref, v_ref = dirs[di][:3]
        b, tot = cums[bi, di, sub]
        sl = slice(h * dk, (h + 1) * dk)
        return (q_ref[bi, rows(di, sub), sl].astype(F32), k_ref[bi, rows(di, sub), sl].astype(F32),
                v_ref[bi, rows(di, sub), h * dv:(h + 1) * dv], b[:, sl], tot[:, sl])

    def decayed_state(bi, di, h, tot):
        return _decay_rows(st_scr[bi, di, h], tot)

    def fast_step():
        chains = [(bi, di, h, sub) for sub in range(n_sub) for bi in range(bsz)
                  for di in range(2) for h in range(GLA_HEADS)]
        live = {}

        def factors(n):
            q, k, v, bh, tot = load(*chains[n])
            half = 0.5 * tot
            eh = jnp.exp(half)
            qe = q * jnp.exp(bh - half)
            ke = k * jnp.exp(half - bh)
            live[n] = dict(v=v, tot=tot, qe=qe.astype(BF16), ke=ke.astype(BF16),
                           qd=(qe * eh).astype(BF16), kd=(ke * eh).astype(BF16))

        def scores(n):
            live[n]["a"] = _dot_nt(live[n].pop("qe"), live[n].pop("ke"))

        def outputs(n):
            bi, di, h, _ = chains[n]
            c_ = live[n]
            a = jnp.where(dirs[di][5], c_.pop("a"), 0.0).astype(BF16)
            lhs = jnp.concatenate([a, c_.pop("qd")], axis=1)
            rhs = jnp.concatenate([c_["v"], st_scr[bi, di, h].astype(BF16)], axis=0)
            c_["o"] = _dot(lhs, rhs)
            c_["u"] = _dot_tn(c_.pop("kd"), c_.pop("v"))

        def finish(n):
            bi, di, h, sub = chains[n]
            c_ = live.pop(n)
            o_ref = dirs[di][4]
            o_ref[bi, rows(di, sub), h * dv:(h + 1) * dv] = c_["o"].astype(o_ref.dtype)
            st_scr[bi, di, h] = decayed_state(bi, di, h, c_["tot"]) + c_["u"]

        def cumulate_for(n):
            bi, di, _, sub = chains[n]
            cumulate(bi, di, sub)

        phases = ((factors, 0), (scores, 1), (outputs, 3), (finish, 5))
        cumulate_for(0)
        for i in range(len(chains) + phases[-1][1]):
            ahead = i + GLA_HEADS
            if i % GLA_HEADS == 0 and ahead < len(chains):
                cumulate_for(ahead)
            for phase, lag in phases:
                if 0 <= i - lag < len(chains):
                    phase(i - lag)

    def safe_step():
        ridx = lax.broadcasted_iota(jnp.int32, (c, 1), 0)
        groups = [(s, b, d) for s in range(n_sub) for b in range(bsz) for d in range(2)]
        for sub, bi, di in groups:
            o_ref = dirs[di][4]
            cumulate(bi, di, sub)
            for h in range(GLA_HEADS):
                q, k, v, bh, tot = load(bi, di, h, sub)
                q32_scr[...] = q
                k32_scr[...] = k
                v32_scr[...] = v.astype(F32)
                b_scr[...] = bh
                o_scr[...] = _dot((q * jnp.exp(bh)).astype(BF16), st_scr[bi, di, h].astype(BF16))
                st_scr[bi, di, h] = (decayed_state(bi, di, h, tot)
                                     + _dot_tn((k * jnp.exp(tot - bh)).astype(BF16), v))

                def body(j, carry):
                    bj = b_scr[pl.ds(j, 1), :]
                    valid = (ridx >= j) if di == 0 else (ridx <= j)
                    e = jnp.exp(jnp.where(valid, b_scr[...] - bj, -jnp.inf))
                    w = jnp.sum(q32_scr[...] * e * k32_scr[pl.ds(j, 1), :], axis=-1, keepdims=True)
                    o_scr[...] += w * v32_scr[pl.ds(j, 1), :]
                    return carry

                lax.fori_loop(0, c, body, 0)
                o_ref[bi, rows(di, sub), h * dv:(h + 1) * dv] = o_scr[...].astype(o_ref.dtype)

    lax.cond(worst <= SAFE_DECAY, fast_step, safe_step)


def _scan_call(q, k, v, lgf, lgb, s0):
    bsz, t, dk_tot = q.shape
    dv_tot = v.shape[-1]
    dk, dv = dk_tot // GLA_HEADS, dv_tot // GLA_HEADS
    c = SCAN_CHUNK
    tile = SCAN_TILE
    nt = t // tile
    assert t % tile == 0 and tile % c == 0 and dv % dk == 0
    fwd = lambda w: pl.BlockSpec((bsz, tile, w), lambda i: (0, i, 0))
    bwd = lambda w: pl.BlockSpec((bsz, tile, w), lambda i: (0, nt - 1 - i, 0))
    st_spec = pl.BlockSpec(s0.shape, lambda i: (0,) * s0.ndim)
    kern = functools.partial(_scan_kernel, dk=dk, dv=dv)
    return pl.pallas_call(
        kern,
        out_shape=[jax.ShapeDtypeStruct((bsz, t, dv_tot), BF16),
                   jax.ShapeDtypeStruct((bsz, t, dv_tot), BF16)],
        grid=(nt,),
        in_specs=[fwd(dk_tot), fwd(dk_tot), fwd(dv_tot), fwd(dk_tot),
                  bwd(dk_tot), bwd(dk_tot), bwd(dv_tot), bwd(dk_tot), st_spec],
        out_specs=[fwd(dv_tot), bwd(dv_tot)],
        scratch_shapes=[pltpu.VMEM(s0.shape, F32),
                        pltpu.VMEM((c, dk), F32), pltpu.VMEM((c, dk), F32),
                        pltpu.VMEM((c, dv), F32), pltpu.VMEM((c, dk), F32),
                        pltpu.VMEM((c, dv), F32)],
        compiler_params=pltpu.CompilerParams(dimension_semantics=("arbitrary",),
                                             vmem_limit_bytes=VMEM_LIMIT),
        name="gla_scan",
    )(q, k, v, lgf, q, k, v, lgb, s0)


def _tail_kernel(x_ref, row_ref, col_ref, ma_ref, gb_ref, of_ref, ob_ref, sr_ref,
                 g1_ref, sh2_ref, sc2_ref, g2_ref, og_ref, n2_ref, fg_ref,
                 pb_ref, wo_ref, w1_ref, w2_ref, out_ref, *, dv):
    xp = _load_tokens(x_ref, row_ref, col_ref)
    o = of_ref[0].astype(F32) + ob_ref[0].astype(F32)
    heads = []
    for h in range(GLA_HEADS):
        sl = slice(h * dv, (h + 1) * dv)
        heads.append(_rms(o[:, sl], og_ref[:, sl]))
    yb = (jnp.concatenate(heads, axis=-1) * sr_ref[0].astype(F32)).astype(BF16)
    merged = ma_ref[0].astype(F32) + gb_ref[0].astype(F32) * _dot(yb, pb_ref[...])
    x1 = xp + g1_ref[0] * _dot(merged.astype(BF16), wo_ref[...])
    h2 = (_rms(x1, n2_ref[...]) * (1.0 + sc2_ref[0]) + sh2_ref[0]).astype(BF16)
    acc = jnp.zeros_like(x1)
    d_ff = w1_ref.shape[1]
    for j in range(d_ff // FF_COL_TILE):
        sl = slice(j * FF_COL_TILE, (j + 1) * FF_COL_TILE)
        f = jnp.maximum(_dot(h2, w1_ref[:, sl]), 0.0)
        acc = acc + _dot((f * f).astype(BF16), w2_ref[sl, :])
    x2 = x1 + g2_ref[0] * acc
    out_ref[0] = _rms(x2, fg_ref[...])


def _tail_call(x4, pos, acts, mods, consts, tm):
    bsz, nrow, gw_, d = x4.shape
    t = nrow * gw_
    rows_per_tile = tm // GRID_W
    rowemb, colemb = pos
    dv = acts[2].shape[-1] // GLA_HEADS
    tok = lambda w: pl.BlockSpec((1, tm, w), lambda b, i: (b, i, 0))
    per_batch = pl.BlockSpec((1, 1, d), lambda b, i: (b, 0, 0))
    in_specs = ([pl.BlockSpec((1, rows_per_tile, GRID_W, d), lambda b, i: (b, i, 0, 0)),
                 pl.BlockSpec((rows_per_tile, 1, d // 2), lambda b, i: (i, 0, 0)),
                 _const_spec(colemb.shape)]
                + [tok(a.shape[-1]) for a in acts]
                + [per_batch] * len(mods)
                + [_const_spec(a.shape) for a in consts])
    return pl.pallas_call(
        functools.partial(_tail_kernel, dv=dv),
        out_shape=jax.ShapeDtypeStruct((bsz, t, d), F32),
        grid=(bsz, t // tm),
        in_specs=in_specs,
        out_specs=tok(d),
        compiler_params=pltpu.CompilerParams(dimension_semantics=("arbitrary", "arbitrary"),
                                             vmem_limit_bytes=VMEM_LIMIT),
        name="tail",
    )(x4, rowemb, colemb, *acts, *mods, *consts)


def _pos_tables(t, d):
    quarter = d // 4
    freqs = 1.0 / (10000.0 ** (jnp.arange(quarter, dtype=F32) / quarter))

    def emb(n):
        a = jnp.arange(n, dtype=F32)[:, None] * freqs[None, :]
        return jnp.concatenate([jnp.sin(a), jnp.cos(a)], -1)

    return emb(t // GRID_W)[:, None, :], emb(GRID_W)[None, :, :]


def kernel(x, c, ctx, c_ctx, w_ada, b_ada, norm1_g, w_in, lnv_g, lnv_b, w_s, b_s, w_dec2, b_dec,
           onorm_g, proj_a, proj_b, w_out, norm2_g, w_ff1, w_ff2, final_g):
    bsz, t, d = x.shape
    depth = w_in.shape[0]
    assert depth == 1, "single-layer problem: the context stream only feeds the scan states"
    d_a = proj_a.shape[1]
    dv_tot = proj_b.shape[1]
    dk_tot = w_dec2.shape[-1]
    seg = _segment_rows(d, d_a, dk_tot, dv_tot, w_dec2.shape[2])

    c_rows = jnp.zeros((MOD_ROWS, d), F32).at[:bsz].set(c).at[bsz].set(c_ctx)
    mod = _mod_call(c_rows, w_ada[0], b_ada[0])
    sh1, sc1, g1, sh2, sc2, g2 = jnp.split(mod, 6, axis=-1)
    latent = lambda m: m[:bsz, None, :]
    context = lambda m: m[bsz:bsz + 1]

    row = lambda a: a.reshape(1, -1)
    g1n = row(norm1_g[0])
    w_in_t = jnp.swapaxes(w_in[0], 0, 1).astype(BF16)
    wd2, bd = w_dec2[0].astype(BF16), b_dec[0][:, None, :]

    ck, cv, clgf, clgb = _ctx_proj_call(ctx, context(sh1), context(sc1), g1n, w_in_t, wd2, bd,
                                        seg, dk_tot, dv_tot)
    s_ctx = _state_call(ck, cv, clgf, clgb)

    pos = _pos_tables(t, d)
    x4 = x.reshape(bsz, t // GRID_W, GRID_W, d)
    consts = (g1n, w_in_t, wd2, bd, row(lnv_g[0]), row(lnv_b[0]),
              w_s[0].astype(BF16), b_s[0][:, :, None], proj_a[0].astype(BF16))
    ma, gb, q, k, v, sr, lgf, lgb = _inproj_call(x4, pos, latent(sh1), latent(sc1), consts, seg,
                                                 d_a, dk_tot, dv_tot)
    o_f, o_b = _scan_call(q, k, v, lgf, lgb, s_ctx)

    consts = (row(onorm_g[0]), row(norm2_g[0]), row(final_g), proj_b[0].astype(BF16),
              w_out[0].astype(BF16), w_ff1[0].astype(BF16), w_ff2[0].astype(BF16))
    mods = (latent(g1), latent(sh2), latent(sc2), latent(g2))
    return _tail_call(x4, pos, (ma, gb, o_f, o_b, sr), mods, consts, TOKEN_TILE)
```

```python
import functools
import math

import jax
import jax.numpy as jnp
from jax import lax
from jax.experimental import pallas as pl
from jax.experimental.pallas import tpu as pltpu

F32 = jnp.float32
BF16 = jnp.bfloat16

GRID_W = 64
GLA_HEADS = 4
GATE_TAU = 16.0
EPS = 1e-6

TOKEN_TILE = 512
SCAN_CHUNK = 128
SCAN_TILE = 256
MOD_ROWS = 8
MOD_COL_TILE = 1024
PROJ_COL_TILE = 512
WEIGHT_STAGE_ROWS = 512
BF16_SUBLANES = 16
FF_COL_TILE = 1024
SAFE_DECAY = 60.0
VMEM_LIMIT = 56 * 1024 * 1024


def _sigmoid(x):
    return 1.0 / (1.0 + jnp.exp(-x))


def _gelu_tanh(x):
    c = math.sqrt(2.0 / math.pi)
    return 0.5 * x * (1.0 + jnp.tanh(c * (x + 0.044715 * (x * x * x))))


def _log_sigmoid(z):
    return jnp.minimum(z, 0.0) - jnp.log(1.0 + jnp.exp(-jnp.abs(z)))


def _rms(x, g):
    return x * lax.rsqrt(jnp.mean(x * x, axis=-1, keepdims=True) + EPS) * g


def _split_bf16(a):
    hi = a.astype(BF16)
    lo = (a - hi.astype(F32)).astype(BF16)
    return hi, lo


def _dot(a, b):
    return jnp.dot(a, b, preferred_element_type=F32)


def _dot_nt(a, b):
    return lax.dot_general(a, b, (((1,), (1,)), ((), ())), preferred_element_type=F32)


def _dot_tn(a, b):
    return lax.dot_general(a, b, (((0,), (0,)), ((), ())), preferred_element_type=F32)


def _const_spec(shape):
    nd = len(shape)
    return pl.BlockSpec(shape, lambda *_: (0,) * nd, pipeline_mode=pl.Buffered(1))


def _load_as_bf16(src_ref, dst_ref, stage_ref, sem_ref, rows):
    n_rows, n_cols = src_ref.shape
    cols = stage_ref.shape[2]
    assert n_rows % rows == 0 and rows <= stage_ref.shape[1] and n_cols % cols == 0
    n = n_rows // rows

    for c0 in range(0, n_cols, cols):
        def copy(j, slot, c0=c0):
            return pltpu.make_async_copy(src_ref.at[pl.ds(j * rows, rows), pl.ds(c0, cols)],
                                         stage_ref.at[slot, pl.ds(0, rows), :], sem_ref.at[slot])

        copy(0, 0).start()

        def body(j, carry, c0=c0, copy=copy):
            slot = j % 2

            @pl.when(j + 1 < n)
            def _():
                copy(j + 1, 1 - slot).start()

            copy(j, slot).wait()
            first = pl.multiple_of(j * rows, rows)
            dst_ref[pl.ds(first, rows), c0:c0 + cols] = (
                stage_ref[slot, pl.ds(0, rows), :].astype(BF16))
            return carry

        lax.fori_loop(0, n, body, 0)


def _stage_rows(n_rows):
    return max(r for r in range(BF16_SUBLANES, WEIGHT_STAGE_ROWS + 1, BF16_SUBLANES)
               if n_rows % r == 0)


def _mod_kernel(c_ref, w_ref, b_ref, o_ref):
    c = c_ref[...]
    a_hi, a_lo = _split_bf16(c * _sigmoid(c))
    w_hi, w_lo = _split_bf16(w_ref[...])
    o_ref[...] = _dot(a_hi, w_hi) + _dot(a_lo, w_hi) + _dot(a_hi, w_lo) + b_ref[...]


def _mod_call(c_rows, w_ada, b_ada):
    d, n = w_ada.shape
    return pl.pallas_call(
        _mod_kernel,
        out_shape=jax.ShapeDtypeStruct((MOD_ROWS, n), F32),
        grid=(n // MOD_COL_TILE,),
        in_specs=[pl.BlockSpec((MOD_ROWS, d), lambda j: (0, 0)),
                  pl.BlockSpec((d, MOD_COL_TILE), lambda j: (0, j)),
                  pl.BlockSpec((1, MOD_COL_TILE), lambda j: (0, j))],
        out_specs=pl.BlockSpec((MOD_ROWS, MOD_COL_TILE), lambda j: (0, j)),
        compiler_params=pltpu.CompilerParams(dimension_semantics=("arbitrary",)),
        name="adaln_mod",
    )(c_rows, w_ada, b_ada.reshape(1, n))


def _load_tokens(x_ref, row_ref, col_ref):
    x = x_ref[0]
    half = x.shape[-1] // 2
    x = jnp.concatenate([x[..., :half] + row_ref[...], x[..., half:] + col_ref[...]], axis=-1)
    return x.reshape(x.shape[0] * x.shape[1], x.shape[2])


def _segment_rows(d, d_a, dk_tot, dv_tot, rank):
    names = ("u", "va", "q", "k", "v", "r", "dec", "ga", "gb")
    widths = (d_a, d_a, dk_tot, dk_tot, dv_tot, dv_tot, 2 * rank, d, d)
    offsets, r0 = {}, 0
    for name, w in zip(names, widths):
        offsets[name] = r0
        r0 += w
    return offsets


def _store_log_decays(dec, wdec2_ref, bdec_ref, out_refs):
    rank = wdec2_ref.shape[1]
    for di, out_ref in enumerate(out_refs):
        z = _dot(dec[:, di * rank:(di + 1) * rank].astype(BF16),
                 wdec2_ref[di].astype(BF16)) + bdec_ref[di]
        out_ref[0] = _log_sigmoid(z) * (1.0 / GATE_TAU)


def _ctx_proj_kernel(x_ref, sh_ref, sc_ref, g1_ref, wk_ref, wv_ref, wd_ref, wdec2_ref, bdec_ref,
                     k_ref, v_ref, lgf_ref, lgb_ref):
    hb = (_rms(x_ref[0], g1_ref[...]) * (1.0 + sc_ref[...]) + sh_ref[...]).astype(BF16)
    proj = lambda w_ref: _dot_nt(hb, w_ref[...].astype(BF16))
    _store_log_decays(proj(wd_ref), wdec2_ref, bdec_ref, (lgf_ref, lgb_ref))
    k_ref[0] = proj(wk_ref).astype(BF16)
    v_ref[0] = proj(wv_ref).astype(BF16)


def _ctx_proj_call(ctx, sh, sc, g1, w_in_t, w_dec2, b_dec, seg, dk_tot, dv_tot):
    bsz, t, d = ctx.shape
    tm = min(TOKEN_TILE, t)
    assert t % tm == 0
    tok = lambda w: pl.BlockSpec((1, tm, w), lambda b, i: (b, i, 0))

    def rows_spec(r0, width):
        assert r0 % width == 0
        return pl.BlockSpec((width, d), lambda b, i: (r0 // width, 0),
                            pipeline_mode=pl.Buffered(1))

    rank = w_dec2.shape[1]
    consts = [sh, sc, g1]
    tail_consts = [w_dec2, b_dec]
    widths = [dk_tot, dv_tot, dk_tot, dk_tot]
    dtypes = [BF16, BF16, F32, F32]
    return pl.pallas_call(
        _ctx_proj_kernel,
        out_shape=[jax.ShapeDtypeStruct((bsz, t, w), dt) for w, dt in zip(widths, dtypes)],
        grid=(bsz, t // tm),
        in_specs=([tok(d)] + [_const_spec(a.shape) for a in consts]
                  + [rows_spec(seg["k"], dk_tot), rows_spec(seg["v"], dv_tot),
                     rows_spec(seg["dec"], 2 * rank)]
                  + [_const_spec(a.shape) for a in tail_consts]),
        out_specs=[tok(w) for w in widths],
        compiler_params=pltpu.CompilerParams(dimension_semantics=("arbitrary", "arbitrary"),
                                             vmem_limit_bytes=VMEM_LIMIT),
        name="ctx_proj",
    )(ctx, *consts, w_in_t, w_in_t, w_in_t, *tail_consts)


def _inproj_kernel(x_ref, row_ref, col_ref, sh_ref, sc_ref, g1_ref, wdec2_ref, bdec_ref,
                   lng_ref, lnb_ref, ws_ref, bs_ref, wt_hbm, pa_hbm,
                   ma_ref, gb_ref, q_ref, k_ref, v_ref, sr_ref, lgf_ref, lgb_ref,
                   ya_scr, wt_ref, pa_ref, stage_ref, sem_ref, *, seg, d_a, dk_tot, dv_tot):
    @pl.when((pl.program_id(0) == 0) & (pl.program_id(1) == 0))
    def _():
        for src, dst in ((wt_hbm, wt_ref), (pa_hbm, pa_ref)):
            _load_as_bf16(src, dst, stage_ref, sem_ref, _stage_rows(src.shape[0]))

    xp = _load_tokens(x_ref, row_ref, col_ref)
    tm = xp.shape[0]
    d = xp.shape[1]
    hb = (_rms(xp, g1_ref[...]) * (1.0 + sc_ref[0]) + sh_ref[0]).astype(BF16)

    def proj(r0, width):
        return _dot_nt(hb, wt_ref[r0:r0 + width, :])

    r_u, r_va, r_q, r_k, r_v, r_r = (seg[n] for n in ("u", "va", "q", "k", "v", "r"))
    r_ga, r_gb = seg["ga"], seg["gb"]
    rank = wdec2_ref.shape[1]
    groups, chunk_a = ws_ref.shape[0], ws_ref.shape[1]
    dk = dk_tot // GLA_HEADS
    gw = d_a // groups
    blk = PROJ_COL_TILE
    cols = lambda j: slice(j * blk, (j + 1) * blk)
    gu, gv, vn = {}, {}, {}

    stage = {}

    def store(out_ref, j, fn):
        def ep(y):
            out_ref[0, :, cols(j)] = fn(y).astype(out_ref.dtype)
        return ep

    def keep(table, j, fn):
        def ep(y):
            table[j] = fn(y)
        return ep

    def add(name, matmul, epilogue):
        stage[name] = (matmul, epilogue)
        return name

    s_dec = add("dec", lambda: proj(seg["dec"], 2 * rank),
                lambda dec: _store_log_decays(dec, wdec2_ref, bdec_ref, (lgf_ref, lgb_ref)))
    s_u = [add(f"u{j}", lambda j=j: proj(r_u + j * blk, blk), keep(gu, j, _gelu_tanh))
           for j in range(d_a // blk)]
    s_va = [add(f"va{j}", lambda j=j: proj(r_va + j * blk, blk), keep(gv, j, _gelu_tanh))
            for j in range(d_a // blk)]

    def ep_q_and_layernorm(y):
        q_ref[0] = (y * dk ** -0.5).astype(BF16)
        nb = d_a // blk
        mu = sum(jnp.sum(gv[j], axis=-1, keepdims=True) for j in range(nb)) * (1.0 / d_a)
        cen = [gv[j] - mu for j in range(nb)]
        var = sum(jnp.sum(c * c, axis=-1, keepdims=True) for c in cen) * (1.0 / d_a)
        inv = lax.rsqrt(var + EPS)
        for j in range(nb):
            vn[j] = (cen[j] * inv * lng_ref[:, cols(j)] + lnb_ref[:, cols(j)]).astype(BF16)
    s_q = add("q", lambda: proj(r_q, dk_tot), ep_q_and_layernorm)
    s_k = add("k", lambda: proj(r_k, dk_tot), store(k_ref, 0, lambda y: y))
    s_r = [add(f"r{j}", lambda j=j: proj(r_r + j * blk, blk),
               store(sr_ref, j, lambda y: y * _sigmoid(y))) for j in range(dv_tot // blk)]
    s_v = [add(f"v{j}", lambda j=j: proj(r_v + j * blk, blk), store(v_ref, j, lambda y: y))
           for j in range(dv_tot // blk)]

    def mm_gate(ci, g):
        r0, c0 = ci * chunk_a, g * gw
        return _dot(ws_ref[g].astype(BF16),
                    vn[c0 // blk][r0:r0 + chunk_a, c0 % blk:c0 % blk + gw])

    def ep_gate(ci, g):
        r0, c0 = ci * chunk_a, g * gw

        def ep(s):
            u = gu[c0 // blk][r0:r0 + chunk_a, c0 % blk:c0 % blk + gw]
            ya_scr[r0:r0 + chunk_a, c0:c0 + gw] = (u * (s + bs_ref[g])).astype(BF16)
        return ep
    s_mix = [add(f"mix{ci}_{g}", lambda ci=ci, g=g: mm_gate(ci, g), ep_gate(ci, g))
             for ci in range(tm // chunk_a) for g in range(groups)]
    s_gb = [add(f"gb{j}", lambda j=j: proj(r_gb + j * blk, blk), store(gb_ref, j, _sigmoid))
            for j in range(d // blk)]
    s_ga = [add(f"ga{j}", lambda j=j: (proj(r_ga + j * blk, blk),
                                       _dot(ya_scr[...], pa_ref[:, cols(j)])),
                store(ma_ref, j, lambda y: _sigmoid(y[0]) * y[1])) for j in range(d // blk)]

    order = [s_dec] + s_u + s_va + [s_q, s_k] + s_r + s_mix + s_gb + s_ga + s_v
    assert sorted(order) == sorted(stage)
    pending = stage[order[0]][0]()
    for i, name in enumerate(order):
        nxt = stage[order[i + 1]][0]() if i + 1 < len(order) else None
        stage[name][1](pending)
        pending = nxt


def _inproj_call(x4, pos, sh, sc, consts, weights, seg, d_a, dk_tot, dv_tot):
    bsz, nrow, gw_, d = x4.shape
    t = nrow * gw_
    tm = TOKEN_TILE
    rows_per_tile = tm // GRID_W
    rowemb, colemb = pos
    chunk_a = consts[5].shape[1]
    assert tm % chunk_a == 0 and tm % GRID_W == 0 and t % tm == 0

    tok = lambda w: pl.BlockSpec((1, tm, w), lambda b, i: (b, i, 0))
    per_batch = pl.BlockSpec((1, 1, d), lambda b, i: (b, 0, 0))
    in_specs = ([pl.BlockSpec((1, rows_per_tile, GRID_W, d), lambda b, i: (b, i, 0, 0)),
                 pl.BlockSpec((rows_per_tile, 1, d // 2), lambda b, i: (i, 0, 0)),
                 _const_spec(colemb.shape), per_batch, per_batch]
                + [_const_spec(a.shape) for a in consts]
                + [pl.BlockSpec(memory_space=pl.ANY)] * len(weights))
    widths = [d, d, dk_tot, dk_tot, dv_tot, dv_tot, dk_tot, dk_tot]
    dtypes = [BF16] * 6 + [F32] * 2
    kern = functools.partial(_inproj_kernel, seg=seg, d_a=d_a, dk_tot=dk_tot, dv_tot=dv_tot)
    return pl.pallas_call(
        kern,
        out_shape=[jax.ShapeDtypeStruct((bsz, t, w), dt) for w, dt in zip(widths, dtypes)],
        grid=(bsz, t // tm),
        in_specs=in_specs,
        out_specs=[tok(w) for w in widths],
        scratch_shapes=([pltpu.VMEM((tm, d_a), BF16)]
                        + [pltpu.VMEM(w.shape, BF16) for w in weights]
                        + [pltpu.VMEM((2, WEIGHT_STAGE_ROWS, d), F32),
                           pltpu.SemaphoreType.DMA((2,))]),
        compiler_params=pltpu.CompilerParams(dimension_semantics=("arbitrary", "arbitrary"),
                                             vmem_limit_bytes=VMEM_LIMIT),
        name="inproj",
    )(x4, rowemb, colemb, sh, sc, *consts, *weights)


def _scan_masks(c):
    row = lax.broadcasted_iota(jnp.int32, (c, c), 0)
    col = lax.broadcasted_iota(jnp.int32, (c, c), 1)
    return ((row >= col, c - 1), (row <= col, 0))


def _chunk_cumsum(lg, keep, end):
    tri = jnp.where(keep, 1.0, 0.0).astype(BF16)
    hi, lo = _split_bf16(lg)
    b = _dot(jnp.concatenate([tri, tri], axis=1), jnp.concatenate([hi, lo], axis=0))
    return b, b[end:end + 1, :]


def _decay_rows(st, tot):
    dk, dv = st.shape
    dcol = jnp.transpose(jnp.broadcast_to(jnp.exp(tot), (dk, dk)))
    return st * jnp.concatenate([dcol] * (dv // dk), axis=1)


def _state_kernel(kf_ref, vf_ref, lgf_ref, kb_ref, vb_ref, lgb_ref, s_ref, st_scr, *, dk, dv):
    t = pl.program_id(0)
    bsz, c = kf_ref.shape[0], SCAN_CHUNK
    n_sub = kf_ref.shape[1] // c

    @pl.when(t == 0)
    def _():
        st_scr[...] = jnp.zeros_like(st_scr)

    masks = _scan_masks(c)
    dirs = ((kf_ref, vf_ref, lgf_ref), (kb_ref, vb_ref, lgb_ref))
    for sub in range(n_sub):
        for bi in range(bsz):
            for di, (k_ref, v_ref, lg_ref) in enumerate(dirs):
                first = (sub if di == 0 else n_sub - 1 - sub) * c
                b, tot = _chunk_cumsum(lg_ref[bi, first:first + c, :], *masks[di])
                for h in range(GLA_HEADS):
                    sl = slice(h * dk, (h + 1) * dk)
                    k = k_ref[bi, first:first + c, sl].astype(F32)
                    kd = (k * jnp.exp(tot[:, sl] - b[:, sl])).astype(BF16)
                    v = v_ref[bi, first:first + c, h * dv:(h + 1) * dv]
                    st_scr[bi, di, h] = _decay_rows(st_scr[bi, di, h], tot[:, sl]) + _dot_tn(kd, v)

    @pl.when(t == pl.num_programs(0) - 1)
    def _():
        s_ref[...] = st_scr[...]


def _state_call(k, v, lgf, lgb):
    bsz, t, dk_tot = k.shape
    dv_tot = v.shape[-1]
    dk, dv = dk_tot // GLA_HEADS, dv_tot // GLA_HEADS
    tile = min(SCAN_TILE, t)
    nt = t // tile
    assert t % tile == 0 and tile % SCAN_CHUNK == 0 and dv % dk == 0
    fwd = lambda w: pl.BlockSpec((bsz, tile, w), lambda i: (0, i, 0))
    bwd = lambda w: pl.BlockSpec((bsz, tile, w), lambda i: (0, nt - 1 - i, 0))
    shape = (bsz, 2, GLA_HEADS, dk, dv)
    return pl.pallas_call(
        functools.partial(_state_kernel, dk=dk, dv=dv),
        out_shape=jax.ShapeDtypeStruct(shape, F32),
        grid=(nt,),
        in_specs=[fwd(dk_tot), fwd(dv_tot), fwd(dk_tot), bwd(dk_tot), bwd(dv_tot), bwd(dk_tot)],
        out_specs=pl.BlockSpec(shape, lambda i: (0,) * len(shape)),
        scratch_shapes=[pltpu.VMEM(shape, F32)],
        compiler_params=pltpu.CompilerParams(dimension_semantics=("arbitrary",),
                                             vmem_limit_bytes=VMEM_LIMIT),
        name="gla_ctx_state",
    )(k, v, lgf, k, v, lgb)


def _scan_kernel(qf_ref, kf_ref, vf_ref, lgf_ref, qb_ref, kb_ref, vb_ref, lgb_ref, s0_ref,
                 of_ref, ob_ref, st_scr, q32_scr, k32_scr, v32_scr, b_scr, o_scr,
                 *, dk, dv):
    t = pl.program_id(0)
    bsz, c = qf_ref.shape[0], SCAN_CHUNK
    n_sub = qf_ref.shape[1] // c

    def rows(di, sub):
        first = (sub if di == 0 else n_sub - 1 - sub) * c
        return slice(first, first + c)

    @pl.when(t == 0)
    def _():
        st_scr[...] = s0_ref[...]

    masks = _scan_masks(c)
    dirs = ((qf_ref, kf_ref, vf_ref, lgf_ref, of_ref) + masks[0],
            (qb_ref, kb_ref, vb_ref, lgb_ref, ob_ref) + masks[1])

    worst = None
    for lg_ref in (lgf_ref, lgb_ref):
        for sub in range(n_sub):
            w = jnp.max(-jnp.sum(lg_ref[:, sub * c:(sub + 1) * c, :], axis=1))
            worst = w if worst is None else jnp.maximum(worst, w)

    cums = {}

    def cumulate(bi, di, sub):
        lg_ref, keep, end = dirs[di][3], dirs[di][5], dirs[di][6]
        cums[bi, di, sub] = _chunk_cumsum(lg_ref[bi, rows(di, sub), :], keep, end)

    def load(bi, di, h, sub):
        q_ref, k_ref, v_ref = dirs[di][:3]
        b, tot = cums[bi, di, sub]
        sl = slice(h * dk, (h + 1) * dk)
        return (q_ref[bi, rows(di, sub), sl].astype(F32), k_ref[bi, rows(di, sub), sl].astype(F32),
                v_ref[bi, rows(di, sub), h * dv:(h + 1) * dv], b[:, sl], tot[:, sl])

    def decayed_state(bi, di, h, tot):
        return _decay_rows(st_scr[bi, di, h], tot)

    def fast_step():
        chains = [(bi, di, h, sub) for sub in range(n_sub) for bi in range(bsz)
                  for di in range(2) for h in range(GLA_HEADS)]
        live = {}

        def factors(n):
            q, k, v, bh, tot = load(*chains[n])
            half = 0.5 * tot
            eh = jnp.exp(half)
            qe = q * jnp.exp(bh - half)
            ke = k * jnp.exp(half - bh)
            live[n] = dict(v=v, tot=tot, qe=qe.astype(BF16), ke=ke.astype(BF16),
                           qd=(qe * eh).astype(BF16), kd=(ke * eh).astype(BF16))

        def scores(n):
            live[n]["a"] = _dot_nt(live[n].pop("qe"), live[n].pop("ke"))

        def outputs(n):
            bi, di, h, _ = chains[n]
            c_ = live[n]
            a = jnp.where(dirs[di][5], c_.pop("a"), 0.0).astype(BF16)
            lhs = jnp.concatenate([a, c_.pop("qd")], axis=1)
            rhs = jnp.concatenate([c_["v"], st_scr[bi, di, h].astype(BF16)], axis=0)
            c_["o"] = _dot(lhs, rhs)
            c_["u"] = _dot_tn(c_.pop("kd"), c_.pop("v"))

        def finish(n):
            bi, di, h, sub = chains[n]
            c_ = live.pop(n)
            o_ref = dirs[di][4]
            o_ref[bi, rows(di, sub), h * dv:(h + 1) * dv] = c_["o"].astype(o_ref.dtype)
            st_scr[bi, di, h] = decayed_state(bi, di, h, c_["tot"]) + c_["u"]

        def cumulate_for(n):
            bi, di, _, sub = chains[n]
            cumulate(bi, di, sub)

        phases = ((factors, 0), (scores, 1), (outputs, 3), (finish, 5))
        cumulate_for(0)
        for i in range(len(chains) + phases[-1][1]):
            ahead = i + GLA_HEADS
            if i % GLA_HEADS == 0 and ahead < len(chains):
                cumulate_for(ahead)
            for phase, lag in phases:
                if 0 <= i - lag < len(chains):
                    phase(i - lag)

    def safe_step():
        ridx = lax.broadcasted_iota(jnp.int32, (c, 1), 0)
        groups = [(s, b, d) for s in range(n_sub) for b in range(bsz) for d in range(2)]
        for sub, bi, di in groups:
            o_ref = dirs[di][4]
            cumulate(bi, di, sub)
            for h in range(GLA_HEADS):
                q, k, v, bh, tot = load(bi, di, h, sub)
                q32_scr[...] = q
                k32_scr[...] = k
                v32_scr[...] = v.astype(F32)
                b_scr[...] = bh
                o_scr[...] = _dot((q * jnp.exp(bh)).astype(BF16), st_scr[bi, di, h].astype(BF16))
                st_scr[bi, di, h] = (decayed_state(bi, di, h, tot)
                                     + _dot_tn((k * jnp.exp(tot - bh)).astype(BF16), v))

                def body(j, carry):
                    bj = b_scr[pl.ds(j, 1), :]
                    valid = (ridx >= j) if di == 0 else (ridx <= j)
                    e = jnp.exp(jnp.where(valid, b_scr[...] - bj, -jnp.inf))
                    w = jnp.sum(q32_scr[...] * e * k32_scr[pl.ds(j, 1), :], axis=-1, keepdims=True)
                    o_scr[...] += w * v32_scr[pl.ds(j, 1), :]
                    return carry

                lax.fori_loop(0, c, body, 0)
                o_ref[bi, rows(di, sub), h * dv:(h + 1) * dv] = o_scr[...].astype(o_ref.dtype)

    lax.cond(worst <= SAFE_DECAY, fast_step, safe_step)


def _scan_call(q, k, v, lgf, lgb, s0):
    bsz, t, dk_tot = q.shape
    dv_tot = v.shape[-1]
    dk, dv = dk_tot // GLA_HEADS, dv_tot // GLA_HEADS
    c = SCAN_CHUNK
    tile = SCAN_TILE
    nt = t // tile
    assert t % tile == 0 and tile % c == 0 and dv % dk == 0
    fwd = lambda w: pl.BlockSpec((bsz, tile, w), lambda i: (0, i, 0))
    bwd = lambda w: pl.BlockSpec((bsz, tile, w), lambda i: (0, nt - 1 - i, 0))
    st_spec = pl.BlockSpec(s0.shape, lambda i: (0,) * s0.ndim)
    kern = functools.partial(_scan_kernel, dk=dk, dv=dv)
    return pl.pallas_call(
        kern,
        out_shape=[jax.ShapeDtypeStruct((bsz, t, dv_tot), BF16),
                   jax.ShapeDtypeStruct((bsz, t, dv_tot), BF16)],
        grid=(nt,),
        in_specs=[fwd(dk_tot), fwd(dk_tot), fwd(dv_tot), fwd(dk_tot),
                  bwd(dk_tot), bwd(dk_tot), bwd(dv_tot), bwd(dk_tot), st_spec],
        out_specs=[fwd(dv_tot), bwd(dv_tot)],
        scratch_shapes=[pltpu.VMEM(s0.shape, F32),
                        pltpu.VMEM((c, dk), F32), pltpu.VMEM((c, dk), F32),
                        pltpu.VMEM((c, dv), F32), pltpu.VMEM((c, dk), F32),
                        pltpu.VMEM((c, dv), F32)],
        compiler_params=pltpu.CompilerParams(dimension_semantics=("arbitrary",),
                                             vmem_limit_bytes=VMEM_LIMIT),
        name="gla_scan",
    )(q, k, v, lgf, q, k, v, lgb, s0)


def _tail_kernel(x_ref, row_ref, col_ref, ma_ref, gb_ref, of_ref, ob_ref, sr_ref,
                 g1_ref, sh2_ref, sc2_ref, g2_ref, og_ref, n2_ref, fg_ref,
                 pb_hbm, wo_hbm, w1_hbm, w2_hbm, out_ref,
                 pb_ref, wo_ref, w1_ref, w2_ref, stage_ref, sem_ref, *, dv):
    @pl.when((pl.program_id(0) == 0) & (pl.program_id(1) == 0))
    def _():
        for src, dst in ((pb_hbm, pb_ref), (wo_hbm, wo_ref), (w1_hbm, w1_ref), (w2_hbm, w2_ref)):
            _load_as_bf16(src, dst, stage_ref, sem_ref, _stage_rows(src.shape[0]))

    xp = _load_tokens(x_ref, row_ref, col_ref)
    o = of_ref[0].astype(F32) + ob_ref[0].astype(F32)
    heads = []
    for h in range(GLA_HEADS):
        sl = slice(h * dv, (h + 1) * dv)
        heads.append(_rms(o[:, sl], og_ref[:, sl]))
    yb = (jnp.concatenate(heads, axis=-1) * sr_ref[0].astype(F32)).astype(BF16)
    merged = ma_ref[0].astype(F32) + gb_ref[0].astype(F32) * _dot(yb, pb_ref[...])
    x1 = xp + g1_ref[0] * _dot(merged.astype(BF16), wo_ref[...])
    h2 = (_rms(x1, n2_ref[...]) * (1.0 + sc2_ref[0]) + sh2_ref[0]).astype(BF16)
    acc = jnp.zeros_like(x1)
    d_ff = w1_ref.shape[1]
    for j in range(d_ff // FF_COL_TILE):
        sl = slice(j * FF_COL_TILE, (j + 1) * FF_COL_TILE)
        f = jnp.maximum(_dot(h2, w1_ref[:, sl]), 0.0)
        acc = acc + _dot((f * f).astype(BF16), w2_ref[sl, :])
    x2 = x1 + g2_ref[0] * acc
    out_ref[0] = _rms(x2, fg_ref[...])


def _tail_call(x4, pos, acts, mods, consts, weights):
    bsz, nrow, gw_, d = x4.shape
    t = nrow * gw_
    tm = TOKEN_TILE
    rows_per_tile = tm // GRID_W
    rowemb, colemb = pos
    dv = acts[2].shape[-1] // GLA_HEADS
    tok = lambda w: pl.BlockSpec((1, tm, w), lambda b, i: (b, i, 0))
    per_batch = pl.BlockSpec((1, 1, d), lambda b, i: (b, 0, 0))
    in_specs = ([pl.BlockSpec((1, rows_per_tile, GRID_W, d), lambda b, i: (b, i, 0, 0)),
                 pl.BlockSpec((rows_per_tile, 1, d // 2), lambda b, i: (i, 0, 0)),
                 _const_spec(colemb.shape)]
                + [tok(a.shape[-1]) for a in acts]
                + [per_batch] * len(mods)
                + [_const_spec(a.shape) for a in consts]
                + [pl.BlockSpec(memory_space=pl.ANY)] * len(weights))
    return pl.pallas_call(
        functools.partial(_tail_kernel, dv=dv),
        out_shape=jax.ShapeDtypeStruct((bsz, t, d), F32),
        grid=(bsz, t // tm),
        in_specs=in_specs,
        out_specs=tok(d),
        scratch_shapes=([pltpu.VMEM(w.shape, BF16) for w in weights]
                        + [pltpu.VMEM((2, WEIGHT_STAGE_ROWS, d), F32),
                           pltpu.SemaphoreType.DMA((2,))]),
        compiler_params=pltpu.CompilerParams(dimension_semantics=("arbitrary", "arbitrary"),
                                             vmem_limit_bytes=VMEM_LIMIT),
        name="tail",
    )(x4, rowemb, colemb, *acts, *mods, *consts, *weights)


def _pos_tables(t, d):
    quarter = d // 4
    freqs = 1.0 / (10000.0 ** (jnp.arange(quarter, dtype=F32) / quarter))

    def emb(n):
        a = jnp.arange(n, dtype=F32)[:, None] * freqs[None, :]
        return jnp.concatenate([jnp.sin(a), jnp.cos(a)], -1)

    return emb(t // GRID_W)[:, None, :], emb(GRID_W)[None, :, :]


def kernel(x, c, ctx, c_ctx, w_ada, b_ada, norm1_g, w_in, lnv_g, lnv_b, w_s, b_s, w_dec2, b_dec,
           onorm_g, proj_a, proj_b, w_out, norm2_g, w_ff1, w_ff2, final_g):
    bsz, t, d = x.shape
    depth = w_in.shape[0]
    assert depth == 1, "single-layer problem: the context stream only feeds the scan states"
    d_a = proj_a.shape[1]
    dv_tot = proj_b.shape[1]
    dk_tot = w_dec2.shape[-1]
    seg = _segment_rows(d, d_a, dk_tot, dv_tot, w_dec2.shape[2])

    c_rows = jnp.zeros((MOD_ROWS, d), F32).at[:bsz].set(c).at[bsz].set(c_ctx)
    mod = _mod_call(c_rows, w_ada[0], b_ada[0])
    sh1, sc1, g1, sh2, sc2, g2 = jnp.split(mod, 6, axis=-1)
    latent = lambda m: m[:bsz, None, :]
    context = lambda m: m[bsz:bsz + 1]

    row = lambda a: a.reshape(1, -1)
    g1n = row(norm1_g[0])
    w_in_t = jnp.swapaxes(w_in[0], 0, 1)
    wd2, bd = w_dec2[0], b_dec[0][:, None, :]

    ck, cv, clgf, clgb = _ctx_proj_call(ctx, context(sh1), context(sc1), g1n, w_in_t, wd2, bd,
                                        seg, dk_tot, dv_tot)
    s_ctx = _state_call(ck, cv, clgf, clgb)

    pos = _pos_tables(t, d)
    x4 = x.reshape(bsz, t // GRID_W, GRID_W, d)
    consts = (g1n, wd2, bd, row(lnv_g[0]), row(lnv_b[0]), w_s[0], b_s[0][:, :, None])
    ma, gb, q, k, v, sr, lgf, lgb = _inproj_call(x4, pos, latent(sh1), latent(sc1), consts,
                                                 (w_in_t, proj_a[0]), seg, d_a, dk_tot, dv_tot)
    o_f, o_b = _scan_call(q, k, v, lgf, lgb, s_ctx)

    mods = (latent(g1), latent(sh2), latent(sc2), latent(g2))
    return _tail_call(x4, pos, (ma, gb, o_f, o_b, sr), mods,
                      (row(onorm_g[0]), row(norm2_g[0]), row(final_g)),
                      (proj_b[0], w_out[0], w_ff1[0], w_ff2[0]))
```

```python
import functools
import math

import jax
import jax.numpy as jnp
from jax import lax
from jax.experimental import pallas as pl
from jax.experimental.pallas import tpu as pltpu

F32 = jnp.float32
BF16 = jnp.bfloat16

GRID_W = 64
GLA_HEADS = 4
GATE_TAU = 16.0
EPS = 1e-6

TOKEN_TILE = 512
SCAN_CHUNK = 128
SCAN_TILE = 256
MOD_ROWS = 8
MOD_COL_TILE = 1024
PROJ_COL_TILE = 512
WEIGHT_STAGE_ROWS = 512
WEIGHT_DMA_PARTS = 4
F32_SUBLANES = 8
BF16_SUBLANES = 16
FF_COL_TILE = 1024
SAFE_DECAY = 60.0
VMEM_LIMIT = 56 * 1024 * 1024


def _sigmoid(x):
    return 1.0 / (1.0 + jnp.exp(-x))


def _gelu_tanh(x):
    c = math.sqrt(2.0 / math.pi)
    return 0.5 * x * (1.0 + jnp.tanh(c * (x + 0.044715 * (x * x * x))))


def _log_sigmoid(z):
    return jnp.minimum(z, 0.0) - jnp.log(1.0 + jnp.exp(-jnp.abs(z)))


def _rms(x, g):
    return x * lax.rsqrt(jnp.mean(x * x, axis=-1, keepdims=True) + EPS) * g


def _split_bf16(a):
    hi = a.astype(BF16)
    lo = (a - hi.astype(F32)).astype(BF16)
    return hi, lo


def _dot(a, b):
    return jnp.dot(a, b, preferred_element_type=F32)


def _dot_nt(a, b):
    return lax.dot_general(a, b, (((1,), (1,)), ((), ())), preferred_element_type=F32)


def _dot_tn(a, b):
    return lax.dot_general(a, b, (((0,), (0,)), ((), ())), preferred_element_type=F32)


def _const_spec(shape):
    nd = len(shape)
    return pl.BlockSpec(shape, lambda *_: (0,) * nd, pipeline_mode=pl.Buffered(1))


def _load_as_bf16(src_ref, dst_ref, stage_ref, sem_ref, rows):
    n_rows, n_cols = src_ref.shape
    cols = stage_ref.shape[2]
    parts = sem_ref.shape[1]
    assert n_rows % rows == 0 and rows <= stage_ref.shape[1] and n_cols % cols == 0
    assert rows % (parts * F32_SUBLANES) == 0
    n = n_rows // rows
    sub = rows // parts

    for c0 in range(0, n_cols, cols):
        def copies(j, slot, c0=c0):
            return [pltpu.make_async_copy(
                src_ref.at[pl.ds(j * rows + p * sub, sub), pl.ds(c0, cols)],
                stage_ref.at[slot, pl.ds(p * sub, sub), :], sem_ref.at[slot, p])
                for p in range(parts)]

        for cp in copies(0, 0):
            cp.start()

        def body(j, carry, c0=c0, copies=copies):
            slot = j % 2

            @pl.when(j + 1 < n)
            def _():
                for cp in copies(j + 1, 1 - slot):
                    cp.start()

            for cp in copies(j, slot):
                cp.wait()
            first = pl.multiple_of(j * rows, rows)
            dst_ref[pl.ds(first, rows), c0:c0 + cols] = (
                stage_ref[slot, pl.ds(0, rows), :].astype(BF16))
            return carry

        lax.fori_loop(0, n, body, 0)


def _stage_rows(n_rows):
    return max(r for r in range(BF16_SUBLANES, WEIGHT_STAGE_ROWS + 1, BF16_SUBLANES)
               if n_rows % r == 0)


def _mod_kernel(c_ref, w_ref, b_ref, o_ref):
    c = c_ref[...]
    a_hi, a_lo = _split_bf16(c * _sigmoid(c))
    w_hi, w_lo = _split_bf16(w_ref[...])
    o_ref[...] = _dot(a_hi, w_hi) + _dot(a_lo, w_hi) + _dot(a_hi, w_lo) + b_ref[...]


def _mod_call(c_rows, w_ada, b_ada):
    d, n = w_ada.shape
    return pl.pallas_call(
        _mod_kernel,
        out_shape=jax.ShapeDtypeStruct((MOD_ROWS, n), F32),
        grid=(n // MOD_COL_TILE,),
        in_specs=[pl.BlockSpec((MOD_ROWS, d), lambda j: (0, 0)),
                  pl.BlockSpec((d, MOD_COL_TILE), lambda j: (0, j)),
                  pl.BlockSpec((1, MOD_COL_TILE), lambda j: (0, j))],
        out_specs=pl.BlockSpec((MOD_ROWS, MOD_COL_TILE), lambda j: (0, j)),
        compiler_params=pltpu.CompilerParams(dimension_semantics=("arbitrary",)),
        name="adaln_mod",
    )(c_rows, w_ada, b_ada.reshape(1, n))


def _load_tokens(x_ref, row_ref, col_ref):
    x = x_ref[0]
    half = x.shape[-1] // 2
    x = jnp.concatenate([x[..., :half] + row_ref[...], x[..., half:] + col_ref[...]], axis=-1)
    return x.reshape(x.shape[0] * x.shape[1], x.shape[2])


def _segment_rows(d, d_a, dk_tot, dv_tot, rank):
    names = ("u", "va", "q", "k", "v", "r", "dec", "ga", "gb")
    widths = (d_a, d_a, dk_tot, dk_tot, dv_tot, dv_tot, 2 * rank, d, d)
    offsets, r0 = {}, 0
    for name, w in zip(names, widths):
        offsets[name] = r0
        r0 += w
    return offsets


def _store_log_decays(dec, wdec2_ref, bdec_ref, out_refs):
    rank = wdec2_ref.shape[1]
    for di, out_ref in enumerate(out_refs):
        z = _dot(dec[:, di * rank:(di + 1) * rank].astype(BF16),
                 wdec2_ref[di].astype(BF16)) + bdec_ref[di]
        out_ref[0] = _log_sigmoid(z) * (1.0 / GATE_TAU)


def _ctx_proj_kernel(x_ref, sh_ref, sc_ref, g1_ref, wk_ref, wv_ref, wd_ref, wdec2_ref, bdec_ref,
                     k_ref, v_ref, lgf_ref, lgb_ref):
    hb = (_rms(x_ref[0], g1_ref[...]) * (1.0 + sc_ref[...]) + sh_ref[...]).astype(BF16)
    proj = lambda w_ref: _dot_nt(hb, w_ref[...].astype(BF16))
    _store_log_decays(proj(wd_ref), wdec2_ref, bdec_ref, (lgf_ref, lgb_ref))
    k_ref[0] = proj(wk_ref).astype(BF16)
    v_ref[0] = proj(wv_ref).astype(BF16)


def _ctx_proj_call(ctx, sh, sc, g1, w_in_t, w_dec2, b_dec, seg, dk_tot, dv_tot):
    bsz, t, d = ctx.shape
    tm = min(TOKEN_TILE, t)
    assert t % tm == 0
    tok = lambda w: pl.BlockSpec((1, tm, w), lambda b, i: (b, i, 0))

    def rows_spec(r0, width):
        assert r0 % width == 0
        return pl.BlockSpec((width, d), lambda b, i: (r0 // width, 0),
                            pipeline_mode=pl.Buffered(1))

    rank = w_dec2.shape[1]
    consts = [sh, sc, g1]
    tail_consts = [w_dec2, b_dec]
    widths = [dk_tot, dv_tot, dk_tot, dk_tot]
    dtypes = [BF16, BF16, F32, F32]
    return pl.pallas_call(
        _ctx_proj_kernel,
        out_shape=[jax.ShapeDtypeStruct((bsz, t, w), dt) for w, dt in zip(widths, dtypes)],
        grid=(bsz, t // tm),
        in_specs=([tok(d)] + [_const_spec(a.shape) for a in consts]
                  + [rows_spec(seg["k"], dk_tot), rows_spec(seg["v"], dv_tot),
                     rows_spec(seg["dec"], 2 * rank)]
                  + [_const_spec(a.shape) for a in tail_consts]),
        out_specs=[tok(w) for w in widths],
        compiler_params=pltpu.CompilerParams(dimension_semantics=("arbitrary", "arbitrary"),
                                             vmem_limit_bytes=VMEM_LIMIT),
        name="ctx_proj",
    )(ctx, *consts, w_in_t, w_in_t, w_in_t, *tail_consts)


def _inproj_kernel(x_ref, row_ref, col_ref, sh_ref, sc_ref, g1_ref, wdec2_ref, bdec_ref,
                   lng_ref, lnb_ref, ws_ref, bs_ref, wt_hbm, pa_hbm,
                   ma_ref, gb_ref, q_ref, k_ref, v_ref, sr_ref, lgf_ref, lgb_ref,
                   ya_scr, wt_ref, pa_ref, stage_ref, sem_ref, *, seg, d_a, dk_tot, dv_tot):
    @pl.when((pl.program_id(0) == 0) & (pl.program_id(1) == 0))
    def _():
        for src, dst in ((wt_hbm, wt_ref), (pa_hbm, pa_ref)):
            _load_as_bf16(src, dst, stage_ref, sem_ref, _stage_rows(src.shape[0]))

    xp = _load_tokens(x_ref, row_ref, col_ref)
    tm = xp.shape[0]
    d = xp.shape[1]
    hb = (_rms(xp, g1_ref[...]) * (1.0 + sc_ref[0]) + sh_ref[0]).astype(BF16)

    def proj(r0, width):
        return _dot_nt(hb, wt_ref[r0:r0 + width, :])

    r_u, r_va, r_q, r_k, r_v, r_r = (seg[n] for n in ("u", "va", "q", "k", "v", "r"))
    r_ga, r_gb = seg["ga"], seg["gb"]
    rank = wdec2_ref.shape[1]
    groups, chunk_a = ws_ref.shape[0], ws_ref.shape[1]
    dk = dk_tot // GLA_HEADS
    gw = d_a // groups
    blk = PROJ_COL_TILE
    cols = lambda j: slice(j * blk, (j + 1) * blk)
    gu, gv, vn = {}, {}, {}

    stage = {}

    def store(out_ref, j, fn):
        def ep(y):
            out_ref[0, :, cols(j)] = fn(y).astype(out_ref.dtype)
        return ep

    def keep(table, j, fn):
        def ep(y):
            table[j] = fn(y)
        return ep

    def add(name, matmul, epilogue):
        stage[name] = (matmul, epilogue)
        return name

    s_dec = add("dec", lambda: proj(seg["dec"], 2 * rank),
                lambda dec: _store_log_decays(dec, wdec2_ref, bdec_ref, (lgf_ref, lgb_ref)))
    s_u = [add(f"u{j}", lambda j=j: proj(r_u + j * blk, blk), keep(gu, j, _gelu_tanh))
           for j in range(d_a // blk)]
    s_va = [add(f"va{j}", lambda j=j: proj(r_va + j * blk, blk), keep(gv, j, _gelu_tanh))
            for j in range(d_a // blk)]

    def ep_q_and_layernorm(y):
        q_ref[0] = (y * dk ** -0.5).astype(BF16)
        nb = d_a // blk
        mu = sum(jnp.sum(gv[j], axis=-1, keepdims=True) for j in range(nb)) * (1.0 / d_a)
        cen = [gv[j] - mu for j in range(nb)]
        var = sum(jnp.sum(c * c, axis=-1, keepdims=True) for c in cen) * (1.0 / d_a)
        inv = lax.rsqrt(var + EPS)
        for j in range(nb):
            vn[j] = (cen[j] * inv * lng_ref[:, cols(j)] + lnb_ref[:, cols(j)]).astype(BF16)
    s_q = add("q", lambda: proj(r_q, dk_tot), ep_q_and_layernorm)
    s_k = add("k", lambda: proj(r_k, dk_tot), store(k_ref, 0, lambda y: y))
    s_r = [add(f"r{j}", lambda j=j: proj(r_r + j * blk, blk),
               store(sr_ref, j, lambda y: y * _sigmoid(y))) for j in range(dv_tot // blk)]
    s_v = [add(f"v{j}", lambda j=j: proj(r_v + j * blk, blk), store(v_ref, j, lambda y: y))
           for j in range(dv_tot // blk)]

    def mm_gate(ci, g):
        r0, c0 = ci * chunk_a, g * gw
        return _dot(ws_ref[g].astype(BF16),
                    vn[c0 // blk][r0:r0 + chunk_a, c0 % blk:c0 % blk + gw])

    def ep_gate(ci, g):
        r0, c0 = ci * chunk_a, g * gw

        def ep(s):
            u = gu[c0 // blk][r0:r0 + chunk_a, c0 % blk:c0 % blk + gw]
            ya_scr[r0:r0 + chunk_a, c0:c0 + gw] = (u * (s + bs_ref[g])).astype(BF16)
        return ep
    s_mix = [add(f"mix{ci}_{g}", lambda ci=ci, g=g: mm_gate(ci, g), ep_gate(ci, g))
             for ci in range(tm // chunk_a) for g in range(groups)]
    s_gb = [add(f"gb{j}", lambda j=j: proj(r_gb + j * blk, blk), store(gb_ref, j, _sigmoid))
            for j in range(d // blk)]
    s_ga = [add(f"ga{j}", lambda j=j: (proj(r_ga + j * blk, blk),
                                       _dot(ya_scr[...], pa_ref[:, cols(j)])),
                store(ma_ref, j, lambda y: _sigmoid(y[0]) * y[1])) for j in range(d // blk)]

    order = [s_dec] + s_u + s_va + [s_q, s_k] + s_r + s_mix + s_gb + s_ga + s_v
    assert sorted(order) == sorted(stage)
    pending = stage[order[0]][0]()
    for i, name in enumerate(order):
        nxt = stage[order[i + 1]][0]() if i + 1 < len(order) else None
        stage[name][1](pending)
        pending = nxt


def _inproj_call(x4, pos, sh, sc, consts, weights, seg, d_a, dk_tot, dv_tot):
    bsz, nrow, gw_, d = x4.shape
    t = nrow * gw_
    tm = TOKEN_TILE
    rows_per_tile = tm // GRID_W
    rowemb, colemb = pos
    chunk_a = consts[5].shape[1]
    assert tm % chunk_a == 0 and tm % GRID_W == 0 and t % tm == 0

    tok = lambda w: pl.BlockSpec((1, tm, w), lambda b, i: (b, i, 0))
    per_batch = pl.BlockSpec((1, 1, d), lambda b, i: (b, 0, 0))
    in_specs = ([pl.BlockSpec((1, rows_per_tile, GRID_W, d), lambda b, i: (b, i, 0, 0)),
                 pl.BlockSpec((rows_per_tile, 1, d // 2), lambda b, i: (i, 0, 0)),
                 _const_spec(colemb.shape), per_batch, per_batch]
                + [_const_spec(a.shape) for a in consts]
                + [pl.BlockSpec(memory_space=pl.ANY)] * len(weights))
    widths = [d, d, dk_tot, dk_tot, dv_tot, dv_tot, dk_tot, dk_tot]
    dtypes = [BF16] * 6 + [F32] * 2
    kern = functools.partial(_inproj_kernel, seg=seg, d_a=d_a, dk_tot=dk_tot, dv_tot=dv_tot)
    return pl.pallas_call(
        kern,
        out_shape=[jax.ShapeDtypeStruct((bsz, t, w), dt) for w, dt in zip(widths, dtypes)],
        grid=(bsz, t // tm),
        in_specs=in_specs,
        out_specs=[tok(w) for w in widths],
        scratch_shapes=([pltpu.VMEM((tm, d_a), BF16)]
                        + [pltpu.VMEM(w.shape, BF16) for w in weights]
                        + [pltpu.VMEM((2, WEIGHT_STAGE_ROWS, d), F32),
                           pltpu.SemaphoreType.DMA((2, WEIGHT_DMA_PARTS))]),
        compiler_params=pltpu.CompilerParams(dimension_semantics=("arbitrary", "arbitrary"),
                                             vmem_limit_bytes=VMEM_LIMIT),
        name="inproj",
    )(x4, rowemb, colemb, sh, sc, *consts, *weights)


def _scan_masks(c):
    row = lax.broadcasted_iota(jnp.int32, (c, c), 0)
    col = lax.broadcasted_iota(jnp.int32, (c, c), 1)
    return ((row >= col, c - 1), (row <= col, 0))


def _chunk_cumsum(lg, keep, end):
    tri = jnp.where(keep, 1.0, 0.0).astype(BF16)
    hi, lo = _split_bf16(lg)
    b = _dot(jnp.concatenate([tri, tri], axis=1), jnp.concatenate([hi, lo], axis=0))
    return b, b[end:end + 1, :]


def _decay_rows(st, tot):
    dk, dv = st.shape
    dcol = jnp.transpose(jnp.broadcast_to(jnp.exp(tot), (dk, dk)))
    return st * jnp.concatenate([dcol] * (dv // dk), axis=1)


def _state_kernel(kf_ref, vf_ref, lgf_ref, kb_ref, vb_ref, lgb_ref, s_ref, st_scr, *, dk, dv):
    t = pl.program_id(0)
    bsz, c = kf_ref.shape[0], SCAN_CHUNK
    n_sub = kf_ref.shape[1] // c

    @pl.when(t == 0)
    def _():
        st_scr[...] = jnp.zeros_like(st_scr)

    masks = _scan_masks(c)
    dirs = ((kf_ref, vf_ref, lgf_ref), (kb_ref, vb_ref, lgb_ref))
    for sub in range(n_sub):
        for bi in range(bsz):
            for di, (k_ref, v_ref, lg_ref) in enumerate(dirs):
                first = (sub if di == 0 else n_sub - 1 - sub) * c
                b, tot = _chunk_cumsum(lg_ref[bi, first:first + c, :], *masks[di])
                for h in range(GLA_HEADS):
                    sl = slice(h * dk, (h + 1) * dk)
                    k = k_ref[bi, first:first + c, sl].astype(F32)
                    kd = (k * jnp.exp(tot[:, sl] - b[:, sl])).astype(BF16)
                    v = v_ref[bi, first:first + c, h * dv:(h + 1) * dv]
                    st_scr[bi, di, h] = _decay_rows(st_scr[bi, di, h], tot[:, sl]) + _dot_tn(kd, v)

    @pl.when(t == pl.num_programs(0) - 1)
    def _():
        s_ref[...] = st_scr[...]


def _state_call(k, v, lgf, lgb):
    bsz, t, dk_tot = k.shape
    dv_tot = v.shape[-1]
    dk, dv = dk_tot // GLA_HEADS, dv_tot // GLA_HEADS
    tile = min(SCAN_TILE, t)
    nt = t // tile
    assert t % tile == 0 and tile % SCAN_CHUNK == 0 and dv % dk == 0
    fwd = lambda w: pl.BlockSpec((bsz, tile, w), lambda i: (0, i, 0))
    bwd = lambda w: pl.BlockSpec((bsz, tile, w), lambda i: (0, nt - 1 - i, 0))
    shape = (bsz, 2, GLA_HEADS, dk, dv)
    return pl.pallas_call(
        functools.partial(_state_kernel, dk=dk, dv=dv),
        out_shape=jax.ShapeDtypeStruct(shape, F32),
        grid=(nt,),
        in_specs=[fwd(dk_tot), fwd(dv_tot), fwd(dk_tot), bwd(dk_tot), bwd(dv_tot), bwd(dk_tot)],
        out_specs=pl.BlockSpec(shape, lambda i: (0,) * len(shape)),
        scratch_shapes=[pltpu.VMEM(shape, F32)],
        compiler_params=pltpu.CompilerParams(dimension_semantics=("arbitrary",),
                                             vmem_limit_bytes=VMEM_LIMIT),
        name="gla_ctx_state",
    )(k, v, lgf, k, v, lgb)


def _scan_kernel(qf_ref, kf_ref, vf_ref, lgf_ref, qb_ref, kb_ref, vb_ref, lgb_ref, s0_ref,
                 of_ref, ob_ref, st_scr, q32_scr, k32_scr, v32_scr, b_scr, o_scr,
                 *, dk, dv):
    t = pl.program_id(0)
    bsz, c = qf_ref.shape[0], SCAN_CHUNK
    n_sub = qf_ref.shape[1] // c

    def rows(di, sub):
        first = (sub if di == 0 else n_sub - 1 - sub) * c
        return slice(first, first + c)

    @pl.when(t == 0)
    def _():
        st_scr[...] = s0_ref[...]

    masks = _scan_masks(c)
    dirs = ((qf_ref, kf_ref, vf_ref, lgf_ref, of_ref) + masks[0],
            (qb_ref, kb_ref, vb_ref, lgb_ref, ob_ref) + masks[1])

    worst = None
    for lg_ref in (lgf_ref, lgb_ref):
        for sub in range(n_sub):
            w = jnp.max(-jnp.sum(lg_ref[:, sub * c:(sub + 1) * c, :], axis=1))
            worst = w if worst is None else jnp.maximum(worst, w)

    cums = {}

    def cumulate(bi, di, sub):
        lg_ref, keep, end = dirs[di][3], dirs[di][5], dirs[di][6]
        cums[bi, di, sub] = _chunk_cumsum(lg_ref[bi, rows(di, sub), :], keep, end)

    def load(bi, di, h, sub):
        q_ref, k_ref, v_ref = dirs[di][:3]
        b, tot = cums[bi, di, sub]
        sl = slice(h * dk, (h + 1) * dk)
        return (q_ref[bi, rows(di, sub), sl].astype(F32), k_ref[bi, rows(di, sub), sl].astype(F32),
                v_ref[bi, rows(di, sub), h * dv:(h + 1) * dv], b[:, sl], tot[:, sl])

    def decayed_state(bi, di, h, tot):
        return _decay_rows(st_scr[bi, di, h], tot)

    def fast_step():
        chains = [(bi, di, h, sub) for sub in range(n_sub) for bi in range(bsz)
                  for di in range(2) for h in range(GLA_HEADS)]
        live = {}

        def factors(n):
            q, k, v, bh, tot = load(*chains[n])
            half = 0.5 * tot
            eh = jnp.exp(half)
            qe = q * jnp.exp(bh - half)
            ke = k * jnp.exp(half - bh)
            live[n] = dict(v=v, tot=tot, qe=qe.astype(BF16), ke=ke.astype(BF16),
                           qd=(qe * eh).astype(BF16), kd=(ke * eh).astype(BF16))

        def scores(n):
            live[n]["a"] = _dot_nt(live[n].pop("qe"), live[n].pop("ke"))

        def outputs(n):
            bi, di, h, _ = chains[n]
            c_ = live[n]
            a = jnp.where(dirs[di][5], c_.pop("a"), 0.0).astype(BF16)
            lhs = jnp.concatenate([a, c_.pop("qd")], axis=1)
            rhs = jnp.concatenate([c_["v"], st_scr[bi, di, h].astype(BF16)], axis=0)
            c_["o"] = _dot(lhs, rhs)
            c_["u"] = _dot_tn(c_.pop("kd"), c_.pop("v"))

        def finish(n):
            bi, di, h, sub = chains[n]
            c_ = live.pop(n)
            o_ref = dirs[di][4]
            o_ref[bi, rows(di, sub), h * dv:(h + 1) * dv] = c_["o"].astype(o_ref.dtype)
            st_scr[bi, di, h] = decayed_state(bi, di, h, c_["tot"]) + c_["u"]

        def cumulate_for(n):
            bi, di, _, sub = chains[n]
            cumulate(bi, di, sub)

        phases = ((factors, 0), (scores, 1), (outputs, 3), (finish, 5))
        cumulate_for(0)
        for i in range(len(chains) + phases[-1][1]):
            ahead = i + GLA_HEADS
            if i % GLA_HEADS == 0 and ahead < len(chains):
                cumulate_for(ahead)
            for phase, lag in phases:
                if 0 <= i - lag < len(chains):
                    phase(i - lag)

    def safe_step():
        ridx = lax.broadcasted_iota(jnp.int32, (c, 1), 0)
        groups = [(s, b, d) for s in range(n_sub) for b in range(bsz) for d in range(2)]
        for sub, bi, di in groups:
            o_ref = dirs[di][4]
            cumulate(bi, di, sub)
            for h in range(GLA_HEADS):
                q, k, v, bh, tot = load(bi, di, h, sub)
                q32_scr[...] = q
                k32_scr[...] = k
                v32_scr[...] = v.astype(F32)
                b_scr[...] = bh
                o_scr[...] = _dot((q * jnp.exp(bh)).astype(BF16), st_scr[bi, di, h].astype(BF16))
                st_scr[bi, di, h] = (decayed_state(bi, di, h, tot)
                                     + _dot_tn((k * jnp.exp(tot - bh)).astype(BF16), v))

                def body(j, carry):
                    bj = b_scr[pl.ds(j, 1), :]
                    valid = (ridx >= j) if di == 0 else (ridx <= j)
                    e = jnp.exp(jnp.where(valid, b_scr[...] - bj, -jnp.inf))
                    w = jnp.sum(q32_scr[...] * e * k32_scr[pl.ds(j, 1), :], axis=-1, keepdims=True)
                    o_scr[...] += w * v32_scr[pl.ds(j, 1), :]
                    return carry

                lax.fori_loop(0, c, body, 0)
                o_ref[bi, rows(di, sub), h * dv:(h + 1) * dv] = o_scr[...].astype(o_ref.dtype)

    lax.cond(worst <= SAFE_DECAY, fast_step, safe_step)


def _scan_call(q, k, v, lgf, lgb, s0):
    bsz, t, dk_tot = q.shape
    dv_tot = v.shape[-1]
    dk, dv = dk_tot // GLA_HEADS, dv_tot // GLA_HEADS
    c = SCAN_CHUNK
    tile = SCAN_TILE
    nt = t // tile
    assert t % tile == 0 and tile % c == 0 and dv % dk == 0
    fwd = lambda w: pl.BlockSpec((bsz, tile, w), lambda i: (0, i, 0))
    bwd = lambda w: pl.BlockSpec((bsz, tile, w), lambda i: (0, nt - 1 - i, 0))
    st_spec = pl.BlockSpec(s0.shape, lambda i: (0,) * s0.ndim)
    kern = functools.partial(_scan_kernel, dk=dk, dv=dv)
    return pl.pallas_call(
        kern,
        out_shape=[jax.ShapeDtypeStruct((bsz, t, dv_tot), BF16),
                   jax.ShapeDtypeStruct((bsz, t, dv_tot), BF16)],
        grid=(nt,),
        in_specs=[fwd(dk_tot), fwd(dk_tot), fwd(dv_tot), fwd(dk_tot),
                  bwd(dk_tot), bwd(dk_tot), bwd(dv_tot), bwd(dk_tot), st_spec],
        out_specs=[fwd(dv_tot), bwd(dv_tot)],
        scratch_shapes=[pltpu.VMEM(s0.shape, F32),
                        pltpu.VMEM((c, dk), F32), pltpu.VMEM((c, dk), F32),
                        pltpu.VMEM((c, dv), F32), pltpu.VMEM((c, dk), F32),
                        pltpu.VMEM((c, dv), F32)],
        compiler_params=pltpu.CompilerParams(dimension_semantics=("arbitrary",),
                                             vmem_limit_bytes=VMEM_LIMIT),
        name="gla_scan",
    )(q, k, v, lgf, q, k, v, lgb, s0)


def _tail_kernel(x_ref, row_ref, col_ref, ma_ref, gb_ref, of_ref, ob_ref, sr_ref,
                 g1_ref, sh2_ref, sc2_ref, g2_ref, og_ref, n2_ref, fg_ref,
                 pb_hbm, wo_hbm, w1_hbm, w2_hbm, out_ref,
                 pb_ref, wo_ref, w1_ref, w2_ref, stage_ref, sem_ref, *, dv):
    @pl.when((pl.program_id(0) == 0) & (pl.program_id(1) == 0))
    def _():
        for src, dst in ((pb_hbm, pb_ref), (wo_hbm, wo_ref), (w1_hbm, w1_ref), (w2_hbm, w2_ref)):
            _load_as_bf16(src, dst, stage_ref, sem_ref, _stage_rows(src.shape[0]))

    xp = _load_tokens(x_ref, row_ref, col_ref)
    o = of_ref[0].astype(F32) + ob_ref[0].astype(F32)
    heads = []
    for h in range(GLA_HEADS):
        sl = slice(h * dv, (h + 1) * dv)
        heads.append(_rms(o[:, sl], og_ref[:, sl]))
    yb = (jnp.concatenate(heads, axis=-1) * sr_ref[0].astype(F32)).astype(BF16)
    merged = ma_ref[0].astype(F32) + gb_ref[0].astype(F32) * _dot(yb, pb_ref[...])
    x1 = xp + g1_ref[0] * _dot(merged.astype(BF16), wo_ref[...])
    h2 = (_rms(x1, n2_ref[...]) * (1.0 + sc2_ref[0]) + sh2_ref[0]).astype(BF16)
    acc = jnp.zeros_like(x1)
    d_ff = w1_ref.shape[1]
    for j in range(d_ff // FF_COL_TILE):
        sl = slice(j * FF_COL_TILE, (j + 1) * FF_COL_TILE)
        f = jnp.maximum(_dot(h2, w1_ref[:, sl]), 0.0)
        acc = acc + _dot((f * f).astype(BF16), w2_ref[sl, :])
    x2 = x1 + g2_ref[0] * acc
    out_ref[0] = _rms(x2, fg_ref[...])


def _tail_call(x4, pos, acts, mods, consts, weights):
    bsz, nrow, gw_, d = x4.shape
    t = nrow * gw_
    tm = TOKEN_TILE
    rows_per_tile = tm // GRID_W
    rowemb, colemb = pos
    dv = acts[2].shape[-1] // GLA_HEADS
    tok = lambda w: pl.BlockSpec((1, tm, w), lambda b, i: (b, i, 0))
    per_batch = pl.BlockSpec((1, 1, d), lambda b, i: (b, 0, 0))
    in_specs = ([pl.BlockSpec((1, rows_per_tile, GRID_W, d), lambda b, i: (b, i, 0, 0)),
                 pl.BlockSpec((rows_per_tile, 1, d // 2), lambda b, i: (i, 0, 0)),
                 _const_spec(colemb.shape)]
                + [tok(a.shape[-1]) for a in acts]
                + [per_batch] * len(mods)
                + [_const_spec(a.shape) for a in consts]
                + [pl.BlockSpec(memory_space=pl.ANY)] * len(weights))
    return pl.pallas_call(
        functools.partial(_tail_kernel, dv=dv),
        out_shape=jax.ShapeDtypeStruct((bsz, t, d), F32),
        grid=(bsz, t // tm),
        in_specs=in_specs,
        out_specs=tok(d),
        scratch_shapes=([pltpu.VMEM(w.shape, BF16) for w in weights]
                        + [pltpu.VMEM((2, WEIGHT_STAGE_ROWS, d), F32),
                           pltpu.SemaphoreType.DMA((2, WEIGHT_DMA_PARTS))]),
        compiler_params=pltpu.CompilerParams(dimension_semantics=("arbitrary", "arbitrary"),
                                             vmem_limit_bytes=VMEM_LIMIT),
        name="tail",
    )(x4, rowemb, colemb, *acts, *mods, *consts, *weights)


def _pos_tables(t, d):
    quarter = d // 4
    freqs = 1.0 / (10000.0 ** (jnp.arange(quarter, dtype=F32) / quarter))

    def emb(n):
        a = jnp.arange(n, dtype=F32)[:, None] * freqs[None, :]
        return jnp.concatenate([jnp.sin(a), jnp.cos(a)], -1)

    return emb(t // GRID_W)[:, None, :], emb(GRID_W)[None, :, :]


def kernel(x, c, ctx, c_ctx, w_ada, b_ada, norm1_g, w_in, lnv_g, lnv_b, w_s, b_s, w_dec2, b_dec,
           onorm_g, proj_a, proj_b, w_out, norm2_g, w_ff1, w_ff2, final_g):
    bsz, t, d = x.shape
    depth = w_in.shape[0]
    assert depth == 1, "single-layer problem: the context stream only feeds the scan states"
    d_a = proj_a.shape[1]
    dv_tot = proj_b.shape[1]
    dk_tot = w_dec2.shape[-1]
    seg = _segment_rows(d, d_a, dk_tot, dv_tot, w_dec2.shape[2])

    c_rows = jnp.zeros((MOD_ROWS, d), F32).at[:bsz].set(c).at[bsz].set(c_ctx)
    mod = _mod_call(c_rows, w_ada[0], b_ada[0])
    sh1, sc1, g1, sh2, sc2, g2 = jnp.split(mod, 6, axis=-1)
    latent = lambda m: m[:bsz, None, :]
    context = lambda m: m[bsz:bsz + 1]

    row = lambda a: a.reshape(1, -1)
    g1n = row(norm1_g[0])
    w_in_t = jnp.swapaxes(w_in[0], 0, 1)
    wd2, bd = w_dec2[0], b_dec[0][:, None, :]

    ck, cv, clgf, clgb = _ctx_proj_call(ctx, context(sh1), context(sc1), g1n, w_in_t, wd2, bd,
                                        seg, dk_tot, dv_tot)
    s_ctx = _state_call(ck, cv, clgf, clgb)

    pos = _pos_tables(t, d)
    x4 = x.reshape(bsz, t // GRID_W, GRID_W, d)
    consts = (g1n, wd2, bd, row(lnv_g[0]), row(lnv_b[0]), w_s[0], b_s[0][:, :, None])
    ma, gb, q, k, v, sr, lgf, lgb = _inproj_call(x4, pos, latent(sh1), latent(sc1), consts,
                                                 (w_in_t, proj_a[0]), seg, d_a, dk_tot, dv_tot)
    o_f, o_b = _scan_call(q, k, v, lgf, lgb, s_ctx)

    mods = (latent(g1), latent(sh2), latent(sc2), latent(g2))
    return _tail_call(x4, pos, (ma, gb, o_f, o_b, sr), mods,
                      (row(onorm_g[0]), row(norm2_g[0]), row(final_g)),
                      (proj_b[0], w_out[0], w_ff1[0], w_ff2[0]))
```

```python
import functools
import math

import jax
import jax.numpy as jnp
from jax import lax
from jax.experimental import pallas as pl
from jax.experimental.pallas import tpu as pltpu

F32 = jnp.float32
BF16 = jnp.bfloat16

GRID_W = 64
GLA_HEADS = 4
GATE_TAU = 16.0
EPS = 1e-6

TOKEN_TILE = 512
SCAN_CHUNK = 128
SCAN_TILE = 256
MOD_ROWS = 8
MOD_COL_TILE = 1024
PROJ_COL_TILE = 512
WEIGHT_STAGE_ROWS = 512
WEIGHT_STAGE_SLOTS = 4
BF16_SUBLANES = 16
FF_COL_TILE = 1024
SAFE_DECAY = 60.0
VMEM_LIMIT = 56 * 1024 * 1024


def _sigmoid(x):
    return 1.0 / (1.0 + jnp.exp(-x))


def _gelu_tanh(x):
    c = math.sqrt(2.0 / math.pi)
    return 0.5 * x * (1.0 + jnp.tanh(c * (x + 0.044715 * (x * x * x))))


def _log_sigmoid(z):
    return jnp.minimum(z, 0.0) - jnp.log(1.0 + jnp.exp(-jnp.abs(z)))


def _rms(x, g):
    return x * lax.rsqrt(jnp.mean(x * x, axis=-1, keepdims=True) + EPS) * g


def _split_bf16(a):
    hi = a.astype(BF16)
    lo = (a - hi.astype(F32)).astype(BF16)
    return hi, lo


def _dot(a, b):
    return jnp.dot(a, b, preferred_element_type=F32)


def _dot_nt(a, b):
    return lax.dot_general(a, b, (((1,), (1,)), ((), ())), preferred_element_type=F32)


def _dot_tn(a, b):
    return lax.dot_general(a, b, (((0,), (0,)), ((), ())), preferred_element_type=F32)


def _const_spec(shape):
    nd = len(shape)
    return pl.BlockSpec(shape, lambda *_: (0,) * nd, pipeline_mode=pl.Buffered(1))


def _load_as_bf16(src_ref, dst_ref, stage_ref, sem_ref, rows):
    n_rows, cols = src_ref.shape
    slots = stage_ref.shape[0]
    assert n_rows % rows == 0 and rows <= stage_ref.shape[1] and cols == stage_ref.shape[2]
    n = n_rows // rows

    def copy(j, slot):
        return pltpu.make_async_copy(src_ref.at[pl.ds(j * rows, rows), :],
                                     stage_ref.at[slot, pl.ds(0, rows), :], sem_ref.at[slot])

    for j in range(min(slots - 1, n)):
        copy(j, j).start()

    def body(j, carry):
        ahead = j + slots - 1

        @pl.when(ahead < n)
        def _():
            copy(ahead, ahead % slots).start()

        copy(j, j % slots).wait()
        first = pl.multiple_of(j * rows, rows)
        dst_ref[pl.ds(first, rows), :] = stage_ref[j % slots, pl.ds(0, rows), :].astype(BF16)
        return carry

    lax.fori_loop(0, n, body, 0)


def _stage_rows(n_rows):
    return max(r for r in range(BF16_SUBLANES, WEIGHT_STAGE_ROWS + 1, BF16_SUBLANES)
               if n_rows % r == 0)


def _mod_kernel(c_ref, w_ref, b_ref, o_ref):
    c = c_ref[...]
    a_hi, a_lo = _split_bf16(c * _sigmoid(c))
    w_hi, w_lo = _split_bf16(w_ref[...])
    rows = c.shape[0]
    both = _dot(jnp.concatenate([a_hi, a_lo], axis=0), w_hi)
    o_ref[...] = both[:rows] + both[rows:] + _dot(a_hi, w_lo) + b_ref[...]


def _mod_call(c_rows, w_ada, b_ada):
    d, n = w_ada.shape
    return pl.pallas_call(
        _mod_kernel,
        out_shape=jax.ShapeDtypeStruct((MOD_ROWS, n), F32),
        grid=(n // MOD_COL_TILE,),
        in_specs=[pl.BlockSpec((MOD_ROWS, d), lambda j: (0, 0)),
                  pl.BlockSpec((d, MOD_COL_TILE), lambda j: (0, j)),
                  pl.BlockSpec((1, MOD_COL_TILE), lambda j: (0, j))],
        out_specs=pl.BlockSpec((MOD_ROWS, MOD_COL_TILE), lambda j: (0, j)),
        compiler_params=pltpu.CompilerParams(dimension_semantics=("arbitrary",)),
        name="adaln_mod",
    )(c_rows, w_ada, b_ada.reshape(1, n))


def _load_tokens(x_ref, row_ref, col_ref):
    x = x_ref[0]
    half = x.shape[-1] // 2
    x = jnp.concatenate([x[..., :half] + row_ref[...], x[..., half:] + col_ref[...]], axis=-1)
    return x.reshape(x.shape[0] * x.shape[1], x.shape[2])


def _segment_rows(d, d_a, dk_tot, dv_tot, rank):
    names = ("u", "va", "q", "k", "v", "r", "dec", "ga", "gb")
    widths = (d_a, d_a, dk_tot, dk_tot, dv_tot, dv_tot, 2 * rank, d, d)
    offsets, r0 = {}, 0
    for name, w in zip(names, widths):
        offsets[name] = r0
        r0 += w
    return offsets


def _store_log_decays(dec, wdec2_ref, bdec_ref, out_refs):
    rank = wdec2_ref.shape[1]
    for di, out_ref in enumerate(out_refs):
        z = _dot(dec[:, di * rank:(di + 1) * rank].astype(BF16),
                 wdec2_ref[di].astype(BF16)) + bdec_ref[di]
        out_ref[0] = _log_sigmoid(z) * (1.0 / GATE_TAU)


def _ctx_proj_kernel(x_ref, sh_ref, sc_ref, g1_ref, wk_ref, wv_ref, wd_ref, wdec2_ref, bdec_ref,
                     k_ref, v_ref, lgf_ref, lgb_ref):
    hb = (_rms(x_ref[0], g1_ref[...]) * (1.0 + sc_ref[...]) + sh_ref[...]).astype(BF16)
    proj = lambda w_ref: _dot_nt(hb, w_ref[...].astype(BF16))
    _store_log_decays(proj(wd_ref), wdec2_ref, bdec_ref, (lgf_ref, lgb_ref))
    k_ref[0] = proj(wk_ref).astype(BF16)
    v_ref[0] = proj(wv_ref).astype(BF16)


def _ctx_proj_call(ctx, sh, sc, g1, w_in_t, w_dec2, b_dec, seg, dk_tot, dv_tot):
    bsz, t, d = ctx.shape
    tm = min(TOKEN_TILE, t)
    assert t % tm == 0
    tok = lambda w: pl.BlockSpec((1, tm, w), lambda b, i: (b, i, 0))

    def rows_spec(r0, width):
        assert r0 % width == 0
        return pl.BlockSpec((width, d), lambda b, i: (r0 // width, 0),
                            pipeline_mode=pl.Buffered(1))

    rank = w_dec2.shape[1]
    consts = [sh, sc, g1]
    tail_consts = [w_dec2, b_dec]
    widths = [dk_tot, dv_tot, dk_tot, dk_tot]
    dtypes = [BF16, BF16, F32, F32]
    return pl.pallas_call(
        _ctx_proj_kernel,
        out_shape=[jax.ShapeDtypeStruct((bsz, t, w), dt) for w, dt in zip(widths, dtypes)],
        grid=(bsz, t // tm),
        in_specs=([tok(d)] + [_const_spec(a.shape) for a in consts]
                  + [rows_spec(seg["k"], dk_tot), rows_spec(seg["v"], dv_tot),
                     rows_spec(seg["dec"], 2 * rank)]
                  + [_const_spec(a.shape) for a in tail_consts]),
        out_specs=[tok(w) for w in widths],
        compiler_params=pltpu.CompilerParams(dimension_semantics=("arbitrary", "arbitrary"),
                                             vmem_limit_bytes=VMEM_LIMIT),
        name="ctx_proj",
    )(ctx, *consts, w_in_t, w_in_t, w_in_t, *tail_consts)


def _inproj_kernel(x_ref, row_ref, col_ref, sh_ref, sc_ref, g1_ref, wdec2_ref, bdec_ref,
                   lng_ref, lnb_ref, ws_ref, bs_ref, wt_hbm, pa_hbm, *rest,
                   seg, d_a, dk_tot, dv_tot, n_later):
    later_f32, rest = rest[:n_later], rest[n_later:]
    (ma_ref, gb_ref, q_ref, k_ref, v_ref, sr_ref, lgf_ref, lgb_ref), rest = rest[:8], rest[8:]
    later_bf16, (ya_scr, wt_ref, pa_ref, stage_ref, sem_ref) = rest[:n_later], rest[n_later:]

    @pl.when((pl.program_id(0) == 0) & (pl.program_id(1) == 0))
    def _():
        for src, dst in ((wt_hbm, wt_ref), (pa_hbm, pa_ref)):
            _load_as_bf16(src, dst, stage_ref, sem_ref, _stage_rows(src.shape[0]))

    for src, dst in zip(later_f32, later_bf16):
        dst[...] = src[...].astype(BF16)

    xp = _load_tokens(x_ref, row_ref, col_ref)
    tm = xp.shape[0]
    d = xp.shape[1]
    hb = (_rms(xp, g1_ref[...]) * (1.0 + sc_ref[0]) + sh_ref[0]).astype(BF16)

    def proj(r0, width):
        return _dot_nt(hb, wt_ref[r0:r0 + width, :])

    r_u, r_va, r_q, r_k, r_v, r_r = (seg[n] for n in ("u", "va", "q", "k", "v", "r"))
    r_ga, r_gb = seg["ga"], seg["gb"]
    rank = wdec2_ref.shape[1]
    groups, chunk_a = ws_ref.shape[0], ws_ref.shape[1]
    dk = dk_tot // GLA_HEADS
    gw = d_a // groups
    blk = PROJ_COL_TILE
    cols = lambda j: slice(j * blk, (j + 1) * blk)
    gu, gv, vn = {}, {}, {}

    stage = {}

    def store(out_ref, j, fn):
        def ep(y):
            out_ref[0, :, cols(j)] = fn(y).astype(out_ref.dtype)
        return ep

    def keep(table, j, fn):
        def ep(y):
            table[j] = fn(y)
        return ep

    def add(name, matmul, epilogue):
        stage[name] = (matmul, epilogue)
        return name

    s_dec = add("dec", lambda: proj(seg["dec"], 2 * rank),
                lambda dec: _store_log_decays(dec, wdec2_ref, bdec_ref, (lgf_ref, lgb_ref)))
    s_u = [add(f"u{j}", lambda j=j: proj(r_u + j * blk, blk), keep(gu, j, _gelu_tanh))
           for j in range(d_a // blk)]
    s_va = [add(f"va{j}", lambda j=j: proj(r_va + j * blk, blk), keep(gv, j, _gelu_tanh))
            for j in range(d_a // blk)]

    def ep_q_and_layernorm(y):
        q_ref[0] = (y * dk ** -0.5).astype(BF16)
        nb = d_a // blk
        mu = sum(jnp.sum(gv[j], axis=-1, keepdims=True) for j in range(nb)) * (1.0 / d_a)
        cen = [gv[j] - mu for j in range(nb)]
        var = sum(jnp.sum(c * c, axis=-1, keepdims=True) for c in cen) * (1.0 / d_a)
        inv = lax.rsqrt(var + EPS)
        for j in range(nb):
            vn[j] = (cen[j] * inv * lng_ref[:, cols(j)] + lnb_ref[:, cols(j)]).astype(BF16)
    s_q = add("q", lambda: proj(r_q, dk_tot), ep_q_and_layernorm)
    s_k = add("k", lambda: proj(r_k, dk_tot), store(k_ref, 0, lambda y: y))
    s_r = [add(f"r{j}", lambda j=j: proj(r_r + j * blk, blk),
               store(sr_ref, j, lambda y: y * _sigmoid(y))) for j in range(dv_tot // blk)]
    s_v = [add(f"v{j}", lambda j=j: proj(r_v + j * blk, blk), store(v_ref, j, lambda y: y))
           for j in range(dv_tot // blk)]

    def mm_gate(ci, g):
        r0, c0 = ci * chunk_a, g * gw
        return _dot(ws_ref[g].astype(BF16),
                    vn[c0 // blk][r0:r0 + chunk_a, c0 % blk:c0 % blk + gw])

    def ep_gate(ci, g):
        r0, c0 = ci * chunk_a, g * gw

        def ep(s):
            u = gu[c0 // blk][r0:r0 + chunk_a, c0 % blk:c0 % blk + gw]
            ya_scr[r0:r0 + chunk_a, c0:c0 + gw] = (u * (s + bs_ref[g])).astype(BF16)
        return ep
    s_mix = [add(f"mix{ci}_{g}", lambda ci=ci, g=g: mm_gate(ci, g), ep_gate(ci, g))
             for ci in range(tm // chunk_a) for g in range(groups)]
    s_gb = [add(f"gb{j}", lambda j=j: proj(r_gb + j * blk, blk), store(gb_ref, j, _sigmoid))
            for j in range(d // blk)]
    s_ga = [add(f"ga{j}", lambda j=j: (proj(r_ga + j * blk, blk),
                                       _dot(ya_scr[...], pa_ref[:, cols(j)])),
                store(ma_ref, j, lambda y: _sigmoid(y[0]) * y[1])) for j in range(d // blk)]

    order = [s_dec] + s_u + s_va + [s_q, s_k] + s_r + s_mix + s_gb + s_ga + s_v
    assert sorted(order) == sorted(stage)
    pending = stage[order[0]][0]()
    for i, name in enumerate(order):
        nxt = stage[order[i + 1]][0]() if i + 1 < len(order) else None
        stage[name][1](pending)
        pending = nxt


def _inproj_call(x4, pos, sh, sc, consts, weights, later_weights, seg, d_a, dk_tot, dv_tot):
    bsz, nrow, gw_, d = x4.shape
    t = nrow * gw_
    tm = TOKEN_TILE
    nt = t // tm
    rows_per_tile = tm // GRID_W
    rowemb, colemb = pos
    chunk_a = consts[5].shape[1]
    assert tm % chunk_a == 0 and tm % GRID_W == 0 and t % tm == 0

    def slab(w):
        rows = w.shape[0] // (bsz * nt)
        assert w.shape[0] % (bsz * nt) == 0 and rows % BF16_SUBLANES == 0
        return pl.BlockSpec((rows, w.shape[1]), lambda b, i: (b * nt + i, 0))

    tok = lambda w: pl.BlockSpec((1, tm, w), lambda b, i: (b, i, 0))
    per_batch = pl.BlockSpec((1, 1, d), lambda b, i: (b, 0, 0))
    in_specs = ([pl.BlockSpec((1, rows_per_tile, GRID_W, d), lambda b, i: (b, i, 0, 0)),
                 pl.BlockSpec((rows_per_tile, 1, d // 2), lambda b, i: (i, 0, 0)),
                 _const_spec(colemb.shape), per_batch, per_batch]
                + [_const_spec(a.shape) for a in consts]
                + [pl.BlockSpec(memory_space=pl.ANY)] * len(weights)
                + [slab(w) for w in later_weights])
    widths = [d, d, dk_tot, dk_tot, dv_tot, dv_tot, dk_tot, dk_tot]
    dtypes = [BF16] * 6 + [F32] * 2
    kern = functools.partial(_inproj_kernel, seg=seg, d_a=d_a, dk_tot=dk_tot, dv_tot=dv_tot,
                             n_later=len(later_weights))
    return pl.pallas_call(
        kern,
        out_shape=([jax.ShapeDtypeStruct((bsz, t, w), dt) for w, dt in zip(widths, dtypes)]
                   + [jax.ShapeDtypeStruct(w.shape, BF16) for w in later_weights]),
        grid=(bsz, nt),
        in_specs=in_specs,
        out_specs=[tok(w) for w in widths] + [slab(w) for w in later_weights],
        scratch_shapes=([pltpu.VMEM((tm, d_a), BF16)]
                        + [pltpu.VMEM(w.shape, BF16) for w in weights]
                        + [pltpu.VMEM((WEIGHT_STAGE_SLOTS, WEIGHT_STAGE_ROWS, d), F32),
                           pltpu.SemaphoreType.DMA((WEIGHT_STAGE_SLOTS,))]),
        compiler_params=pltpu.CompilerParams(dimension_semantics=("arbitrary", "arbitrary"),
                                             vmem_limit_bytes=VMEM_LIMIT),
        name="inproj",
    )(x4, rowemb, colemb, sh, sc, *consts, *weights, *later_weights)


def _scan_masks(c):
    row = lax.broadcasted_iota(jnp.int32, (c, c), 0)
    col = lax.broadcasted_iota(jnp.int32, (c, c), 1)
    return ((row >= col, c - 1), (row <= col, 0))


def _chunk_cumsum(lg, keep, end):
    tri = jnp.where(keep, 1.0, 0.0).astype(BF16)
    hi, lo = _split_bf16(lg)
    b = _dot(jnp.concatenate([tri, tri], axis=1), jnp.concatenate([hi, lo], axis=0))
    return b, b[end:end + 1, :]


def _decay_rows(st, tot):
    dk, dv = st.shape
    dcol = jnp.transpose(jnp.broadcast_to(jnp.exp(tot), (dk, dk)))
    return st * jnp.concatenate([dcol] * (dv // dk), axis=1)


def _state_kernel(kf_ref, vf_ref, lgf_ref, kb_ref, vb_ref, lgb_ref, s_ref, st_scr, *, dk, dv):
    t = pl.program_id(0)
    bsz, c = kf_ref.shape[0], SCAN_CHUNK
    n_sub = kf_ref.shape[1] // c

    @pl.when(t == 0)
    def _():
        st_scr[...] = jnp.zeros_like(st_scr)

    masks = _scan_masks(c)
    dirs = ((kf_ref, vf_ref, lgf_ref), (kb_ref, vb_ref, lgb_ref))
    for sub in range(n_sub):
        for bi in range(bsz):
            for di, (k_ref, v_ref, lg_ref) in enumerate(dirs):
                first = (sub if di == 0 else n_sub - 1 - sub) * c
                b, tot = _chunk_cumsum(lg_ref[bi, first:first + c, :], *masks[di])
                for h in range(GLA_HEADS):
                    sl = slice(h * dk, (h + 1) * dk)
                    k = k_ref[bi, first:first + c, sl].astype(F32)
                    kd = (k * jnp.exp(tot[:, sl] - b[:, sl])).astype(BF16)
                    v = v_ref[bi, first:first + c, h * dv:(h + 1) * dv]
                    st_scr[bi, di, h] = _decay_rows(st_scr[bi, di, h], tot[:, sl]) + _dot_tn(kd, v)

    @pl.when(t == pl.num_programs(0) - 1)
    def _():
        s_ref[...] = st_scr[...]


def _state_call(k, v, lgf, lgb):
    bsz, t, dk_tot = k.shape
    dv_tot = v.shape[-1]
    dk, dv = dk_tot // GLA_HEADS, dv_tot // GLA_HEADS
    tile = min(SCAN_TILE, t)
    nt = t // tile
    assert t % tile == 0 and tile % SCAN_CHUNK == 0 and dv % dk == 0
    fwd = lambda w: pl.BlockSpec((bsz, tile, w), lambda i: (0, i, 0))
    bwd = lambda w: pl.BlockSpec((bsz, tile, w), lambda i: (0, nt - 1 - i, 0))
    shape = (bsz, 2, GLA_HEADS, dk, dv)
    return pl.pallas_call(
        functools.partial(_state_kernel, dk=dk, dv=dv),
        out_shape=jax.ShapeDtypeStruct(shape, F32),
        grid=(nt,),
        in_specs=[fwd(dk_tot), fwd(dv_tot), fwd(dk_tot), bwd(dk_tot), bwd(dv_tot), bwd(dk_tot)],
        out_specs=pl.BlockSpec(shape, lambda i: (0,) * len(shape)),
        scratch_shapes=[pltpu.VMEM(shape, F32)],
        compiler_params=pltpu.CompilerParams(dimension_semantics=("arbitrary",),
                                             vmem_limit_bytes=VMEM_LIMIT),
        name="gla_ctx_state",
    )(k, v, lgf, k, v, lgb)


def _scan_kernel(qf_ref, kf_ref, vf_ref, lgf_ref, qb_ref, kb_ref, vb_ref, lgb_ref, s0_ref,
                 of_ref, ob_ref, st_scr, q32_scr, k32_scr, v32_scr, b_scr, o_scr,
                 *, dk, dv):
    t = pl.program_id(0)
    bsz, c = qf_ref.shape[0], SCAN_CHUNK
    n_sub = qf_ref.shape[1] // c

    def rows(di, sub):
        first = (sub if di == 0 else n_sub - 1 - sub) * c
        return slice(first, first + c)

    @pl.when(t == 0)
    def _():
        st_scr[...] = s0_ref[...]

    masks = _scan_masks(c)
    dirs = ((qf_ref, kf_ref, vf_ref, lgf_ref, of_ref) + masks[0],
            (qb_ref, kb_ref, vb_ref, lgb_ref, ob_ref) + masks[1])

    worst = None
    for lg_ref in (lgf_ref, lgb_ref):
        for sub in range(n_sub):
            w = jnp.max(-jnp.sum(lg_ref[:, sub * c:(sub + 1) * c, :], axis=1))
            worst = w if worst is None else jnp.maximum(worst, w)

    cums = {}

    def cumulate(bi, di, sub):
        lg_ref, keep, end = dirs[di][3], dirs[di][5], dirs[di][6]
        cums[bi, di, sub] = _chunk_cumsum(lg_ref[bi, rows(di, sub), :], keep, end)

    def load(bi, di, h, sub):
        q_ref, k_ref, v_ref = dirs[di][:3]
        b, tot = cums[bi, di, sub]
        sl = slice(h * dk, (h + 1) * dk)
        return (q_ref[bi, rows(di, sub), sl].astype(F32), k_ref[bi, rows(di, sub), sl].astype(F32),
                v_ref[bi, rows(di, sub), h * dv:(h + 1) * dv], b[:, sl], tot[:, sl])

    def decayed_state(bi, di, h, tot):
        return _decay_rows(st_scr[bi, di, h], tot)

    def fast_step():
        chains = [(bi, di, h, sub) for sub in range(n_sub) for bi in range(bsz)
                  for di in range(2) for h in range(GLA_HEADS)]
        live = {}

        def factors(n):
            q, k, v, bh, tot = load(*chains[n])
            half = 0.5 * tot
            eh = jnp.exp(half)
            qe = q * jnp.exp(bh - half)
            ke = k * jnp.exp(half - bh)
            live[n] = dict(v=v, tot=tot, qe=qe.astype(BF16), ke=ke.astype(BF16),
                           qd=(qe * eh).astype(BF16), kd=(ke * eh).astype(BF16))

        def scores(n):
            live[n]["a"] = _dot_nt(live[n].pop("qe"), live[n].pop("ke"))

        def outputs(n):
            bi, di, h, _ = chains[n]
            c_ = live[n]
            a = jnp.where(dirs[di][5], c_.pop("a"), 0.0).astype(BF16)
            lhs = jnp.concatenate([a, c_.pop("qd")], axis=1)
            rhs = jnp.concatenate([c_["v"], st_scr[bi, di, h].astype(BF16)], axis=0)
            c_["o"] = _dot(lhs, rhs)
            c_["u"] = _dot_tn(c_.pop("kd"), c_.pop("v"))

        def finish(n):
            bi, di, h, sub = chains[n]
            c_ = live.pop(n)
            o_ref = dirs[di][4]
            o_ref[bi, rows(di, sub), h * dv:(h + 1) * dv] = c_["o"].astype(o_ref.dtype)
            st_scr[bi, di, h] = decayed_state(bi, di, h, c_["tot"]) + c_["u"]

        def cumulate_for(n):
            bi, di, _, sub = chains[n]
            cumulate(bi, di, sub)

        phases = ((factors, 0), (scores, 1), (outputs, 3), (finish, 5))
        cumulate_for(0)
        for i in range(len(chains) + phases[-1][1]):
            ahead = i + GLA_HEADS
            if i % GLA_HEADS == 0 and ahead < len(chains):
                cumulate_for(ahead)
            for phase, lag in phases:
                if 0 <= i - lag < len(chains):
                    phase(i - lag)

    def safe_step():
        ridx = lax.broadcasted_iota(jnp.int32, (c, 1), 0)
        groups = [(s, b, d) for s in range(n_sub) for b in range(bsz) for d in range(2)]
        for sub, bi, di in groups:
            o_ref = dirs[di][4]
            cumulate(bi, di, sub)
            for h in range(GLA_HEADS):
                q, k, v, bh, tot = load(bi, di, h, sub)
                q32_scr[...] = q
                k32_scr[...] = k
                v32_scr[...] = v.astype(F32)
                b_scr[...] = bh
                o_scr[...] = _dot((q * jnp.exp(bh)).astype(BF16), st_scr[bi, di, h].astype(BF16))
                st_scr[bi, di, h] = (decayed_state(bi, di, h, tot)
                                     + _dot_tn((k * jnp.exp(tot - bh)).astype(BF16), v))

                def body(j, carry):
                    bj = b_scr[pl.ds(j, 1), :]
                    valid = (ridx >= j) if di == 0 else (ridx <= j)
                    e = jnp.exp(jnp.where(valid, b_scr[...] - bj, -jnp.inf))
                    w = jnp.sum(q32_scr[...] * e * k32_scr[pl.ds(j, 1), :], axis=-1, keepdims=True)
                    o_scr[...] += w * v32_scr[pl.ds(j, 1), :]
                    return carry

                lax.fori_loop(0, c, body, 0)
                o_ref[bi, rows(di, sub), h * dv:(h + 1) * dv] = o_scr[...].astype(o_ref.dtype)

    lax.cond(worst <= SAFE_DECAY, fast_step, safe_step)


def _scan_call(q, k, v, lgf, lgb, s0):
    bsz, t, dk_tot = q.shape
    dv_tot = v.shape[-1]
    dk, dv = dk_tot // GLA_HEADS, dv_tot // GLA_HEADS
    c = SCAN_CHUNK
    tile = SCAN_TILE
    nt = t // tile
    assert t % tile == 0 and tile % c == 0 and dv % dk == 0
    fwd = lambda w: pl.BlockSpec((bsz, tile, w), lambda i: (0, i, 0))
    bwd = lambda w: pl.BlockSpec((bsz, tile, w), lambda i: (0, nt - 1 - i, 0))
    st_spec = pl.BlockSpec(s0.shape, lambda i: (0,) * s0.ndim)
    kern = functools.partial(_scan_kernel, dk=dk, dv=dv)
    return pl.pallas_call(
        kern,
        out_shape=[jax.ShapeDtypeStruct((bsz, t, dv_tot), BF16),
                   jax.ShapeDtypeStruct((bsz, t, dv_tot), BF16)],
        grid=(nt,),
        in_specs=[fwd(dk_tot), fwd(dk_tot), fwd(dv_tot), fwd(dk_tot),
                  bwd(dk_tot), bwd(dk_tot), bwd(dv_tot), bwd(dk_tot), st_spec],
        out_specs=[fwd(dv_tot), bwd(dv_tot)],
        scratch_shapes=[pltpu.VMEM(s0.shape, F32),
                        pltpu.VMEM((c, dk), F32), pltpu.VMEM((c, dk), F32),
                        pltpu.VMEM((c, dv), F32), pltpu.VMEM((c, dk), F32),
                        pltpu.VMEM((c, dv), F32)],
        compiler_params=pltpu.CompilerParams(dimension_semantics=("arbitrary",),
                                             vmem_limit_bytes=VMEM_LIMIT),
        name="gla_scan",
    )(q, k, v, lgf, q, k, v, lgb, s0)


def _tail_kernel(x_ref, row_ref, col_ref, ma_ref, gb_ref, of_ref, ob_ref, sr_ref,
                 g1_ref, sh2_ref, sc2_ref, g2_ref, og_ref, n2_ref, fg_ref,
                 pb_ref, wo_ref, w1_ref, w2_ref, out_ref, *, dv):
    xp = _load_tokens(x_ref, row_ref, col_ref)
    o = of_ref[0].astype(F32) + ob_ref[0].astype(F32)
    heads = []
    for h in range(GLA_HEADS):
        sl = slice(h * dv, (h + 1) * dv)
        heads.append(_rms(o[:, sl], og_ref[:, sl]))
    yb = (jnp.concatenate(heads, axis=-1) * sr_ref[0].astype(F32)).astype(BF16)
    merged = ma_ref[0].astype(F32) + gb_ref[0].astype(F32) * _dot(yb, pb_ref[...])
    x1 = xp + g1_ref[0] * _dot(merged.astype(BF16), wo_ref[...])
    h2 = (_rms(x1, n2_ref[...]) * (1.0 + sc2_ref[0]) + sh2_ref[0]).astype(BF16)
    acc = jnp.zeros_like(x1)
    d_ff = w1_ref.shape[1]
    for j in range(d_ff // FF_COL_TILE):
        sl = slice(j * FF_COL_TILE, (j + 1) * FF_COL_TILE)
        f = jnp.maximum(_dot(h2, w1_ref[:, sl]), 0.0)
        acc = acc + _dot((f * f).astype(BF16), w2_ref[sl, :])
    x2 = x1 + g2_ref[0] * acc
    out_ref[0] = _rms(x2, fg_ref[...])


def _tail_call(x4, pos, acts, mods, consts):
    bsz, nrow, gw_, d = x4.shape
    t = nrow * gw_
    tm = TOKEN_TILE
    rows_per_tile = tm // GRID_W
    rowemb, colemb = pos
    dv = acts[2].shape[-1] // GLA_HEADS
    tok = lambda w: pl.BlockSpec((1, tm, w), lambda b, i: (b, i, 0))
    per_batch = pl.BlockSpec((1, 1, d), lambda b, i: (b, 0, 0))
    in_specs = ([pl.BlockSpec((1, rows_per_tile, GRID_W, d), lambda b, i: (b, i, 0, 0)),
                 pl.BlockSpec((rows_per_tile, 1, d // 2), lambda b, i: (i, 0, 0)),
                 _const_spec(colemb.shape)]
                + [tok(a.shape[-1]) for a in acts]
                + [per_batch] * len(mods)
                + [_const_spec(a.shape) for a in consts])
    return pl.pallas_call(
        functools.partial(_tail_kernel, dv=dv),
        out_shape=jax.ShapeDtypeStruct((bsz, t, d), F32),
        grid=(bsz, t // tm),
        in_specs=in_specs,
        out_specs=tok(d),
        compiler_params=pltpu.CompilerParams(dimension_semantics=("arbitrary", "arbitrary"),
                                             vmem_limit_bytes=VMEM_LIMIT),
        name="tail",
    )(x4, rowemb, colemb, *acts, *mods, *consts)


def _pos_tables(t, d):
    quarter = d // 4
    freqs = 1.0 / (10000.0 ** (jnp.arange(quarter, dtype=F32) / quarter))

    def emb(n):
        a = jnp.arange(n, dtype=F32)[:, None] * freqs[None, :]
        return jnp.concatenate([jnp.sin(a), jnp.cos(a)], -1)

    return emb(t // GRID_W)[:, None, :], emb(GRID_W)[None, :, :]


def kernel(x, c, ctx, c_ctx, w_ada, b_ada, norm1_g, w_in, lnv_g, lnv_b, w_s, b_s, w_dec2, b_dec,
           onorm_g, proj_a, proj_b, w_out, norm2_g, w_ff1, w_ff2, final_g):
    bsz, t, d = x.shape
    depth = w_in.shape[0]
    assert depth == 1, "single-layer problem: the context stream only feeds the scan states"
    d_a = proj_a.shape[1]
    dv_tot = proj_b.shape[1]
    dk_tot = w_dec2.shape[-1]
    seg = _segment_rows(d, d_a, dk_tot, dv_tot, w_dec2.shape[2])

    c_rows = jnp.zeros((MOD_ROWS, d), F32).at[:bsz].set(c).at[bsz].set(c_ctx)
    mod = _mod_call(c_rows, w_ada[0], b_ada[0])
    sh1, sc1, g1, sh2, sc2, g2 = jnp.split(mod, 6, axis=-1)
    latent = lambda m: m[:bsz, None, :]
    context = lambda m: m[bsz:bsz + 1]

    row = lambda a: a.reshape(1, -1)
    g1n = row(norm1_g[0])
    w_in_t = jnp.swapaxes(w_in[0], 0, 1)
    wd2, bd = w_dec2[0], b_dec[0][:, None, :]

    ck, cv, clgf, clgb = _ctx_proj_call(ctx, context(sh1), context(sc1), g1n, w_in_t, wd2, bd,
                                        seg, dk_tot, dv_tot)
    s_ctx = _state_call(ck, cv, clgf, clgb)

    pos = _pos_tables(t, d)
    x4 = x.reshape(bsz, t // GRID_W, GRID_W, d)
    consts = (g1n, wd2, bd, row(lnv_g[0]), row(lnv_b[0]), w_s[0], b_s[0][:, :, None])
    ma, gb, q, k, v, sr, lgf, lgb, *tail_weights = _inproj_call(
        x4, pos, latent(sh1), latent(sc1), consts, (w_in_t, proj_a[0]),
        (proj_b[0], w_out[0], w_ff1[0], w_ff2[0]), seg, d_a, dk_tot, dv_tot)
    o_f, o_b = _scan_call(q, k, v, lgf, lgb, s_ctx)

    mods = (latent(g1), latent(sh2), latent(sc2), latent(g2))
    return _tail_call(x4, pos, (ma, gb, o_f, o_b, sr), mods,
                      (row(onorm_g[0]), row(norm2_g[0]), row(final_g), *tail_weights))
```

```python
import functools
import math

import jax
import jax.numpy as jnp
import numpy as np
from jax import lax
from jax.experimental import pallas as pl
from jax.experimental.pallas import tpu as pltpu

F32 = jnp.float32
BF16 = jnp.bfloat16

GRID_W = 64
GLA_HEADS = 4
GATE_TAU = 16.0
EPS = 1e-6

TOKEN_TILE = 512
SCAN_CHUNK = 128
SCAN_TILE = 512
MOD_ROWS = 8
MOD_COL_TILE = 1024
PROJ_COL_TILE = 512
STAGE_LOOKAHEAD = 1
WEIGHT_STAGE_ROWS = 512
WEIGHT_STAGE_SLOTS = 4
BF16_SUBLANES = 16
FF_COL_TILE = 1024
SAFE_DECAY = 60.0
VMEM_LIMIT = 56 * 1024 * 1024


def _sigmoid(x):
    return 1.0 / (1.0 + jnp.exp(-x))


def _gelu_tanh(x):
    c = math.sqrt(2.0 / math.pi)
    return 0.5 * x * (1.0 + jnp.tanh(c * (x + 0.044715 * (x * x * x))))


def _log_sigmoid(z):
    return jnp.minimum(z, 0.0) - jnp.log(1.0 + jnp.exp(-jnp.abs(z)))


def _rms(x, g):
    return x * lax.rsqrt(jnp.mean(x * x, axis=-1, keepdims=True) + EPS) * g


def _split_bf16(a):
    hi = a.astype(BF16)
    lo = (a - hi.astype(F32)).astype(BF16)
    return hi, lo


def _dot(a, b):
    return jnp.dot(a, b, preferred_element_type=F32)


def _dot_nt(a, b):
    return lax.dot_general(a, b, (((1,), (1,)), ((), ())), preferred_element_type=F32)


def _dot_tn(a, b):
    return lax.dot_general(a, b, (((0,), (0,)), ((), ())), preferred_element_type=F32)


def _const_spec(shape):
    nd = len(shape)
    return pl.BlockSpec(shape, lambda *_: (0,) * nd, pipeline_mode=pl.Buffered(1))


MOD_COLUMNS = ("shift1", "scale1", "gate1", "shift2", "scale2", "gate2")


def _mod_spec(mod, name):
    d = mod.shape[1] // len(MOD_COLUMNS)
    j = MOD_COLUMNS.index(name)
    return pl.BlockSpec((mod.shape[0], d), lambda *_: (0, j), pipeline_mode=pl.Buffered(1))


def _load_as_bf16(src_ref, dst_ref, stage_ref, sem_ref, rows):
    n_rows, cols = src_ref.shape
    slots = stage_ref.shape[0]
    assert n_rows % rows == 0 and rows <= stage_ref.shape[1] and cols == stage_ref.shape[2]
    n = n_rows // rows

    def copy(j, slot):
        return pltpu.make_async_copy(src_ref.at[pl.ds(j * rows, rows), :],
                                     stage_ref.at[slot, pl.ds(0, rows), :], sem_ref.at[slot])

    for j in range(min(slots - 1, n)):
        copy(j, j).start()

    def body(j, carry):
        ahead = j + slots - 1

        @pl.when(ahead < n)
        def _():
            copy(ahead, ahead % slots).start()

        copy(j, j % slots).wait()
        first = pl.multiple_of(j * rows, rows)
        dst_ref[pl.ds(first, rows), :] = stage_ref[j % slots, pl.ds(0, rows), :].astype(BF16)
        return carry

    lax.fori_loop(0, n, body, 0)


def _stage_rows(n_rows):
    return max(r for r in range(BF16_SUBLANES, WEIGHT_STAGE_ROWS + 1, BF16_SUBLANES)
               if n_rows % r == 0)


def _mod_kernel(c_ref, w_ref, b_ref, o_ref):
    c = c_ref[...]
    a_hi, a_lo = _split_bf16(c * _sigmoid(c))
    w_hi, w_lo = _split_bf16(w_ref[...])
    rows = c.shape[0]
    both = _dot(jnp.concatenate([a_hi, a_lo], axis=0), w_hi)
    o_ref[...] = both[:rows] + both[rows:] + _dot(a_hi, w_lo) + b_ref[...]


def _mod_call(c_rows, w_ada, b_ada):
    d, n = w_ada.shape
    return pl.pallas_call(
        _mod_kernel,
        out_shape=jax.ShapeDtypeStruct((MOD_ROWS, n), F32),
        grid=(n // MOD_COL_TILE,),
        in_specs=[pl.BlockSpec((MOD_ROWS, d), lambda j: (0, 0)),
                  pl.BlockSpec((d, MOD_COL_TILE), lambda j: (0, j)),
                  pl.BlockSpec((1, MOD_COL_TILE), lambda j: (0, j))],
        out_specs=pl.BlockSpec((MOD_ROWS, MOD_COL_TILE), lambda j: (0, j)),
        compiler_params=pltpu.CompilerParams(dimension_semantics=("arbitrary",)),
        name="adaln_mod",
    )(c_rows, w_ada, b_ada.reshape(1, n))


def _load_tokens(x_ref, row_ref, col_ref):
    x = x_ref[0]
    half = x.shape[-1] // 2
    x = jnp.concatenate([x[..., :half] + row_ref[...], x[..., half:] + col_ref[...]], axis=-1)
    return x.reshape(x.shape[0] * x.shape[1], x.shape[2])


def _segment_rows(d, d_a, dk_tot, dv_tot, rank):
    names = ("u", "va", "q", "k", "v", "r", "dec", "ga", "gb")
    widths = (d_a, d_a, dk_tot, dk_tot, dv_tot, dv_tot, 2 * rank, d, d)
    offsets, r0 = {}, 0
    for name, w in zip(names, widths):
        offsets[name] = r0
        r0 += w
    return offsets


def _store_log_decays(dec, wdec2_ref, bdec_ref, out_refs):
    rank = wdec2_ref.shape[1]
    for di, out_ref in enumerate(out_refs):
        z = _dot(dec[:, di * rank:(di + 1) * rank].astype(BF16),
                 wdec2_ref[di].astype(BF16)) + bdec_ref[di]
        out_ref[0] = _log_sigmoid(z) * (1.0 / GATE_TAU)


def _ctx_proj_kernel(x_ref, sh_ref, sc_ref, g1_ref, wk_ref, wv_ref, wd_ref, wdec2_ref, bdec_ref,
                     k_ref, v_ref, lgf_ref, lgb_ref, *, mod_row):
    hb = (_rms(x_ref[0], g1_ref[...]) * (1.0 + sc_ref[mod_row:mod_row + 1, :])
          + sh_ref[mod_row:mod_row + 1, :]).astype(BF16)
    proj = lambda w_ref: _dot_nt(hb, w_ref[...].astype(BF16))
    _store_log_decays(proj(wd_ref), wdec2_ref, bdec_ref, (lgf_ref, lgb_ref))
    k_ref[0] = proj(wk_ref).astype(BF16)
    v_ref[0] = proj(wv_ref).astype(BF16)


def _ctx_proj_call(ctx, mod, mod_row, g1, w_in_t, w_dec2, b_dec, seg, dk_tot, dv_tot):
    bsz, t, d = ctx.shape
    tm = min(TOKEN_TILE, t)
    assert t % tm == 0
    tok = lambda w: pl.BlockSpec((1, tm, w), lambda b, i: (b, i, 0))

    def rows_spec(r0, width):
        assert r0 % width == 0
        return pl.BlockSpec((width, d), lambda b, i: (r0 // width, 0),
                            pipeline_mode=pl.Buffered(1))

    rank = w_dec2.shape[1]
    tail_consts = [w_dec2, b_dec]
    widths = [dk_tot, dv_tot, dk_tot, dk_tot]
    dtypes = [BF16, BF16, F32, F32]
    return pl.pallas_call(
        functools.partial(_ctx_proj_kernel, mod_row=mod_row),
        out_shape=[jax.ShapeDtypeStruct((bsz, t, w), dt) for w, dt in zip(widths, dtypes)],
        grid=(bsz, t // tm),
        in_specs=([tok(d), _mod_spec(mod, "shift1"), _mod_spec(mod, "scale1"),
                   _const_spec(g1.shape)]
                  + [rows_spec(seg["k"], dk_tot), rows_spec(seg["v"], dv_tot),
                     rows_spec(seg["dec"], 2 * rank)]
                  + [_const_spec(a.shape) for a in tail_consts]),
        out_specs=[tok(w) for w in widths],
        compiler_params=pltpu.CompilerParams(dimension_semantics=("arbitrary", "arbitrary"),
                                             vmem_limit_bytes=VMEM_LIMIT),
        name="ctx_proj",
    )(ctx, mod, mod, g1, w_in_t, w_in_t, w_in_t, *tail_consts)


def _inproj_kernel(x_ref, row_ref, col_ref, sh_ref, sc_ref, g1_ref, wdec2_ref, bdec_ref,
                   lng_ref, lnb_ref, ws_ref, bs_ref, wt_hbm, pa_hbm, *rest,
                   seg, d_a, dk_tot, dv_tot, n_later):
    later_f32, rest = rest[:n_later], rest[n_later:]
    (ma_ref, gb_ref, q_ref, k_ref, v_ref, sr_ref, lgf_ref, lgb_ref), rest = rest[:8], rest[8:]
    later_bf16, (ya_scr, wt_ref, pa_ref, stage_ref, sem_ref) = rest[:n_later], rest[n_later:]

    @pl.when((pl.program_id(0) == 0) & (pl.program_id(1) == 0))
    def _():
        for src, dst in ((wt_hbm, wt_ref), (pa_hbm, pa_ref)):
            _load_as_bf16(src, dst, stage_ref, sem_ref, _stage_rows(src.shape[0]))

    for src, dst in zip(later_f32, later_bf16):
        dst[...] = src[...].astype(BF16)

    xp = _load_tokens(x_ref, row_ref, col_ref)
    tm = xp.shape[0]
    d = xp.shape[1]
    mine = pl.ds(pl.program_id(0), 1)
    hb = (_rms(xp, g1_ref[...]) * (1.0 + sc_ref[mine, :]) + sh_ref[mine, :]).astype(BF16)

    def proj(r0, width):
        return _dot_nt(hb, wt_ref[r0:r0 + width, :])

    r_u, r_va, r_q, r_k, r_v, r_r = (seg[n] for n in ("u", "va", "q", "k", "v", "r"))
    r_ga, r_gb = seg["ga"], seg["gb"]
    rank = wdec2_ref.shape[1]
    groups, chunk_a = ws_ref.shape[0], ws_ref.shape[1]
    dk = dk_tot // GLA_HEADS
    gw = d_a // groups
    blk = PROJ_COL_TILE
    cols = lambda j: slice(j * blk, (j + 1) * blk)
    gu, gv, vn = {}, {}, {}

    stage = {}

    def store(out_ref, j, fn):
        def ep(y):
            out_ref[0, :, cols(j)] = fn(y).astype(out_ref.dtype)
        return ep

    def keep(table, j, fn):
        def ep(y):
            table[j] = fn(y)
        return ep

    def add(name, matmul, epilogue):
        stage[name] = (matmul, epilogue)
        return name

    s_dec = add("dec", lambda: proj(seg["dec"], 2 * rank),
                lambda dec: _store_log_decays(dec, wdec2_ref, bdec_ref, (lgf_ref, lgb_ref)))
    s_u = [add(f"u{j}", lambda j=j: proj(r_u + j * blk, blk), keep(gu, j, _gelu_tanh))
           for j in range(d_a // blk)]
    s_va = [add(f"va{j}", lambda j=j: proj(r_va + j * blk, blk), keep(gv, j, _gelu_tanh))
            for j in range(d_a // blk)]

    def ep_q_and_layernorm(y):
        q_ref[0] = (y * dk ** -0.5).astype(BF16)
        nb = d_a // blk
        mu = sum(jnp.sum(gv[j], axis=-1, keepdims=True) for j in range(nb)) * (1.0 / d_a)
        cen = [gv[j] - mu for j in range(nb)]
        var = sum(jnp.sum(c * c, axis=-1, keepdims=True) for c in cen) * (1.0 / d_a)
        inv = lax.rsqrt(var + EPS)
        for j in range(nb):
            vn[j] = (cen[j] * inv * lng_ref[:, cols(j)] + lnb_ref[:, cols(j)]).astype(BF16)
    s_q = add("q", lambda: proj(r_q, dk_tot), ep_q_and_layernorm)
    def ep_k(y):
        k_ref[0] = y.astype(BF16)
    s_k = add("k", lambda: proj(r_k, dk_tot), ep_k)
    s_r = [add(f"r{j}", lambda j=j: proj(r_r + j * blk, blk),
               store(sr_ref, j, lambda y: y * _sigmoid(y))) for j in range(dv_tot // blk)]
    s_v = [add(f"v{j}", lambda j=j: proj(r_v + j * blk, blk), store(v_ref, j, lambda y: y))
           for j in range(dv_tot // blk)]

    def mm_gate(ci, g):
        r0, c0 = ci * chunk_a, g * gw
        return _dot(ws_ref[g].astype(BF16),
                    vn[c0 // blk][r0:r0 + chunk_a, c0 % blk:c0 % blk + gw])

    def ep_gate(ci, g):
        r0, c0 = ci * chunk_a, g * gw

        def ep(s):
            u = gu[c0 // blk][r0:r0 + chunk_a, c0 % blk:c0 % blk + gw]
            ya_scr[r0:r0 + chunk_a, c0:c0 + gw] = (u * (s + bs_ref[g])).astype(BF16)
        return ep
    s_mix = [add(f"mix{ci}_{g}", lambda ci=ci, g=g: mm_gate(ci, g), ep_gate(ci, g))
             for ci in range(tm // chunk_a) for g in range(groups)]
    s_gb = [add(f"gb{j}", lambda j=j: proj(r_gb + j * blk, blk), store(gb_ref, j, _sigmoid))
            for j in range(d // blk)]
    s_ga = [add(f"ga{j}", lambda j=j: (proj(r_ga + j * blk, blk),
                                       _dot(ya_scr[...], pa_ref[:, cols(j)])),
                store(ma_ref, j, lambda y: _sigmoid(y[0]) * y[1])) for j in range(d // blk)]

    order = [s_dec] + s_u + s_va + [s_q, s_k] + s_r + s_mix + s_gb + s_ga + s_v
    assert sorted(order) == sorted(stage)
    results = {}
    for i in range(len(order) + STAGE_LOOKAHEAD):
        if i < len(order):
            results[order[i]] = stage[order[i]][0]()
        if i >= STAGE_LOOKAHEAD:
            done = order[i - STAGE_LOOKAHEAD]
            stage[done][1](results.pop(done))


def _inproj_call(x4, pos, mod, consts, weights, later_weights, seg, d_a, dk_tot, dv_tot):
    bsz, nrow, gw_, d = x4.shape
    t = nrow * gw_
    tm = TOKEN_TILE
    nt = t // tm
    rows_per_tile = tm // GRID_W
    rowemb, colemb = pos
    chunk_a = consts[5].shape[1]
    assert tm % chunk_a == 0 and tm % GRID_W == 0 and t % tm == 0

    def slab(w):
        rows = w.shape[0] // (bsz * nt)
        assert w.shape[0] % (bsz * nt) == 0 and rows % BF16_SUBLANES == 0
        return pl.BlockSpec((rows, w.shape[1]), lambda b, i: (b * nt + i, 0))

    tok = lambda w: pl.BlockSpec((1, tm, w), lambda b, i: (b, i, 0))
    in_specs = ([pl.BlockSpec((1, rows_per_tile, GRID_W, d), lambda b, i: (b, i, 0, 0)),
                 pl.BlockSpec((rows_per_tile, 1, d // 2), lambda b, i: (i, 0, 0)),
                 _const_spec(colemb.shape), _mod_spec(mod, "shift1"), _mod_spec(mod, "scale1")]
                + [_const_spec(a.shape) for a in consts]
                + [pl.BlockSpec(memory_space=pl.ANY)] * len(weights)
                + [slab(w) for w in later_weights])
    widths = [d, d, dk_tot, dk_tot, dv_tot, dv_tot, dk_tot, dk_tot]
    dtypes = [BF16] * 6 + [F32] * 2
    kern = functools.partial(_inproj_kernel, seg=seg, d_a=d_a, dk_tot=dk_tot, dv_tot=dv_tot,
                             n_later=len(later_weights))
    return pl.pallas_call(
        kern,
        out_shape=([jax.ShapeDtypeStruct((bsz, t, w), dt) for w, dt in zip(widths, dtypes)]
                   + [jax.ShapeDtypeStruct(w.shape, BF16) for w in later_weights]),
        grid=(bsz, nt),
        in_specs=in_specs,
        out_specs=[tok(w) for w in widths] + [slab(w) for w in later_weights],
        scratch_shapes=([pltpu.VMEM((tm, d_a), BF16)]
                        + [pltpu.VMEM(w.shape, BF16) for w in weights]
                        + [pltpu.VMEM((WEIGHT_STAGE_SLOTS, WEIGHT_STAGE_ROWS, d), F32),
                           pltpu.SemaphoreType.DMA((WEIGHT_STAGE_SLOTS,))]),
        compiler_params=pltpu.CompilerParams(dimension_semantics=("arbitrary", "arbitrary"),
                                             vmem_limit_bytes=VMEM_LIMIT),
        name="inproj",
    )(x4, rowemb, colemb, mod, mod, *consts, *weights, *later_weights)


def _scan_masks(c):
    row = lax.broadcasted_iota(jnp.int32, (c, c), 0)
    col = lax.broadcasted_iota(jnp.int32, (c, c), 1)
    return ((row >= col, c - 1), (row <= col, 0))


def _chunk_cumsum(lg, keep, end):
    tri = jnp.where(keep, 1.0, 0.0).astype(BF16)
    hi, lo = _split_bf16(lg)
    b = _dot(jnp.concatenate([tri, tri], axis=1), jnp.concatenate([hi, lo], axis=0))
    return b, b[end:end + 1, :]


def _decay_rows(st, tot):
    dk, dv = st.shape
    dcol = jnp.transpose(jnp.broadcast_to(jnp.exp(tot), (dk, dk)))
    return st * jnp.concatenate([dcol] * (dv // dk), axis=1)


def _state_kernel(kf_ref, vf_ref, lgf_ref, kb_ref, vb_ref, lgb_ref, s_ref, st_scr, *, dk, dv):
    t = pl.program_id(0)
    bsz, c = kf_ref.shape[0], SCAN_CHUNK
    n_sub = kf_ref.shape[1] // c

    @pl.when(t == 0)
    def _():
        st_scr[...] = jnp.zeros_like(st_scr)

    masks = _scan_masks(c)
    dirs = ((kf_ref, vf_ref, lgf_ref), (kb_ref, vb_ref, lgb_ref))
    for sub in range(n_sub):
        for bi in range(bsz):
            for di, (k_ref, v_ref, lg_ref) in enumerate(dirs):
                first = (sub if di == 0 else n_sub - 1 - sub) * c
                b, tot = _chunk_cumsum(lg_ref[bi, first:first + c, :], *masks[di])
                for h in range(GLA_HEADS):
                    sl = slice(h * dk, (h + 1) * dk)
                    k = k_ref[bi, first:first + c, sl].astype(F32)
                    kd = (k * jnp.exp(tot[:, sl] - b[:, sl])).astype(BF16)
                    v = v_ref[bi, first:first + c, h * dv:(h + 1) * dv]
                    st_scr[bi, di, h] = _decay_rows(st_scr[bi, di, h], tot[:, sl]) + _dot_tn(kd, v)

    @pl.when(t == pl.num_programs(0) - 1)
    def _():
        s_ref[...] = st_scr[...]


def _state_call(k, v, lgf, lgb):
    bsz, t, dk_tot = k.shape
    dv_tot = v.shape[-1]
    dk, dv = dk_tot // GLA_HEADS, dv_tot // GLA_HEADS
    tile = min(SCAN_TILE, t)
    nt = t // tile
    assert t % tile == 0 and tile % SCAN_CHUNK == 0 and dv % dk == 0
    fwd = lambda w: pl.BlockSpec((bsz, tile, w), lambda i: (0, i, 0))
    bwd = lambda w: pl.BlockSpec((bsz, tile, w), lambda i: (0, nt - 1 - i, 0))
    shape = (bsz, 2, GLA_HEADS, dk, dv)
    return pl.pallas_call(
        functools.partial(_state_kernel, dk=dk, dv=dv),
        out_shape=jax.ShapeDtypeStruct(shape, F32),
        grid=(nt,),
        in_specs=[fwd(dk_tot), fwd(dv_tot), fwd(dk_tot), bwd(dk_tot), bwd(dv_tot), bwd(dk_tot)],
        out_specs=pl.BlockSpec(shape, lambda i: (0,) * len(shape)),
        scratch_shapes=[pltpu.VMEM(shape, F32)],
        compiler_params=pltpu.CompilerParams(dimension_semantics=("arbitrary",),
                                             vmem_limit_bytes=VMEM_LIMIT),
        name="gla_ctx_state",
    )(k, v, lgf, k, v, lgb)


def _scan_kernel(qf_ref, kf_ref, vf_ref, lgf_ref, qb_ref, kb_ref, vb_ref, lgb_ref, s0_ref,
                 of_ref, ob_ref, st_scr, q32_scr, k32_scr, v32_scr, b_scr, o_scr,
                 *, dk, dv):
    t = pl.program_id(0)
    bsz, c = qf_ref.shape[0], SCAN_CHUNK
    n_sub = qf_ref.shape[1] // c

    def rows(di, sub):
        first = (sub if di == 0 else n_sub - 1 - sub) * c
        return slice(first, first + c)

    @pl.when(t == 0)
    def _():
        st_scr[...] = s0_ref[...]

    masks = _scan_masks(c)
    dirs = ((qf_ref, kf_ref, vf_ref, lgf_ref, of_ref) + masks[0],
            (qb_ref, kb_ref, vb_ref, lgb_ref, ob_ref) + masks[1])

    worst = None
    for lg_ref in (lgf_ref, lgb_ref):
        for sub in range(n_sub):
            w = jnp.max(-jnp.sum(lg_ref[:, sub * c:(sub + 1) * c, :], axis=1))
            worst = w if worst is None else jnp.maximum(worst, w)

    cums = {}

    def cumulate(bi, di, sub):
        lg_ref, keep, end = dirs[di][3], dirs[di][5], dirs[di][6]
        cums[bi, di, sub] = _chunk_cumsum(lg_ref[bi, rows(di, sub), :], keep, end)

    def load(bi, di, h, sub):
        q_ref, k_ref, v_ref = dirs[di][:3]
        b, tot = cums[bi, di, sub]
        sl = slice(h * dk, (h + 1) * dk)
        return (q_ref[bi, rows(di, sub), sl].astype(F32), k_ref[bi, rows(di, sub), sl].astype(F32),
                v_ref[bi, rows(di, sub), h * dv:(h + 1) * dv], b[:, sl], tot[:, sl])

    def decayed_state(bi, di, h, tot):
        return _decay_rows(st_scr[bi, di, h], tot)

    def fast_step():
        chains = [(bi, di, h, sub) for sub in range(n_sub) for bi in range(bsz)
                  for di in range(2) for h in range(GLA_HEADS)]
        live = {}

        def factors(n):
            q, k, v, bh, tot = load(*chains[n])
            half = 0.5 * tot
            eh = jnp.exp(half)
            qe = q * jnp.exp(bh - half)
            ke = k * jnp.exp(half - bh)
            live[n] = dict(v=v, tot=tot, qe=qe.astype(BF16), ke=ke.astype(BF16),
                           qd=(qe * eh).astype(BF16), kd=(ke * eh).astype(BF16))

        def scores(n):
            live[n]["a"] = _dot_nt(live[n].pop("qe"), live[n].pop("ke"))

        def outputs(n):
            bi, di, h, _ = chains[n]
            c_ = live[n]
            a = jnp.where(dirs[di][5], c_.pop("a"), 0.0).astype(BF16)
            lhs = jnp.concatenate([a, c_.pop("qd")], axis=1)
            rhs = jnp.concatenate([c_["v"], st_scr[bi, di, h].astype(BF16)], axis=0)
            c_["o"] = _dot(lhs, rhs)
            c_["u"] = _dot_tn(c_.pop("kd"), c_.pop("v"))

        def finish(n):
            bi, di, h, sub = chains[n]
            c_ = live.pop(n)
            o_ref = dirs[di][4]
            o_ref[bi, rows(di, sub), h * dv:(h + 1) * dv] = c_["o"].astype(o_ref.dtype)
            st_scr[bi, di, h] = decayed_state(bi, di, h, c_["tot"]) + c_["u"]

        def cumulate_for(n):
            bi, di, _, sub = chains[n]
            cumulate(bi, di, sub)

        phases = ((factors, 0), (scores, 1), (outputs, 3), (finish, 5))
        cumulate_for(0)
        for i in range(len(chains) + phases[-1][1]):
            ahead = i + GLA_HEADS
            if i % GLA_HEADS == 0 and ahead < len(chains):
                cumulate_for(ahead)
            for phase, lag in phases:
                if 0 <= i - lag < len(chains):
                    phase(i - lag)

    def safe_step():
        ridx = lax.broadcasted_iota(jnp.int32, (c, 1), 0)
        groups = [(s, b, d) for s in range(n_sub) for b in range(bsz) for d in range(2)]
        for sub, bi, di in groups:
            o_ref = dirs[di][4]
            cumulate(bi, di, sub)
            for h in range(GLA_HEADS):
                q, k, v, bh, tot = load(bi, di, h, sub)
                q32_scr[...] = q
                k32_scr[...] = k
                v32_scr[...] = v.astype(F32)
                b_scr[...] = bh
                o_scr[...] = _dot((q * jnp.exp(bh)).astype(BF16), st_scr[bi, di, h].astype(BF16))
                st_scr[bi, di, h] = (decayed_state(bi, di, h, tot)
                                     + _dot_tn((k * jnp.exp(tot - bh)).astype(BF16), v))

                def body(j, carry):
                    bj = b_scr[pl.ds(j, 1), :]
                    valid = (ridx >= j) if di == 0 else (ridx <= j)
                    e = jnp.exp(jnp.where(valid, b_scr[...] - bj, -jnp.inf))
                    w = jnp.sum(q32_scr[...] * e * k32_scr[pl.ds(j, 1), :], axis=-1, keepdims=True)
                    o_scr[...] += w * v32_scr[pl.ds(j, 1), :]
                    return carry

                lax.fori_loop(0, c, body, 0)
                o_ref[bi, rows(di, sub), h * dv:(h + 1) * dv] = o_scr[...].astype(o_ref.dtype)

    lax.cond(worst <= SAFE_DECAY, fast_step, safe_step)


def _scan_call(q, k, v, lgf, lgb, s0):
    bsz, t, dk_tot = q.shape
    dv_tot = v.shape[-1]
    dk, dv = dk_tot // GLA_HEADS, dv_tot // GLA_HEADS
    c = SCAN_CHUNK
    tile = SCAN_TILE
    nt = t // tile
    assert t % tile == 0 and tile % c == 0 and dv % dk == 0
    fwd = lambda w: pl.BlockSpec((bsz, tile, w), lambda i: (0, i, 0))
    bwd = lambda w: pl.BlockSpec((bsz, tile, w), lambda i: (0, nt - 1 - i, 0))
    st_spec = pl.BlockSpec(s0.shape, lambda i: (0,) * s0.ndim)
    kern = functools.partial(_scan_kernel, dk=dk, dv=dv)
    return pl.pallas_call(
        kern,
        out_shape=[jax.ShapeDtypeStruct((bsz, t, dv_tot), BF16),
                   jax.ShapeDtypeStruct((bsz, t, dv_tot), BF16)],
        grid=(nt,),
        in_specs=[fwd(dk_tot), fwd(dk_tot), fwd(dv_tot), fwd(dk_tot),
                  bwd(dk_tot), bwd(dk_tot), bwd(dv_tot), bwd(dk_tot), st_spec],
        out_specs=[fwd(dv_tot), bwd(dv_tot)],
        scratch_shapes=[pltpu.VMEM(s0.shape, F32),
                        pltpu.VMEM((c, dk), F32), pltpu.VMEM((c, dk), F32),
                        pltpu.VMEM((c, dv), F32), pltpu.VMEM((c, dk), F32),
                        pltpu.VMEM((c, dv), F32)],
        compiler_params=pltpu.CompilerParams(dimension_semantics=("arbitrary",),
                                             vmem_limit_bytes=VMEM_LIMIT),
        name="gla_scan",
    )(q, k, v, lgf, q, k, v, lgb, s0)


def _tail_kernel(x_ref, row_ref, col_ref, ma_ref, gb_ref, of_ref, ob_ref, sr_ref,
                 g1_ref, sh2_ref, sc2_ref, g2_ref, og_ref, n2_ref, fg_ref,
                 pb_ref, wo_ref, w1_ref, w2_ref, out_ref, *, dv):
    xp = _load_tokens(x_ref, row_ref, col_ref)
    mine = pl.ds(pl.program_id(0), 1)
    o = of_ref[0].astype(F32) + ob_ref[0].astype(F32)
    heads = []
    for h in range(GLA_HEADS):
        sl = slice(h * dv, (h + 1) * dv)
        heads.append(_rms(o[:, sl], og_ref[:, sl]))
    yb = (jnp.concatenate(heads, axis=-1) * sr_ref[0].astype(F32)).astype(BF16)
    merged = ma_ref[0].astype(F32) + gb_ref[0].astype(F32) * _dot(yb, pb_ref[...])
    x1 = xp + g1_ref[mine, :] * _dot(merged.astype(BF16), wo_ref[...])
    h2 = (_rms(x1, n2_ref[...]) * (1.0 + sc2_ref[mine, :]) + sh2_ref[mine, :]).astype(BF16)
    acc = jnp.zeros_like(x1)
    d_ff = w1_ref.shape[1]
    for j in range(d_ff // FF_COL_TILE):
        sl = slice(j * FF_COL_TILE, (j + 1) * FF_COL_TILE)
        f = jnp.maximum(_dot(h2, w1_ref[:, sl]), 0.0)
        acc = acc + _dot((f * f).astype(BF16), w2_ref[sl, :])
    x2 = x1 + g2_ref[mine, :] * acc
    out_ref[0] = _rms(x2, fg_ref[...])


def _tail_call(x4, pos, acts, mod, consts):
    bsz, nrow, gw_, d = x4.shape
    t = nrow * gw_
    tm = TOKEN_TILE
    rows_per_tile = tm // GRID_W
    rowemb, colemb = pos
    dv = acts[2].shape[-1] // GLA_HEADS
    tok = lambda w: pl.BlockSpec((1, tm, w), lambda b, i: (b, i, 0))
    mod_names = ("gate1", "shift2", "scale2", "gate2")
    in_specs = ([pl.BlockSpec((1, rows_per_tile, GRID_W, d), lambda b, i: (b, i, 0, 0)),
                 pl.BlockSpec((rows_per_tile, 1, d // 2), lambda b, i: (i, 0, 0)),
                 _const_spec(colemb.shape)]
                + [tok(a.shape[-1]) for a in acts]
                + [_mod_spec(mod, name) for name in mod_names]
                + [_const_spec(a.shape) for a in consts])
    return pl.pallas_call(
        functools.partial(_tail_kernel, dv=dv),
        out_shape=jax.ShapeDtypeStruct((bsz, t, d), F32),
        grid=(bsz, t // tm),
        in_specs=in_specs,
        out_specs=tok(d),
        compiler_params=pltpu.CompilerParams(dimension_semantics=("arbitrary", "arbitrary"),
                                             vmem_limit_bytes=VMEM_LIMIT),
        name="tail",
    )(x4, rowemb, colemb, *acts, *[mod] * len(mod_names), *consts)


def _pos_tables(t, d):
    quarter = d // 4
    one, base = np.float32(1.0), np.float32(10000.0)
    freqs = one / (base ** (np.arange(quarter, dtype=np.float32) / np.float32(quarter)))

    def emb(n):
        a = np.arange(n, dtype=np.float32)[:, None] * freqs[None, :]
        return np.concatenate([np.sin(a), np.cos(a)], -1).astype(np.float32)

    return jnp.asarray(emb(t // GRID_W)[:, None, :]), jnp.asarray(emb(GRID_W)[None, :, :])


def kernel(x, c, ctx, c_ctx, w_ada, b_ada, norm1_g, w_in, lnv_g, lnv_b, w_s, b_s, w_dec2, b_dec,
           onorm_g, proj_a, proj_b, w_out, norm2_g, w_ff1, w_ff2, final_g):
    bsz, t, d = x.shape
    depth = w_in.shape[0]
    assert depth == 1, "single-layer problem: the context stream only feeds the scan states"
    d_a = proj_a.shape[1]
    dv_tot = proj_b.shape[1]
    dk_tot = w_dec2.shape[-1]
    seg = _segment_rows(d, d_a, dk_tot, dv_tot, w_dec2.shape[2])

    c_rows = jnp.zeros((MOD_ROWS, d), F32).at[:bsz].set(c).at[bsz].set(c_ctx)
    mod = _mod_call(c_rows, w_ada[0], b_ada[0])

    row = lambda a: a.reshape(1, -1)
    g1n = row(norm1_g[0])
    w_in_t = jnp.swapaxes(w_in[0], 0, 1)
    wd2, bd = w_dec2[0], b_dec[0][:, None, :]

    ck, cv, clgf, clgb = _ctx_proj_call(ctx, mod, bsz, g1n, w_in_t, wd2, bd, seg, dk_tot, dv_tot)
    s_ctx = _state_call(ck, cv, clgf, clgb)

    pos = _pos_tables(t, d)
    x4 = x.reshape(bsz, t // GRID_W, GRID_W, d)
    consts = (g1n, wd2, bd, row(lnv_g[0]), row(lnv_b[0]), w_s[0], b_s[0][:, :, None])
    ma, gb, q, k, v, sr, lgf, lgb, *tail_weights = _inproj_call(
        x4, pos, mod, consts, (w_in_t, proj_a[0]),
        (proj_b[0], w_out[0], w_ff1[0], w_ff2[0]), seg, d_a, dk_tot, dv_tot)
    o_f, o_b = _scan_call(q, k, v, lgf, lgb, s_ctx)

    return _tail_call(x4, pos, (ma, gb, o_f, o_b, sr), mod,
                      (row(onorm_g[0]), row(norm2_g[0]), row(final_g), *tail_weights))
```

```python
import functools
import math

import jax
import jax.numpy as jnp
import numpy as np
from jax import lax
from jax.experimental import pallas as pl
from jax.experimental.pallas import tpu as pltpu

F32 = jnp.float32
BF16 = jnp.bfloat16

GRID_W = 64
GLA_HEADS = 4
GATE_TAU = 16.0
EPS = 1e-6

TOKEN_TILE = 512
SCAN_CHUNK = 128
SCAN_TILE = 512
MOD_ROWS = 8
MOD_COL_TILE = 1024
PROJ_COL_TILE = 512
PREPARE_PARTS = 4
STAGE_LOOKAHEAD = 1
WEIGHT_STAGE_ROWS = 512
WEIGHT_STAGE_SLOTS = 2
BF16_SUBLANES = 16
FF_COL_TILE = 1024
SAFE_DECAY = 60.0
VMEM_LIMIT = 56 * 1024 * 1024


def _sigmoid(x):
    return 1.0 / (1.0 + jnp.exp(-x))


def _gelu_tanh(x):
    c = math.sqrt(2.0 / math.pi)
    return 0.5 * x * (1.0 + jnp.tanh(c * (x + 0.044715 * (x * x * x))))


def _log_sigmoid(z):
    return jnp.minimum(z, 0.0) - jnp.log(1.0 + jnp.exp(-jnp.abs(z)))


def _rms(x, g):
    return x * lax.rsqrt(jnp.mean(x * x, axis=-1, keepdims=True) + EPS) * g


def _split_bf16(a):
    hi = a.astype(BF16)
    lo = (a - hi.astype(F32)).astype(BF16)
    return hi, lo


def _dot(a, b):
    return jnp.dot(a, b, preferred_element_type=F32)


def _dot_nt(a, b):
    return lax.dot_general(a, b, (((1,), (1,)), ((), ())), preferred_element_type=F32)


def _dot_tn(a, b):
    return lax.dot_general(a, b, (((0,), (0,)), ((), ())), preferred_element_type=F32)


def _const_spec(shape):
    nd = len(shape)
    return pl.BlockSpec(shape, lambda *_: (0,) * nd, pipeline_mode=pl.Buffered(1))


MOD_COLUMNS = ("shift1", "scale1", "gate1", "shift2", "scale2", "gate2")


def _mod_spec(mod, name):
    d = mod.shape[1] // len(MOD_COLUMNS)
    j = MOD_COLUMNS.index(name)
    return pl.BlockSpec((mod.shape[0], d), lambda *_: (0, j), pipeline_mode=pl.Buffered(1))


def _load_as_bf16(src_ref, dst_ref, stage_ref, sem_ref, rows):
    n_rows, cols = src_ref.shape
    slots = stage_ref.shape[0]
    assert n_rows % rows == 0 and rows <= stage_ref.shape[1] and cols == stage_ref.shape[2]
    n = n_rows // rows

    def copy(j, slot):
        return pltpu.make_async_copy(src_ref.at[pl.ds(j * rows, rows), :],
                                     stage_ref.at[slot, pl.ds(0, rows), :], sem_ref.at[slot])

    for j in range(min(slots - 1, n)):
        copy(j, j).start()

    def body(j, carry):
        ahead = j + slots - 1

        @pl.when(ahead < n)
        def _():
            copy(ahead, ahead % slots).start()

        copy(j, j % slots).wait()
        first = pl.multiple_of(j * rows, rows)
        dst_ref[pl.ds(first, rows), :] = stage_ref[j % slots, pl.ds(0, rows), :].astype(BF16)
        return carry

    lax.fori_loop(0, n, body, 0)


def _stage_rows(n_rows):
    return max(r for r in range(BF16_SUBLANES, WEIGHT_STAGE_ROWS + 1, BF16_SUBLANES)
               if n_rows % r == 0)


def _mod_kernel(c_ref, w_ref, b_ref, o_ref):
    c = c_ref[...]
    a_hi, a_lo = _split_bf16(c * _sigmoid(c))
    w_hi, w_lo = _split_bf16(w_ref[...])
    rows = c.shape[0]
    both = _dot(jnp.concatenate([a_hi, a_lo], axis=0), w_hi)
    o_ref[...] = both[:rows] + both[rows:] + _dot(a_hi, w_lo) + b_ref[...]


def _mod_call(c_rows, w_ada, b_ada):
    d, n = w_ada.shape
    return pl.pallas_call(
        _mod_kernel,
        out_shape=jax.ShapeDtypeStruct((MOD_ROWS, n), F32),
        grid=(n // MOD_COL_TILE,),
        in_specs=[pl.BlockSpec((MOD_ROWS, d), lambda j: (0, 0)),
                  pl.BlockSpec((d, MOD_COL_TILE), lambda j: (0, j)),
                  pl.BlockSpec((1, MOD_COL_TILE), lambda j: (0, j))],
        out_specs=pl.BlockSpec((MOD_ROWS, MOD_COL_TILE), lambda j: (0, j)),
        compiler_params=pltpu.CompilerParams(dimension_semantics=("arbitrary",)),
        name="adaln_mod",
    )(c_rows, w_ada, b_ada.reshape(1, n))


def _load_tokens(x_ref, row_ref, col_ref):
    x = x_ref[0]
    half = x.shape[-1] // 2
    x = jnp.concatenate([x[..., :half] + row_ref[...], x[..., half:] + col_ref[...]], axis=-1)
    return x.reshape(x.shape[0] * x.shape[1], x.shape[2])


def _segment_rows(d, d_a, dk_tot, dv_tot, rank):
    names = ("u", "va", "q", "k", "v", "r", "dec", "ga", "gb")
    widths = (d_a, d_a, dk_tot, dk_tot, dv_tot, dv_tot, 2 * rank, d, d)
    offsets, r0 = {}, 0
    for name, w in zip(names, widths):
        offsets[name] = r0
        r0 += w
    return offsets


def _store_log_decay(dec, wdec2_ref, bdec_ref, di, out_ref):
    rank = wdec2_ref.shape[1]
    z = _dot(dec[:, di * rank:(di + 1) * rank].astype(BF16),
             wdec2_ref[di].astype(BF16)) + bdec_ref[di]
    out_ref[0] = _log_sigmoid(z) * (1.0 / GATE_TAU)


def _ctx_proj_kernel(x_ref, sh_ref, sc_ref, g1_ref, wk_ref, wv_ref, wd_ref, wdec2_ref, bdec_ref,
                     k_ref, v_ref, lgf_ref, lgb_ref, *, mod_row):
    hb = (_rms(x_ref[0], g1_ref[...]) * (1.0 + sc_ref[mod_row:mod_row + 1, :])
          + sh_ref[mod_row:mod_row + 1, :]).astype(BF16)
    proj = lambda w_ref: _dot_nt(hb, w_ref[...].astype(BF16))
    dec = proj(wd_ref)
    for di, out_ref in enumerate((lgf_ref, lgb_ref)):
        _store_log_decay(dec, wdec2_ref, bdec_ref, di, out_ref)
    k_ref[0] = proj(wk_ref).astype(BF16)
    v_ref[0] = proj(wv_ref).astype(BF16)


def _ctx_proj_call(ctx, mod, mod_row, g1, w_in_t, w_dec2, b_dec, seg, dk_tot, dv_tot):
    bsz, t, d = ctx.shape
    tm = min(TOKEN_TILE, t)
    assert t % tm == 0
    tok = lambda w: pl.BlockSpec((1, tm, w), lambda b, i: (b, i, 0))

    def rows_spec(r0, width):
        assert r0 % width == 0
        return pl.BlockSpec((width, d), lambda b, i: (r0 // width, 0),
                            pipeline_mode=pl.Buffered(1))

    rank = w_dec2.shape[1]
    tail_consts = [w_dec2, b_dec]
    widths = [dk_tot, dv_tot, dk_tot, dk_tot]
    dtypes = [BF16, BF16, F32, F32]
    return pl.pallas_call(
        functools.partial(_ctx_proj_kernel, mod_row=mod_row),
        out_shape=[jax.ShapeDtypeStruct((bsz, t, w), dt) for w, dt in zip(widths, dtypes)],
        grid=(bsz, t // tm),
        in_specs=([tok(d), _mod_spec(mod, "shift1"), _mod_spec(mod, "scale1"),
                   _const_spec(g1.shape)]
                  + [rows_spec(seg["k"], dk_tot), rows_spec(seg["v"], dv_tot),
                     rows_spec(seg["dec"], 2 * rank)]
                  + [_const_spec(a.shape) for a in tail_consts]),
        out_specs=[tok(w) for w in widths],
        compiler_params=pltpu.CompilerParams(dimension_semantics=("arbitrary", "arbitrary"),
                                             vmem_limit_bytes=VMEM_LIMIT),
        name="ctx_proj",
    )(ctx, mod, mod, g1, w_in_t, w_in_t, w_in_t, *tail_consts)


def _inproj_kernel(x_ref, row_ref, col_ref, sh_ref, sc_ref, g1_ref, wdec2_ref, bdec_ref,
                   lng_ref, lnb_ref, ws_ref, bs_ref, wt_hbm, pa_hbm, *rest,
                   seg, d_a, dk_tot, dv_tot, n_later, nt):
    later_f32, rest = rest[:n_later], rest[n_later:]
    (ma_ref, gb_ref, q_ref, k_ref, v_ref, sr_ref, lgf_ref, lgb_ref), rest = rest[:8], rest[8:]
    later_bf16, (ya_scr, hb_scr, wt_ref, pa_ref, stage_ref, sem_ref) = rest[:n_later], rest[n_later:]
    step = pl.program_id(0)
    _, tm, d = hb_scr.shape

    grid_rows = x_ref.shape[1]

    def prepare_next(part):
        n = grid_rows // PREPARE_PARTS
        xp = _load_tokens(x_ref.at[:, pl.ds(part * n, n)], row_ref.at[pl.ds(part * n, n)], col_ref)
        batch = jnp.minimum(step, pl.num_programs(0) - 2) // nt
        mine = pl.ds(batch, 1)
        hb = _rms(xp, g1_ref[...]) * (1.0 + sc_ref[mine, :]) + sh_ref[mine, :]
        hb_scr[step % 2, pl.ds(part * n * GRID_W, n * GRID_W), :] = hb.astype(BF16)

    def warm_up():
        for src, dst in ((wt_hbm, wt_ref), (pa_hbm, pa_ref)):
            _load_as_bf16(src, dst, stage_ref, sem_ref, _stage_rows(src.shape[0]))
        for part in range(PREPARE_PARTS):
            prepare_next(part)

    lax.cond(step == 0, warm_up,
             functools.partial(_project_tile, prepare_next, rest_refs=(
                 wdec2_ref, bdec_ref, lng_ref, lnb_ref, ws_ref, bs_ref, later_f32, later_bf16,
                 ma_ref, gb_ref, q_ref, k_ref, v_ref, sr_ref, lgf_ref, lgb_ref,
                 ya_scr, hb_scr, wt_ref, pa_ref),
                 seg=seg, d_a=d_a, dk_tot=dk_tot, dv_tot=dv_tot))


def _project_tile(prepare_next, *, rest_refs, seg, d_a, dk_tot, dv_tot):
    (wdec2_ref, bdec_ref, lng_ref, lnb_ref, ws_ref, bs_ref, later_f32, later_bf16,
     ma_ref, gb_ref, q_ref, k_ref, v_ref, sr_ref, lgf_ref, lgb_ref,
     ya_scr, hb_scr, wt_ref, pa_ref) = rest_refs
    _, tm, d = hb_scr.shape
    ready = (pl.program_id(0) + 1) % 2

    for src, dst in zip(later_f32, later_bf16):
        dst[...] = src[...].astype(BF16)

    def proj(r0, width):
        return _dot_nt(hb_scr[ready], wt_ref[r0:r0 + width, :])

    r_u, r_va, r_q, r_k, r_v, r_r = (seg[n] for n in ("u", "va", "q", "k", "v", "r"))
    r_ga, r_gb = seg["ga"], seg["gb"]
    rank = wdec2_ref.shape[1]
    groups, chunk_a = ws_ref.shape[0], ws_ref.shape[1]
    dk = dk_tot // GLA_HEADS
    gw = d_a // groups
    blk = PROJ_COL_TILE
    cols = lambda j: slice(j * blk, (j + 1) * blk)
    gu, gv, vn = {}, {}, {}

    stage = {}

    def store(out_ref, j, fn):
        def ep(y):
            out_ref[0, :, cols(j)] = fn(y).astype(out_ref.dtype)
        return ep

    def keep(table, j, fn):
        def ep(y):
            table[j] = fn(y)
        return ep

    def add(name, matmul, epilogue):
        stage[name] = (matmul, epilogue)
        return name

    def ep_dec(dec):
        for di, out_ref in enumerate((lgf_ref, lgb_ref)):
            _store_log_decay(dec, wdec2_ref, bdec_ref, di, out_ref)
    s_dec = add("dec", lambda: proj(seg["dec"], 2 * rank), ep_dec)
    s_u = [add(f"u{j}", lambda j=j: proj(r_u + j * blk, blk), keep(gu, j, _gelu_tanh))
           for j in range(d_a // blk)]
    s_va = [add(f"va{j}", lambda j=j: proj(r_va + j * blk, blk), keep(gv, j, _gelu_tanh))
            for j in range(d_a // blk)]

    def ep_q_and_layernorm(y):
        q_ref[0] = (y * dk ** -0.5).astype(BF16)
        nb = d_a // blk
        mu = sum(jnp.sum(gv[j], axis=-1, keepdims=True) for j in range(nb)) * (1.0 / d_a)
        cen = [gv[j] - mu for j in range(nb)]
        var = sum(jnp.sum(c * c, axis=-1, keepdims=True) for c in cen) * (1.0 / d_a)
        inv = lax.rsqrt(var + EPS)
        for j in range(nb):
            vn[j] = (cen[j] * inv * lng_ref[:, cols(j)] + lnb_ref[:, cols(j)]).astype(BF16)
    s_q = add("q", lambda: proj(r_q, dk_tot), ep_q_and_layernorm)
    def ep_k(y):
        k_ref[0] = y.astype(BF16)
    s_k = add("k", lambda: proj(r_k, dk_tot), ep_k)
    s_r = [add(f"r{j}", lambda j=j: proj(r_r + j * blk, blk),
               store(sr_ref, j, lambda y: y * _sigmoid(y))) for j in range(dv_tot // blk)]
    s_v = [add(f"v{j}", lambda j=j: proj(r_v + j * blk, blk), store(v_ref, j, lambda y: y))
           for j in range(dv_tot // blk)]

    def mm_gate(ci, g):
        r0, c0 = ci * chunk_a, g * gw
        return _dot(ws_ref[g].astype(BF16),
                    vn[c0 // blk][r0:r0 + chunk_a, c0 % blk:c0 % blk + gw])

    def ep_gate(ci, g):
        r0, c0 = ci * chunk_a, g * gw

        def ep(s):
            u = gu[c0 // blk][r0:r0 + chunk_a, c0 % blk:c0 % blk + gw]
            ya_scr[r0:r0 + chunk_a, c0:c0 + gw] = (u * (s + bs_ref[g])).astype(BF16)
        return ep
    s_mix = [add(f"mix{ci}_{g}", lambda ci=ci, g=g: mm_gate(ci, g), ep_gate(ci, g))
             for ci in range(tm // chunk_a) for g in range(groups)]
    s_gb = [add(f"gb{j}", lambda j=j: proj(r_gb + j * blk, blk), store(gb_ref, j, _sigmoid))
            for j in range(d // blk)]
    s_ga = [add(f"ga{j}", lambda j=j: (proj(r_ga + j * blk, blk),
                                       _dot(ya_scr[...], pa_ref[:, cols(j)])),
                store(ma_ref, j, lambda y: _sigmoid(y[0]) * y[1])) for j in range(d // blk)]

    order = ([s_dec] + s_u + s_va + [s_q, s_k] + s_r + s_gb + s_mix + s_v[:1] + s_ga + s_v[1:])
    assert sorted(order) == sorted(stage)
    extra = {}
    for part, name in enumerate(order[-PREPARE_PARTS - 1:-1]):
        extra.setdefault(name, []).append(functools.partial(prepare_next, part))
    results = {}
    for i in range(len(order) + STAGE_LOOKAHEAD):
        if i < len(order):
            results[order[i]] = stage[order[i]][0]()
        if i >= STAGE_LOOKAHEAD:
            done = order[i - STAGE_LOOKAHEAD]
            stage[done][1](results.pop(done))
            for work in extra.get(done, ()):
                work()


def _inproj_call(x4, pos, mod, consts, weights, later_weights, seg, d_a, dk_tot, dv_tot):
    bsz, nrow, gw_, d = x4.shape
    t = nrow * gw_
    tm = TOKEN_TILE
    nt = t // tm
    rows_per_tile = tm // GRID_W
    rowemb, colemb = pos
    chunk_a = consts[5].shape[1]
    assert tm % chunk_a == 0 and tm % GRID_W == 0 and t % tm == 0

    n_tiles = bsz * nt
    preparing = lambda s: jnp.minimum(s, n_tiles - 1)
    projecting = lambda s: jnp.maximum(s - 1, 0)

    def slab(w):
        rows = w.shape[0] // n_tiles
        assert w.shape[0] % n_tiles == 0 and rows % BF16_SUBLANES == 0
        return pl.BlockSpec((rows, w.shape[1]), lambda s: (projecting(s), 0))

    tok = lambda w: pl.BlockSpec((1, tm, w), lambda s: (projecting(s) // nt, projecting(s) % nt, 0))
    in_specs = ([pl.BlockSpec((1, rows_per_tile, GRID_W, d),
                              lambda s: (preparing(s) // nt, preparing(s) % nt, 0, 0)),
                 pl.BlockSpec((rows_per_tile, 1, d // 2), lambda s: (preparing(s) % nt, 0, 0)),
                 _const_spec(colemb.shape), _mod_spec(mod, "shift1"), _mod_spec(mod, "scale1")]
                + [_const_spec(a.shape) for a in consts]
                + [pl.BlockSpec(memory_space=pl.ANY)] * len(weights)
                + [slab(w) for w in later_weights])
    widths = [d, d, dk_tot, dk_tot, dv_tot, dv_tot, dk_tot, dk_tot]
    dtypes = [BF16] * 6 + [F32] * 2
    kern = functools.partial(_inproj_kernel, seg=seg, d_a=d_a, dk_tot=dk_tot, dv_tot=dv_tot,
                             n_later=len(later_weights), nt=nt)
    return pl.pallas_call(
        kern,
        out_shape=([jax.ShapeDtypeStruct((bsz, t, w), dt) for w, dt in zip(widths, dtypes)]
                   + [jax.ShapeDtypeStruct(w.shape, BF16) for w in later_weights]),
        grid=(n_tiles + 1,),
        in_specs=in_specs,
        out_specs=[tok(w) for w in widths] + [slab(w) for w in later_weights],
        scratch_shapes=([pltpu.VMEM((tm, d_a), BF16), pltpu.VMEM((2, tm, d), BF16)]
                        + [pltpu.VMEM(w.shape, BF16) for w in weights]
                        + [pltpu.VMEM((WEIGHT_STAGE_SLOTS, WEIGHT_STAGE_ROWS, d), F32),
                           pltpu.SemaphoreType.DMA((WEIGHT_STAGE_SLOTS,))]),
        compiler_params=pltpu.CompilerParams(dimension_semantics=("arbitrary",),
                                             vmem_limit_bytes=VMEM_LIMIT),
        name="inproj",
    )(x4, rowemb, colemb, mod, mod, *consts, *weights, *later_weights)


def _scan_masks(c):
    row = lax.broadcasted_iota(jnp.int32, (c, c), 0)
    col = lax.broadcasted_iota(jnp.int32, (c, c), 1)
    return ((row >= col, c - 1), (row <= col, 0))


def _chunk_cumsum(lg, keep, end):
    tri = jnp.where(keep, 1.0, 0.0).astype(BF16)
    hi, lo = _split_bf16(lg)
    b = _dot(jnp.concatenate([tri, tri], axis=1), jnp.concatenate([hi, lo], axis=0))
    return b, b[end:end + 1, :]


def _decay_rows(st, tot):
    dk, dv = st.shape
    dcol = jnp.transpose(jnp.broadcast_to(jnp.exp(tot), (dk, dk)))
    return st * jnp.concatenate([dcol] * (dv // dk), axis=1)


def _state_kernel(kf_ref, vf_ref, lgf_ref, kb_ref, vb_ref, lgb_ref, s_ref, st_scr, *, dk, dv):
    t = pl.program_id(0)
    bsz, c = kf_ref.shape[0], SCAN_CHUNK
    n_sub = kf_ref.shape[1] // c

    @pl.when(t == 0)
    def _():
        st_scr[...] = jnp.zeros_like(st_scr)

    masks = _scan_masks(c)
    dirs = ((kf_ref, vf_ref, lgf_ref), (kb_ref, vb_ref, lgb_ref))
    for sub in range(n_sub):
        for bi in range(bsz):
            for di, (k_ref, v_ref, lg_ref) in enumerate(dirs):
                first = (sub if di == 0 else n_sub - 1 - sub) * c
                b, tot = _chunk_cumsum(lg_ref[bi, first:first + c, :], *masks[di])
                for h in range(GLA_HEADS):
                    sl = slice(h * dk, (h + 1) * dk)
                    k = k_ref[bi, first:first + c, sl].astype(F32)
                    kd = (k * jnp.exp(tot[:, sl] - b[:, sl])).astype(BF16)
                    v = v_ref[bi, first:first + c, h * dv:(h + 1) * dv]
                    st_scr[bi, di, h] = _decay_rows(st_scr[bi, di, h], tot[:, sl]) + _dot_tn(kd, v)

    @pl.when(t == pl.num_programs(0) - 1)
    def _():
        s_ref[...] = st_scr[...]


def _state_call(k, v, lgf, lgb):
    bsz, t, dk_tot = k.shape
    dv_tot = v.shape[-1]
    dk, dv = dk_tot // GLA_HEADS, dv_tot // GLA_HEADS
    tile = min(SCAN_TILE, t)
    nt = t // tile
    assert t % tile == 0 and tile % SCAN_CHUNK == 0 and dv % dk == 0
    fwd = lambda w: pl.BlockSpec((bsz, tile, w), lambda i: (0, i, 0))
    bwd = lambda w: pl.BlockSpec((bsz, tile, w), lambda i: (0, nt - 1 - i, 0))
    shape = (bsz, 2, GLA_HEADS, dk, dv)
    return pl.pallas_call(
        functools.partial(_state_kernel, dk=dk, dv=dv),
        out_shape=jax.ShapeDtypeStruct(shape, F32),
        grid=(nt,),
        in_specs=[fwd(dk_tot), fwd(dv_tot), fwd(dk_tot), bwd(dk_tot), bwd(dv_tot), bwd(dk_tot)],
        out_specs=pl.BlockSpec(shape, lambda i: (0,) * len(shape)),
        scratch_shapes=[pltpu.VMEM(shape, F32)],
        compiler_params=pltpu.CompilerParams(dimension_semantics=("arbitrary",),
                                             vmem_limit_bytes=VMEM_LIMIT),
        name="gla_ctx_state",
    )(k, v, lgf, k, v, lgb)


def _scan_kernel(qf_ref, kf_ref, vf_ref, lgf_ref, qb_ref, kb_ref, vb_ref, lgb_ref, s0_ref,
                 of_ref, ob_ref, st_scr, q32_scr, k32_scr, v32_scr, b_scr, o_scr,
                 *, dk, dv):
    t = pl.program_id(0)
    bsz, c = qf_ref.shape[0], SCAN_CHUNK
    n_sub = qf_ref.shape[1] // c

    def rows(di, sub):
        first = (sub if di == 0 else n_sub - 1 - sub) * c
        return slice(first, first + c)

    @pl.when(t == 0)
    def _():
        st_scr[...] = s0_ref[...]

    masks = _scan_masks(c)
    dirs = ((qf_ref, kf_ref, vf_ref, lgf_ref, of_ref) + masks[0],
            (qb_ref, kb_ref, vb_ref, lgb_ref, ob_ref) + masks[1])

    worst = None
    for lg_ref in (lgf_ref, lgb_ref):
        for sub in range(n_sub):
            w = jnp.max(-jnp.sum(lg_ref[:, sub * c:(sub + 1) * c, :], axis=1))
            worst = w if worst is None else jnp.maximum(worst, w)

    cums = {}

    def cumulate(bi, di, sub):
        lg_ref, keep, end = dirs[di][3], dirs[di][5], dirs[di][6]
        cums[bi, di, sub] = _chunk_cumsum(lg_ref[bi, rows(di, sub), :], keep, end)

    def load(bi, di, h, sub):
        q_ref, k_ref, v_ref = dirs[di][:3]
        b, tot = cums[bi, di, sub]
        sl = slice(h * dk, (h + 1) * dk)
        return (q_ref[bi, rows(di, sub), sl].astype(F32), k_ref[bi, rows(di, sub), sl].astype(F32),
                v_ref[bi, rows(di, sub), h * dv:(h + 1) * dv], b[:, sl], tot[:, sl])

    def decayed_state(bi, di, h, tot):
        return _decay_rows(st_scr[bi, di, h], tot)

    def fast_step():
        chains = [(bi, di, h, sub) for sub in range(n_sub) for bi in range(bsz)
                  for di in range(2) for h in range(GLA_HEADS)]
        live = {}

        def factors(n):
            q, k, v, bh, tot = load(*chains[n])
            half = 0.5 * tot
            eh = jnp.exp(half)
            qe = q * jnp.exp(bh - half)
            ke = k * jnp.exp(half - bh)
            live[n] = dict(v=v, tot=tot, qe=qe.astype(BF16), ke=ke.astype(BF16),
                           qd=(qe * eh).astype(BF16), kd=(ke * eh).astype(BF16))

        def scores(n):
            live[n]["a"] = _dot_nt(live[n].pop("qe"), live[n].pop("ke"))

        def outputs(n):
            bi, di, h, _ = chains[n]
            c_ = live[n]
            a = jnp.where(dirs[di][5], c_.pop("a"), 0.0).astype(BF16)
            lhs = jnp.concatenate([a, c_.pop("qd")], axis=1)
            rhs = jnp.concatenate([c_["v"], st_scr[bi, di, h].astype(BF16)], axis=0)
            c_["o"] = _dot(lhs, rhs)
            c_["u"] = _dot_tn(c_.pop("kd"), c_.pop("v"))

        def finish(n):
            bi, di, h, sub = chains[n]
            c_ = live.pop(n)
            o_ref = dirs[di][4]
            o_ref[bi, rows(di, sub), h * dv:(h + 1) * dv] = c_["o"].astype(o_ref.dtype)
            st_scr[bi, di, h] = decayed_state(bi, di, h, c_["tot"]) + c_["u"]

        def cumulate_for(n):
            bi, di, _, sub = chains[n]
            cumulate(bi, di, sub)

        phases = ((factors, 0), (scores, 1), (outputs, 3), (finish, 5))
        cumulate_for(0)
        for i in range(len(chains) + phases[-1][1]):
            ahead = i + GLA_HEADS
            if i % GLA_HEADS == 0 and ahead < len(chains):
                cumulate_for(ahead)
            for phase, lag in phases:
                if 0 <= i - lag < len(chains):
                    phase(i - lag)

    def safe_step():
        ridx = lax.broadcasted_iota(jnp.int32, (c, 1), 0)
        groups = [(s, b, d) for s in range(n_sub) for b in range(bsz) for d in range(2)]
        for sub, bi, di in groups:
            o_ref = dirs[di][4]
            cumulate(bi, di, sub)
            for h in range(GLA_HEADS):
                q, k, v, bh, tot = load(bi, di, h, sub)
                q32_scr[...] = q
                k32_scr[...] = k
                v32_scr[...] = v.astype(F32)
                b_scr[...] = bh
                o_scr[...] = _dot((q * jnp.exp(bh)).astype(BF16), st_scr[bi, di, h].astype(BF16))
                st_scr[bi, di, h] = (decayed_state(bi, di, h, tot)
                                     + _dot_tn((k * jnp.exp(tot - bh)).astype(BF16), v))

                def body(j, carry):
                    bj = b_scr[pl.ds(j, 1), :]
                    valid = (ridx >= j) if di == 0 else (ridx <= j)
                    e = jnp.exp(jnp.where(valid, b_scr[...] - bj, -jnp.inf))
                    w = jnp.sum(q32_scr[...] * e * k32_scr[pl.ds(j, 1), :], axis=-1, keepdims=True)
                    o_scr[...] += w * v32_scr[pl.ds(j, 1), :]
                    return carry

                lax.fori_loop(0, c, body, 0)
                o_ref[bi, rows(di, sub), h * dv:(h + 1) * dv] = o_scr[...].astype(o_ref.dtype)

    lax.cond(worst <= SAFE_DECAY, fast_step, safe_step)


def _scan_call(q, k, v, lgf, lgb, s0):
    bsz, t, dk_tot = q.shape
    dv_tot = v.shape[-1]
    dk, dv = dk_tot // GLA_HEADS, dv_tot // GLA_HEADS
    c = SCAN_CHUNK
    tile = SCAN_TILE
    nt = t // tile
    assert t % tile == 0 and tile % c == 0 and dv % dk == 0
    fwd = lambda w: pl.BlockSpec((bsz, tile, w), lambda i: (0, i, 0))
    bwd = lambda w: pl.BlockSpec((bsz, tile, w), lambda i: (0, nt - 1 - i, 0))
    st_spec = pl.BlockSpec(s0.shape, lambda i: (0,) * s0.ndim)
    kern = functools.partial(_scan_kernel, dk=dk, dv=dv)
    return pl.pallas_call(
        kern,
        out_shape=[jax.ShapeDtypeStruct((bsz, t, dv_tot), BF16),
                   jax.ShapeDtypeStruct((bsz, t, dv_tot), BF16)],
        grid=(nt,),
        in_specs=[fwd(dk_tot), fwd(dk_tot), fwd(dv_tot), fwd(dk_tot),
                  bwd(dk_tot), bwd(dk_tot), bwd(dv_tot), bwd(dk_tot), st_spec],
        out_specs=[fwd(dv_tot), bwd(dv_tot)],
        scratch_shapes=[pltpu.VMEM(s0.shape, F32),
                        pltpu.VMEM((c, dk), F32), pltpu.VMEM((c, dk), F32),
                        pltpu.VMEM((c, dv), F32), pltpu.VMEM((c, dk), F32),
                        pltpu.VMEM((c, dv), F32)],
        compiler_params=pltpu.CompilerParams(dimension_semantics=("arbitrary",),
                                             vmem_limit_bytes=VMEM_LIMIT),
        name="gla_scan",
    )(q, k, v, lgf, q, k, v, lgb, s0)


def _tail_kernel(x_ref, row_ref, col_ref, ma_ref, gb_ref, of_ref, ob_ref, sr_ref,
                 g1_ref, sh2_ref, sc2_ref, g2_ref, og_ref, n2_ref, fg_ref,
                 pb_ref, wo_ref, w1_ref, w2_ref, out_ref, *, dv):
    xp = _load_tokens(x_ref, row_ref, col_ref)
    mine = pl.ds(pl.program_id(0), 1)
    o = of_ref[0].astype(F32) + ob_ref[0].astype(F32)
    heads = []
    for h in range(GLA_HEADS):
        sl = slice(h * dv, (h + 1) * dv)
        heads.append(_rms(o[:, sl], og_ref[:, sl]))
    yb = (jnp.concatenate(heads, axis=-1) * sr_ref[0].astype(F32)).astype(BF16)
    merged = ma_ref[0].astype(F32) + gb_ref[0].astype(F32) * _dot(yb, pb_ref[...])
    x1 = xp + g1_ref[mine, :] * _dot(merged.astype(BF16), wo_ref[...])
    h2 = (_rms(x1, n2_ref[...]) * (1.0 + sc2_ref[mine, :]) + sh2_ref[mine, :]).astype(BF16)
    acc = jnp.zeros_like(x1)
    d_ff = w1_ref.shape[1]
    for j in range(d_ff // FF_COL_TILE):
        sl = slice(j * FF_COL_TILE, (j + 1) * FF_COL_TILE)
        f = jnp.maximum(_dot(h2, w1_ref[:, sl]), 0.0)
        acc = acc + _dot((f * f).astype(BF16), w2_ref[sl, :])
    x2 = x1 + g2_ref[mine, :] * acc
    out_ref[0] = _rms(x2, fg_ref[...])


def _tail_call(x4, pos, acts, mod, consts):
    bsz, nrow, gw_, d = x4.shape
    t = nrow * gw_
    tm = TOKEN_TILE
    rows_per_tile = tm // GRID_W
    rowemb, colemb = pos
    dv = acts[2].shape[-1] // GLA_HEADS
    tok = lambda w: pl.BlockSpec((1, tm, w), lambda b, i: (b, i, 0))
    mod_names = ("gate1", "shift2", "scale2", "gate2")
    in_specs = ([pl.BlockSpec((1, rows_per_tile, GRID_W, d), lambda b, i: (b, i, 0, 0)),
                 pl.BlockSpec((rows_per_tile, 1, d // 2), lambda b, i: (i, 0, 0)),
                 _const_spec(colemb.shape)]
                + [tok(a.shape[-1]) for a in acts]
                + [_mod_spec(mod, name) for name in mod_names]
                + [_const_spec(a.shape) for a in consts])
    return pl.pallas_call(
        functools.partial(_tail_kernel, dv=dv),
        out_shape=jax.ShapeDtypeStruct((bsz, t, d), F32),
        grid=(bsz, t // tm),
        in_specs=in_specs,
        out_specs=tok(d),
        compiler_params=pltpu.CompilerParams(dimension_semantics=("arbitrary", "arbitrary"),
                                             vmem_limit_bytes=VMEM_LIMIT),
        name="tail",
    )(x4, rowemb, colemb, *acts, *[mod] * len(mod_names), *consts)


def _pos_tables(t, d):
    quarter = d // 4
    one, base = np.float32(1.0), np.float32(10000.0)
    freqs = one / (base ** (np.arange(quarter, dtype=np.float32) / np.float32(quarter)))

    def emb(n):
        a = np.arange(n, dtype=np.float32)[:, None] * freqs[None, :]
        return np.concatenate([np.sin(a), np.cos(a)], -1).astype(np.float32)

    return jnp.asarray(emb(t // GRID_W)[:, None, :]), jnp.asarray(emb(GRID_W)[None, :, :])


def kernel(x, c, ctx, c_ctx, w_ada, b_ada, norm1_g, w_in, lnv_g, lnv_b, w_s, b_s, w_dec2, b_dec,
           onorm_g, proj_a, proj_b, w_out, norm2_g, w_ff1, w_ff2, final_g):
    bsz, t, d = x.shape
    depth = w_in.shape[0]
    assert depth == 1, "single-layer problem: the context stream only feeds the scan states"
    d_a = proj_a.shape[1]
    dv_tot = proj_b.shape[1]
    dk_tot = w_dec2.shape[-1]
    seg = _segment_rows(d, d_a, dk_tot, dv_tot, w_dec2.shape[2])

    c_rows = jnp.zeros((MOD_ROWS, d), F32).at[:bsz].set(c).at[bsz].set(c_ctx)
    mod = _mod_call(c_rows, w_ada[0], b_ada[0])

    row = lambda a: a.reshape(1, -1)
    g1n = row(norm1_g[0])
    w_in_t = jnp.swapaxes(w_in[0], 0, 1)
    wd2, bd = w_dec2[0], b_dec[0][:, None, :]

    ck, cv, clgf, clgb = _ctx_proj_call(ctx, mod, bsz, g1n, w_in_t, wd2, bd, seg, dk_tot, dv_tot)
    s_ctx = _state_call(ck, cv, clgf, clgb)

    pos = _pos_tables(t, d)
    x4 = x.reshape(bsz, t // GRID_W, GRID_W, d)
    consts = (g1n, wd2, bd, row(lnv_g[0]), row(lnv_b[0]), w_s[0], b_s[0][:, :, None])
    ma, gb, q, k, v, sr, lgf, lgb, *tail_weights = _inproj_call(
        x4, pos, mod, consts, (w_in_t, proj_a[0]),
        (proj_b[0], w_out[0], w_ff1[0], w_ff2[0]), seg, d_a, dk_tot, dv_tot)
    o_f, o_b = _scan_call(q, k, v, lgf, lgb, s_ctx)

    return _tail_call(x4, pos, (ma, gb, o_f, o_b, sr), mod,
                      (row(onorm_g[0]), row(norm2_g[0]), row(final_g), *tail_weights))
```

```python
import functools
import math

import jax
import jax.numpy as jnp
import numpy as np
from jax import lax
from jax.experimental import pallas as pl
from jax.experimental.pallas import tpu as pltpu

F32 = jnp.float32
BF16 = jnp.bfloat16

GRID_W = 64
GLA_HEADS = 4
GATE_TAU = 16.0
EPS = 1e-6

TOKEN_TILE = 512
SCAN_CHUNK = 128
SCAN_TILE = 512
MOD_ROWS = 8
MOD_COL_TILE = 1024
PROJ_COL_TILE = 512
PREPARE_PARTS = 4
STAGE_LOOKAHEAD = 1
WEIGHT_STAGE_ROWS = 512
WEIGHT_STAGE_SLOTS = 2
BF16_SUBLANES = 16
FF_COL_TILE = 1024
SAFE_DECAY = 60.0
VMEM_LIMIT = 56 * 1024 * 1024


def _sigmoid(x):
    return 1.0 / (1.0 + jnp.exp(-x))


def _gelu_tanh(x):
    c = math.sqrt(2.0 / math.pi)
    return 0.5 * x * (1.0 + jnp.tanh(c * (x + 0.044715 * (x * x * x))))


def _log_sigmoid(z):
    return jnp.minimum(z, 0.0) - jnp.log(1.0 + jnp.exp(-jnp.abs(z)))


def _rms(x, g):
    return x * lax.rsqrt(jnp.mean(x * x, axis=-1, keepdims=True) + EPS) * g


def _split_bf16(a):
    hi = a.astype(BF16)
    lo = (a - hi.astype(F32)).astype(BF16)
    return hi, lo


def _dot(a, b):
    return jnp.dot(a, b, preferred_element_type=F32)


def _dot_nt(a, b):
    return lax.dot_general(a, b, (((1,), (1,)), ((), ())), preferred_element_type=F32)


def _dot_tn(a, b):
    return lax.dot_general(a, b, (((0,), (0,)), ((), ())), preferred_element_type=F32)


def _const_spec(shape):
    nd = len(shape)
    return pl.BlockSpec(shape, lambda *_: (0,) * nd, pipeline_mode=pl.Buffered(1))


MOD_COLUMNS = ("shift1", "scale1", "gate1", "shift2", "scale2", "gate2")


def _mod_spec(mod, name):
    d = mod.shape[1] // len(MOD_COLUMNS)
    j = MOD_COLUMNS.index(name)
    return pl.BlockSpec((mod.shape[0], d), lambda *_: (0, j), pipeline_mode=pl.Buffered(1))


def _load_as_bf16(src_ref, dst_ref, stage_ref, sem_ref, rows):
    n_rows, cols = src_ref.shape
    slots = stage_ref.shape[0]
    assert n_rows % rows == 0 and rows <= stage_ref.shape[1] and cols == stage_ref.shape[2]
    n = n_rows // rows

    def copy(j, slot):
        return pltpu.make_async_copy(src_ref.at[pl.ds(j * rows, rows), :],
                                     stage_ref.at[slot, pl.ds(0, rows), :], sem_ref.at[slot])

    for j in range(min(slots - 1, n)):
        copy(j, j).start()

    def body(j, carry):
        ahead = j + slots - 1

        @pl.when(ahead < n)
        def _():
            copy(ahead, ahead % slots).start()

        copy(j, j % slots).wait()
        first = pl.multiple_of(j * rows, rows)
        dst_ref[pl.ds(first, rows), :] = stage_ref[j % slots, pl.ds(0, rows), :].astype(BF16)
        return carry

    lax.fori_loop(0, n, body, 0)


def _stage_rows(n_rows):
    return max(r for r in range(BF16_SUBLANES, WEIGHT_STAGE_ROWS + 1, BF16_SUBLANES)
               if n_rows % r == 0)


def _mod_kernel(c_ref, cc_ref, w_ref, b_ref, o_ref):
    rows, bsz = o_ref.shape[0], c_ref.shape[0]
    row_id = lax.broadcasted_iota(jnp.int32, (rows, c_ref.shape[1]), 0)
    c = jnp.where(row_id == bsz, cc_ref[...], 0.0)
    for j in range(bsz):
        c = jnp.where(row_id == j, c_ref[j:j + 1, :], c)
    a_hi, a_lo = _split_bf16(c * _sigmoid(c))
    w_hi, w_lo = _split_bf16(w_ref[...])
    both = _dot(jnp.concatenate([a_hi, a_lo], axis=0), w_hi)
    o_ref[...] = both[:rows] + both[rows:] + _dot(a_hi, w_lo) + b_ref[...]


def _mod_call(c, c_ctx, w_ada, b_ada):
    d, n = w_ada.shape
    assert c.shape[0] < MOD_ROWS
    return pl.pallas_call(
        _mod_kernel,
        out_shape=jax.ShapeDtypeStruct((MOD_ROWS, n), F32),
        grid=(n // MOD_COL_TILE,),
        in_specs=[pl.BlockSpec(c.shape, lambda j: (0, 0)),
                  pl.BlockSpec((1, d), lambda j: (0, 0)),
                  pl.BlockSpec((d, MOD_COL_TILE), lambda j: (0, j)),
                  pl.BlockSpec((1, MOD_COL_TILE), lambda j: (0, j))],
        out_specs=pl.BlockSpec((MOD_ROWS, MOD_COL_TILE), lambda j: (0, j)),
        compiler_params=pltpu.CompilerParams(dimension_semantics=("arbitrary",)),
        name="adaln_mod",
    )(c, c_ctx.reshape(1, d), w_ada, b_ada.reshape(1, n))


def _load_tokens(x_ref, row_ref, col_ref):
    x = x_ref[0]
    half = x.shape[-1] // 2
    x = jnp.concatenate([x[..., :half] + row_ref[...], x[..., half:] + col_ref[...]], axis=-1)
    return x.reshape(x.shape[0] * x.shape[1], x.shape[2])


def _segment_rows(d, d_a, dk_tot, dv_tot, rank):
    names = ("u", "va", "q", "k", "v", "r", "dec", "ga", "gb")
    widths = (d_a, d_a, dk_tot, dk_tot, dv_tot, dv_tot, 2 * rank, d, d)
    offsets, r0 = {}, 0
    for name, w in zip(names, widths):
        offsets[name] = r0
        r0 += w
    return offsets


def _store_log_decay(dec, wdec2_ref, bdec_ref, di, out_ref):
    rank = wdec2_ref.shape[1]
    z = _dot(dec[:, di * rank:(di + 1) * rank].astype(BF16),
             wdec2_ref[di].astype(BF16)) + bdec_ref[di]
    out_ref[0] = (_log_sigmoid(z) * (1.0 / GATE_TAU)).astype(out_ref.dtype)


def _ctx_proj_kernel(x_ref, sh_ref, sc_ref, g1_ref, wk_ref, wv_ref, wd_ref, wdec2_ref, bdec_ref,
                     k_ref, v_ref, lgf_ref, lgb_ref, *, mod_row):
    hb = (_rms(x_ref[0], g1_ref[...]) * (1.0 + sc_ref[mod_row:mod_row + 1, :])
          + sh_ref[mod_row:mod_row + 1, :]).astype(BF16)
    proj = lambda w_ref: _dot_nt(hb, w_ref[...].astype(BF16))
    dec = proj(wd_ref)
    for di, out_ref in enumerate((lgf_ref, lgb_ref)):
        _store_log_decay(dec, wdec2_ref, bdec_ref, di, out_ref)
    k_ref[0] = proj(wk_ref).astype(BF16)
    v_ref[0] = proj(wv_ref).astype(BF16)


def _ctx_proj_call(ctx, mod, mod_row, g1, w_in_t, w_dec2, b_dec, seg, dk_tot, dv_tot):
    bsz, t, d = ctx.shape
    tm = min(TOKEN_TILE, t)
    assert t % tm == 0
    tok = lambda w: pl.BlockSpec((1, tm, w), lambda b, i: (b, i, 0))

    def rows_spec(r0, width):
        assert r0 % width == 0
        return pl.BlockSpec((width, d), lambda b, i: (r0 // width, 0),
                            pipeline_mode=pl.Buffered(1))

    rank = w_dec2.shape[1]
    tail_consts = [w_dec2, b_dec]
    widths = [dk_tot, dv_tot, dk_tot, dk_tot]
    dtypes = [BF16] * 4
    return pl.pallas_call(
        functools.partial(_ctx_proj_kernel, mod_row=mod_row),
        out_shape=[jax.ShapeDtypeStruct((bsz, t, w), dt) for w, dt in zip(widths, dtypes)],
        grid=(bsz, t // tm),
        in_specs=([tok(d), _mod_spec(mod, "shift1"), _mod_spec(mod, "scale1"),
                   _const_spec(g1.shape)]
                  + [rows_spec(seg["k"], dk_tot), rows_spec(seg["v"], dv_tot),
                     rows_spec(seg["dec"], 2 * rank)]
                  + [_const_spec(a.shape) for a in tail_consts]),
        out_specs=[tok(w) for w in widths],
        compiler_params=pltpu.CompilerParams(dimension_semantics=("arbitrary", "arbitrary"),
                                             vmem_limit_bytes=VMEM_LIMIT),
        name="ctx_proj",
    )(ctx, mod, mod, g1, w_in_t, w_in_t, w_in_t, *tail_consts)


def _inproj_kernel(x_ref, row_ref, col_ref, sh_ref, sc_ref, g1_ref, wdec2_ref, bdec_ref,
                   lng_ref, lnb_ref, ws_ref, bs_ref, wt_hbm, pa_hbm, *rest,
                   seg, d_a, dk_tot, dv_tot, n_later, nt):
    later_f32, rest = rest[:n_later], rest[n_later:]
    (ma_ref, gb_ref, q_ref, k_ref, v_ref, sr_ref, lgf_ref, lgb_ref), rest = rest[:8], rest[8:]
    later_bf16, (ya_scr, hb_scr, wt_ref, pa_ref, stage_ref, sem_ref) = rest[:n_later], rest[n_later:]
    step = pl.program_id(0)
    _, tm, d = hb_scr.shape

    grid_rows = x_ref.shape[1]

    def prepare_next(part):
        n = grid_rows // PREPARE_PARTS
        xp = _load_tokens(x_ref.at[:, pl.ds(part * n, n)], row_ref.at[pl.ds(part * n, n)], col_ref)
        batch = jnp.minimum(step, pl.num_programs(0) - 2) // nt
        mine = pl.ds(batch, 1)
        hb = _rms(xp, g1_ref[...]) * (1.0 + sc_ref[mine, :]) + sh_ref[mine, :]
        hb_scr[step % 2, pl.ds(part * n * GRID_W, n * GRID_W), :] = hb.astype(BF16)

    def warm_up():
        for src, dst in ((wt_hbm, wt_ref), (pa_hbm, pa_ref)):
            _load_as_bf16(src, dst, stage_ref, sem_ref, _stage_rows(src.shape[0]))
        for part in range(PREPARE_PARTS):
            prepare_next(part)

    lax.cond(step == 0, warm_up,
             functools.partial(_project_tile, prepare_next, rest_refs=(
                 wdec2_ref, bdec_ref, lng_ref, lnb_ref, ws_ref, bs_ref, later_f32, later_bf16,
                 ma_ref, gb_ref, q_ref, k_ref, v_ref, sr_ref, lgf_ref, lgb_ref,
                 ya_scr, hb_scr, wt_ref, pa_ref),
                 seg=seg, d_a=d_a, dk_tot=dk_tot, dv_tot=dv_tot))


def _project_tile(prepare_next, *, rest_refs, seg, d_a, dk_tot, dv_tot):
    (wdec2_ref, bdec_ref, lng_ref, lnb_ref, ws_ref, bs_ref, later_f32, later_bf16,
     ma_ref, gb_ref, q_ref, k_ref, v_ref, sr_ref, lgf_ref, lgb_ref,
     ya_scr, hb_scr, wt_ref, pa_ref) = rest_refs
    _, tm, d = hb_scr.shape
    ready = (pl.program_id(0) + 1) % 2

    for src, dst in zip(later_f32, later_bf16):
        dst[...] = src[...].astype(BF16)

    def proj(r0, width):
        return _dot_nt(hb_scr[ready], wt_ref[r0:r0 + width, :])

    r_u, r_va, r_q, r_k, r_v, r_r = (seg[n] for n in ("u", "va", "q", "k", "v", "r"))
    r_ga, r_gb = seg["ga"], seg["gb"]
    rank = wdec2_ref.shape[1]
    groups, chunk_a = ws_ref.shape[0], ws_ref.shape[1]
    dk = dk_tot // GLA_HEADS
    gw = d_a // groups
    blk = PROJ_COL_TILE
    cols = lambda j: slice(j * blk, (j + 1) * blk)
    gu, gv, vn = {}, {}, {}

    stage = {}

    def store(out_ref, j, fn):
        def ep(y):
            out_ref[0, :, cols(j)] = fn(y).astype(out_ref.dtype)
        return ep

    def keep(table, j, fn):
        def ep(y):
            table[j] = fn(y)
        return ep

    def add(name, matmul, epilogue):
        stage[name] = (matmul, epilogue)
        return name

    def ep_dec(dec):
        for di, out_ref in enumerate((lgf_ref, lgb_ref)):
            _store_log_decay(dec, wdec2_ref, bdec_ref, di, out_ref)
    s_dec = add("dec", lambda: proj(seg["dec"], 2 * rank), ep_dec)
    s_u = [add(f"u{j}", lambda j=j: proj(r_u + j * blk, blk), keep(gu, j, _gelu_tanh))
           for j in range(d_a // blk)]
    s_va = [add(f"va{j}", lambda j=j: proj(r_va + j * blk, blk), keep(gv, j, _gelu_tanh))
            for j in range(d_a // blk)]

    def ep_q_and_layernorm(y):
        q_ref[0] = (y * dk ** -0.5).astype(BF16)
        nb = d_a // blk
        mu = sum(jnp.sum(gv[j], axis=-1, keepdims=True) for j in range(nb)) * (1.0 / d_a)
        cen = [gv[j] - mu for j in range(nb)]
        var = sum(jnp.sum(c * c, axis=-1, keepdims=True) for c in cen) * (1.0 / d_a)
        inv = lax.rsqrt(var + EPS)
        for j in range(nb):
            vn[j] = (cen[j] * inv * lng_ref[:, cols(j)] + lnb_ref[:, cols(j)]).astype(BF16)
    s_q = add("q", lambda: proj(r_q, dk_tot), ep_q_and_layernorm)
    def ep_k(y):
        k_ref[0] = y.astype(BF16)
    s_k = add("k", lambda: proj(r_k, dk_tot), ep_k)
    s_r = [add(f"r{j}", lambda j=j: proj(r_r + j * blk, blk),
               store(sr_ref, j, lambda y: y * _sigmoid(y))) for j in range(dv_tot // blk)]
    s_v = [add(f"v{j}", lambda j=j: proj(r_v + j * blk, blk), store(v_ref, j, lambda y: y))
           for j in range(dv_tot // blk)]

    def mm_gate(ci, g):
        r0, c0 = ci * chunk_a, g * gw
        return _dot(ws_ref[g].astype(BF16),
                    vn[c0 // blk][r0:r0 + chunk_a, c0 % blk:c0 % blk + gw])

    def ep_gate(ci, g):
        r0, c0 = ci * chunk_a, g * gw

        def ep(s):
            u = gu[c0 // blk][r0:r0 + chunk_a, c0 % blk:c0 % blk + gw]
            ya_scr[r0:r0 + chunk_a, c0:c0 + gw] = (u * (s + bs_ref[g])).astype(BF16)
        return ep
    s_mix = [add(f"mix{ci}_{g}", lambda ci=ci, g=g: mm_gate(ci, g), ep_gate(ci, g))
             for ci in range(tm // chunk_a) for g in range(groups)]
    s_gb = [add(f"gb{j}", lambda j=j: proj(r_gb + j * blk, blk), store(gb_ref, j, _sigmoid))
            for j in range(d // blk)]
    s_ga = [add(f"ga{j}", lambda j=j: (proj(r_ga + j * blk, blk),
                                       _dot(ya_scr[...], pa_ref[:, cols(j)])),
                store(ma_ref, j, lambda y: _sigmoid(y[0]) * y[1])) for j in range(d // blk)]

    order = ([s_dec] + s_u + s_va + [s_q, s_k] + s_r + s_gb + s_mix + s_v[:1] + s_ga + s_v[1:])
    assert sorted(order) == sorted(stage)
    extra = {}
    for part, name in enumerate(order[-PREPARE_PARTS - 1:-1]):
        extra.setdefault(name, []).append(functools.partial(prepare_next, part))
    results = {}
    for i in range(len(order) + STAGE_LOOKAHEAD):
        if i < len(order):
            results[order[i]] = stage[order[i]][0]()
        if i >= STAGE_LOOKAHEAD:
            done = order[i - STAGE_LOOKAHEAD]
            stage[done][1](results.pop(done))
            for work in extra.get(done, ()):
                work()


def _inproj_call(x4, pos, mod, consts, weights, later_weights, seg, d_a, dk_tot, dv_tot):
    bsz, nrow, gw_, d = x4.shape
    t = nrow * gw_
    tm = TOKEN_TILE
    nt = t // tm
    rows_per_tile = tm // GRID_W
    rowemb, colemb = pos
    chunk_a = consts[5].shape[1]
    assert tm % chunk_a == 0 and tm % GRID_W == 0 and t % tm == 0

    n_tiles = bsz * nt
    preparing = lambda s: jnp.minimum(s, n_tiles - 1)
    projecting = lambda s: jnp.maximum(s - 1, 0)

    def slab(w):
        rows = w.shape[0] // n_tiles
        assert w.shape[0] % n_tiles == 0 and rows % BF16_SUBLANES == 0
        return pl.BlockSpec((rows, w.shape[1]), lambda s: (projecting(s), 0))

    tok = lambda w: pl.BlockSpec((1, tm, w), lambda s: (projecting(s) // nt, projecting(s) % nt, 0))
    in_specs = ([pl.BlockSpec((1, rows_per_tile, GRID_W, d),
                              lambda s: (preparing(s) // nt, preparing(s) % nt, 0, 0)),
                 pl.BlockSpec((rows_per_tile, 1, d // 2), lambda s: (preparing(s) % nt, 0, 0)),
                 _const_spec(colemb.shape), _mod_spec(mod, "shift1"), _mod_spec(mod, "scale1")]
                + [_const_spec(a.shape) for a in consts]
                + [pl.BlockSpec(memory_space=pl.ANY)] * len(weights)
                + [slab(w) for w in later_weights])
    widths = [d, d, dk_tot, dk_tot, dv_tot, dv_tot, dk_tot, dk_tot]
    dtypes = [BF16] * 8
    kern = functools.partial(_inproj_kernel, seg=seg, d_a=d_a, dk_tot=dk_tot, dv_tot=dv_tot,
                             n_later=len(later_weights), nt=nt)
    return pl.pallas_call(
        kern,
        out_shape=([jax.ShapeDtypeStruct((bsz, t, w), dt) for w, dt in zip(widths, dtypes)]
                   + [jax.ShapeDtypeStruct(w.shape, BF16) for w in later_weights]),
        grid=(n_tiles + 1,),
        in_specs=in_specs,
        out_specs=[tok(w) for w in widths] + [slab(w) for w in later_weights],
        scratch_shapes=([pltpu.VMEM((tm, d_a), BF16), pltpu.VMEM((2, tm, d), BF16)]
                        + [pltpu.VMEM(w.shape, BF16) for w in weights]
                        + [pltpu.VMEM((WEIGHT_STAGE_SLOTS, WEIGHT_STAGE_ROWS, d), F32),
                           pltpu.SemaphoreType.DMA((WEIGHT_STAGE_SLOTS,))]),
        compiler_params=pltpu.CompilerParams(dimension_semantics=("arbitrary",),
                                             vmem_limit_bytes=VMEM_LIMIT),
        name="inproj",
    )(x4, rowemb, colemb, mod, mod, *consts, *weights, *later_weights)


def _scan_masks(c):
    row = lax.broadcasted_iota(jnp.int32, (c, c), 0)
    col = lax.broadcasted_iota(jnp.int32, (c, c), 1)
    return ((row >= col, c - 1), (row <= col, 0))


def _chunk_cumsum(lg, keep, end):
    b = _dot(jnp.where(keep, 1.0, 0.0).astype(BF16), lg)
    return b, b[end:end + 1, :]


def _decay_rows(st, tot):
    dk, dv = st.shape
    dcol = jnp.transpose(jnp.broadcast_to(jnp.exp(tot), (dk, dk)))
    return st * jnp.concatenate([dcol] * (dv // dk), axis=1)


def _state_kernel(kf_ref, vf_ref, lgf_ref, kb_ref, vb_ref, lgb_ref, s_ref, st_scr, *, dk, dv):
    t = pl.program_id(0)
    bsz, c = kf_ref.shape[0], SCAN_CHUNK
    n_sub = kf_ref.shape[1] // c

    @pl.when(t == 0)
    def _():
        st_scr[...] = jnp.zeros_like(st_scr)

    masks = _scan_masks(c)
    dirs = ((kf_ref, vf_ref, lgf_ref), (kb_ref, vb_ref, lgb_ref))
    for sub in range(n_sub):
        for bi in range(bsz):
            for di, (k_ref, v_ref, lg_ref) in enumerate(dirs):
                first = (sub if di == 0 else n_sub - 1 - sub) * c
                b, tot = _chunk_cumsum(lg_ref[bi, first:first + c, :], *masks[di])
                for h in range(GLA_HEADS):
                    sl = slice(h * dk, (h + 1) * dk)
                    k = k_ref[bi, first:first + c, sl].astype(F32)
                    kd = (k * jnp.exp(tot[:, sl] - b[:, sl])).astype(BF16)
                    v = v_ref[bi, first:first + c, h * dv:(h + 1) * dv]
                    st_scr[bi, di, h] = _decay_rows(st_scr[bi, di, h], tot[:, sl]) + _dot_tn(kd, v)

    @pl.when(t == pl.num_programs(0) - 1)
    def _():
        s_ref[...] = st_scr[...]


def _state_call(k, v, lgf, lgb):
    bsz, t, dk_tot = k.shape
    dv_tot = v.shape[-1]
    dk, dv = dk_tot // GLA_HEADS, dv_tot // GLA_HEADS
    tile = min(SCAN_TILE, t)
    nt = t // tile
    assert t % tile == 0 and tile % SCAN_CHUNK == 0 and dv % dk == 0
    fwd = lambda w: pl.BlockSpec((bsz, tile, w), lambda i: (0, i, 0))
    bwd = lambda w: pl.BlockSpec((bsz, tile, w), lambda i: (0, nt - 1 - i, 0))
    shape = (bsz, 2, GLA_HEADS, dk, dv)
    return pl.pallas_call(
        functools.partial(_state_kernel, dk=dk, dv=dv),
        out_shape=jax.ShapeDtypeStruct(shape, F32),
        grid=(nt,),
        in_specs=[fwd(dk_tot), fwd(dv_tot), fwd(dk_tot), bwd(dk_tot), bwd(dv_tot), bwd(dk_tot)],
        out_specs=pl.BlockSpec(shape, lambda i: (0,) * len(shape)),
        scratch_shapes=[pltpu.VMEM(shape, F32)],
        compiler_params=pltpu.CompilerParams(dimension_semantics=("arbitrary",),
                                             vmem_limit_bytes=VMEM_LIMIT),
        name="gla_ctx_state",
    )(k, v, lgf, k, v, lgb)


def _scan_kernel(qf_ref, kf_ref, vf_ref, lgf_ref, qb_ref, kb_ref, vb_ref, lgb_ref, s0_ref,
                 of_ref, ob_ref, st_scr, q32_scr, k32_scr, v32_scr, b_scr, o_scr,
                 *, dk, dv):
    t = pl.program_id(0)
    bsz, c = qf_ref.shape[0], SCAN_CHUNK
    n_sub = qf_ref.shape[1] // c

    def rows(di, sub):
        first = (sub if di == 0 else n_sub - 1 - sub) * c
        return slice(first, first + c)

    @pl.when(t == 0)
    def _():
        st_scr[...] = s0_ref[...]

    masks = _scan_masks(c)
    dirs = ((qf_ref, kf_ref, vf_ref, lgf_ref, of_ref) + masks[0],
            (qb_ref, kb_ref, vb_ref, lgb_ref, ob_ref) + masks[1])

    worst = None
    for lg_ref in (lgf_ref, lgb_ref):
        for sub in range(n_sub):
            w = jnp.max(-jnp.sum(lg_ref[:, sub * c:(sub + 1) * c, :].astype(F32), axis=1))
            worst = w if worst is None else jnp.maximum(worst, w)

    cums = {}

    def cumulate(bi, di, sub):
        lg_ref, keep, end = dirs[di][3], dirs[di][5], dirs[di][6]
        cums[bi, di, sub] = _chunk_cumsum(lg_ref[bi, rows(di, sub), :], keep, end)

    def load(bi, di, h, sub):
        q_ref, k_ref, v_ref = dirs[di][:3]
        b, tot = cums[bi, di, sub]
        sl = slice(h * dk, (h + 1) * dk)
        return (q_ref[bi, rows(di, sub), sl].astype(F32), k_ref[bi, rows(di, sub), sl].astype(F32),
                v_ref[bi, rows(di, sub), h * dv:(h + 1) * dv], b[:, sl], tot[:, sl])

    def decayed_state(bi, di, h, tot):
        return _decay_rows(st_scr[bi, di, h], tot)

    def fast_step():
        chains = [(bi, di, h, sub) for sub in range(n_sub) for bi in range(bsz)
                  for di in range(2) for h in range(GLA_HEADS)]
        live = {}

        def factors(n):
            q, k, v, bh, tot = load(*chains[n])
            half = 0.5 * tot
            eh = jnp.exp(half)
            qe = q * jnp.exp(bh - half)
            ke = k * jnp.exp(half - bh)
            live[n] = dict(v=v, tot=tot, qe=qe.astype(BF16), ke=ke.astype(BF16),
                           qd=(qe * eh).astype(BF16), kd=(ke * eh).astype(BF16))

        def scores(n):
            live[n]["a"] = _dot_nt(live[n].pop("qe"), live[n].pop("ke"))

        def outputs(n):
            bi, di, h, _ = chains[n]
            c_ = live[n]
            a = jnp.where(dirs[di][5], c_.pop("a"), 0.0).astype(BF16)
            lhs = jnp.concatenate([a, c_.pop("qd")], axis=1)
            rhs = jnp.concatenate([c_["v"], st_scr[bi, di, h].astype(BF16)], axis=0)
            c_["o"] = _dot(lhs, rhs)
            c_["u"] = _dot_tn(c_.pop("kd"), c_.pop("v"))

        def finish(n):
            bi, di, h, sub = chains[n]
            c_ = live.pop(n)
            o_ref = dirs[di][4]
            o_ref[bi, rows(di, sub), h * dv:(h + 1) * dv] = c_["o"].astype(o_ref.dtype)
            st_scr[bi, di, h] = decayed_state(bi, di, h, c_["tot"]) + c_["u"]

        def cumulate_for(n):
            bi, di, _, sub = chains[n]
            cumulate(bi, di, sub)

        phases = ((factors, 0), (scores, 1), (outputs, 3), (finish, 5))
        cumulate_for(0)
        for i in range(len(chains) + phases[-1][1]):
            ahead = i + GLA_HEADS
            if i % GLA_HEADS == 0 and ahead < len(chains):
                cumulate_for(ahead)
            for phase, lag in phases:
                if 0 <= i - lag < len(chains):
                    phase(i - lag)

    def safe_step():
        ridx = lax.broadcasted_iota(jnp.int32, (c, 1), 0)
        groups = [(s, b, d) for s in range(n_sub) for b in range(bsz) for d in range(2)]
        for sub, bi, di in groups:
            o_ref = dirs[di][4]
            cumulate(bi, di, sub)
            for h in range(GLA_HEADS):
                q, k, v, bh, tot = load(bi, di, h, sub)
                q32_scr[...] = q
                k32_scr[...] = k
                v32_scr[...] = v.astype(F32)
                b_scr[...] = bh
                o_scr[...] = _dot((q * jnp.exp(bh)).astype(BF16), st_scr[bi, di, h].astype(BF16))
                st_scr[bi, di, h] = (decayed_state(bi, di, h, tot)
                                     + _dot_tn((k * jnp.exp(tot - bh)).astype(BF16), v))

                def body(j, carry):
                    bj = b_scr[pl.ds(j, 1), :]
                    valid = (ridx >= j) if di == 0 else (ridx <= j)
                    e = jnp.exp(jnp.where(valid, b_scr[...] - bj, -jnp.inf))
                    w = jnp.sum(q32_scr[...] * e * k32_scr[pl.ds(j, 1), :], axis=-1, keepdims=True)
                    o_scr[...] += w * v32_scr[pl.ds(j, 1), :]
                    return carry

                lax.fori_loop(0, c, body, 0)
                o_ref[bi, rows(di, sub), h * dv:(h + 1) * dv] = o_scr[...].astype(o_ref.dtype)

    lax.cond(worst <= SAFE_DECAY, fast_step, safe_step)


def _scan_call(q, k, v, lgf, lgb, s0):
    bsz, t, dk_tot = q.shape
    dv_tot = v.shape[-1]
    dk, dv = dk_tot // GLA_HEADS, dv_tot // GLA_HEADS
    c = SCAN_CHUNK
    tile = SCAN_TILE
    nt = t // tile
    assert t % tile == 0 and tile % c == 0 and dv % dk == 0
    fwd = lambda w: pl.BlockSpec((bsz, tile, w), lambda i: (0, i, 0))
    bwd = lambda w: pl.BlockSpec((bsz, tile, w), lambda i: (0, nt - 1 - i, 0))
    st_spec = pl.BlockSpec(s0.shape, lambda i: (0,) * s0.ndim)
    kern = functools.partial(_scan_kernel, dk=dk, dv=dv)
    return pl.pallas_call(
        kern,
        out_shape=[jax.ShapeDtypeStruct((bsz, t, dv_tot), BF16),
                   jax.ShapeDtypeStruct((bsz, t, dv_tot), BF16)],
        grid=(nt,),
        in_specs=[fwd(dk_tot), fwd(dk_tot), fwd(dv_tot), fwd(dk_tot),
                  bwd(dk_tot), bwd(dk_tot), bwd(dv_tot), bwd(dk_tot), st_spec],
        out_specs=[fwd(dv_tot), bwd(dv_tot)],
        scratch_shapes=[pltpu.VMEM(s0.shape, F32),
                        pltpu.VMEM((c, dk), F32), pltpu.VMEM((c, dk), F32),
                        pltpu.VMEM((c, dv), F32), pltpu.VMEM((c, dk), F32),
                        pltpu.VMEM((c, dv), F32)],
        compiler_params=pltpu.CompilerParams(dimension_semantics=("arbitrary",),
                                             vmem_limit_bytes=VMEM_LIMIT),
        name="gla_scan",
    )(q, k, v, lgf, q, k, v, lgb, s0)


def _tail_kernel(x_ref, row_ref, col_ref, ma_ref, gb_ref, of_ref, ob_ref, sr_ref,
                 g1_ref, sh2_ref, sc2_ref, g2_ref, og_ref, n2_ref, fg_ref,
                 pb_ref, wo_ref, w1_ref, w2_ref, out_ref, *, dv):
    xp = _load_tokens(x_ref, row_ref, col_ref)
    mine = pl.ds(pl.program_id(0), 1)
    o = of_ref[0].astype(F32) + ob_ref[0].astype(F32)
    heads = []
    for h in range(GLA_HEADS):
        sl = slice(h * dv, (h + 1) * dv)
        heads.append(_rms(o[:, sl], og_ref[:, sl]))
    yb = (jnp.concatenate(heads, axis=-1) * sr_ref[0].astype(F32)).astype(BF16)
    merged = ma_ref[0].astype(F32) + gb_ref[0].astype(F32) * _dot(yb, pb_ref[...])
    x1 = xp + g1_ref[mine, :] * _dot(merged.astype(BF16), wo_ref[...])
    h2 = (_rms(x1, n2_ref[...]) * (1.0 + sc2_ref[mine, :]) + sh2_ref[mine, :]).astype(BF16)
    acc = jnp.zeros_like(x1)
    d_ff = w1_ref.shape[1]
    for j in range(d_ff // FF_COL_TILE):
        sl = slice(j * FF_COL_TILE, (j + 1) * FF_COL_TILE)
        f = jnp.maximum(_dot(h2, w1_ref[:, sl]), 0.0)
        acc = acc + _dot((f * f).astype(BF16), w2_ref[sl, :])
    x2 = x1 + g2_ref[mine, :] * acc
    out_ref[0] = _rms(x2, fg_ref[...])


def _tail_call(x4, pos, acts, mod, consts):
    bsz, nrow, gw_, d = x4.shape
    t = nrow * gw_
    tm = TOKEN_TILE
    rows_per_tile = tm // GRID_W
    rowemb, colemb = pos
    dv = acts[2].shape[-1] // GLA_HEADS
    tok = lambda w: pl.BlockSpec((1, tm, w), lambda b, i: (b, i, 0))
    mod_names = ("gate1", "shift2", "scale2", "gate2")
    in_specs = ([pl.BlockSpec((1, rows_per_tile, GRID_W, d), lambda b, i: (b, i, 0, 0)),
                 pl.BlockSpec((rows_per_tile, 1, d // 2), lambda b, i: (i, 0, 0)),
                 _const_spec(colemb.shape)]
                + [tok(a.shape[-1]) for a in acts]
                + [_mod_spec(mod, name) for name in mod_names]
                + [_const_spec(a.shape) for a in consts])
    return pl.pallas_call(
        functools.partial(_tail_kernel, dv=dv),
        out_shape=jax.ShapeDtypeStruct((bsz, t, d), F32),
        grid=(bsz, t // tm),
        in_specs=in_specs,
        out_specs=tok(d),
        compiler_params=pltpu.CompilerParams(dimension_semantics=("arbitrary", "arbitrary"),
                                             vmem_limit_bytes=VMEM_LIMIT),
        name="tail",
    )(x4, rowemb, colemb, *acts, *[mod] * len(mod_names), *consts)


def _pos_tables(t, d):
    quarter = d // 4
    one, base = np.float32(1.0), np.float32(10000.0)
    freqs = one / (base ** (np.arange(quarter, dtype=np.float32) / np.float32(quarter)))

    def emb(n):
        a = np.arange(n, dtype=np.float32)[:, None] * freqs[None, :]
        return np.concatenate([np.sin(a), np.cos(a)], -1).astype(np.float32)

    return jnp.asarray(emb(t // GRID_W)[:, None, :]), jnp.asarray(emb(GRID_W)[None, :, :])


def kernel(x, c, ctx, c_ctx, w_ada, b_ada, norm1_g, w_in, lnv_g, lnv_b, w_s, b_s, w_dec2, b_dec,
           onorm_g, proj_a, proj_b, w_out, norm2_g, w_ff1, w_ff2, final_g):
    bsz, t, d = x.shape
    depth = w_in.shape[0]
    assert depth == 1, "single-layer problem: the context stream only feeds the scan states"
    d_a = proj_a.shape[1]
    dv_tot = proj_b.shape[1]
    dk_tot = w_dec2.shape[-1]
    seg = _segment_rows(d, d_a, dk_tot, dv_tot, w_dec2.shape[2])

    mod = _mod_call(c, c_ctx, w_ada[0], b_ada[0])

    row = lambda a: a.reshape(1, -1)
    g1n = row(norm1_g[0])
    w_in_t = jnp.swapaxes(w_in[0], 0, 1)
    wd2, bd = w_dec2[0], b_dec[0][:, None, :]

    ck, cv, clgf, clgb = _ctx_proj_call(ctx, mod, bsz, g1n, w_in_t, wd2, bd, seg, dk_tot, dv_tot)
    s_ctx = _state_call(ck, cv, clgf, clgb)

    pos = _pos_tables(t, d)
    x4 = x.reshape(bsz, t // GRID_W, GRID_W, d)
    consts = (g1n, wd2, bd, row(lnv_g[0]), row(lnv_b[0]), w_s[0], b_s[0][:, :, None])
    ma, gb, q, k, v, sr, lgf, lgb, *tail_weights = _inproj_call(
        x4, pos, mod, consts, (w_in_t, proj_a[0]),
        (proj_b[0], w_out[0], w_ff1[0], w_ff2[0]), seg, d_a, dk_tot, dv_tot)
    o_f, o_b = _scan_call(q, k, v, lgf, lgb, s_ctx)

    return _tail_call(x4, pos, (ma, gb, o_f, o_b, sr), mod,
                      (row(onorm_g[0]), row(norm2_g[0]), row(final_g), *tail_weights))
```

```python
import functools
import math

import jax
import jax.numpy as jnp
import numpy as np
from jax import lax
from jax.experimental import pallas as pl
from jax.experimental.pallas import tpu as pltpu

F32 = jnp.float32
BF16 = jnp.bfloat16

GRID_W = 64
GLA_HEADS = 4
GATE_TAU = 16.0
EPS = 1e-6

TOKEN_TILE = 512
SCAN_CHUNK = 128
SCAN_TILE = 512
MOD_ROWS = 8
MOD_COL_TILE = 1024
PROJ_COL_TILE = 512
TAIL_PARTS = 4
PREPARE_PARTS = 4
STAGE_LOOKAHEAD = 1
WEIGHT_STAGE_ROWS = 512
WEIGHT_STAGE_SLOTS = 2
BF16_SUBLANES = 16
FF_COL_TILE = 1024
SAFE_DECAY = 60.0
VMEM_LIMIT = 56 * 1024 * 1024


def _sigmoid(x):
    return 1.0 / (1.0 + jnp.exp(-x))


def _gelu_tanh(x):
    c = math.sqrt(2.0 / math.pi)
    return 0.5 * x * (1.0 + jnp.tanh(c * (x + 0.044715 * (x * x * x))))


def _log_sigmoid(z):
    return jnp.minimum(z, 0.0) - jnp.log(1.0 + jnp.exp(-jnp.abs(z)))


def _rms(x, g):
    return x * lax.rsqrt(jnp.mean(x * x, axis=-1, keepdims=True) + EPS) * g


def _split_bf16(a):
    hi = a.astype(BF16)
    lo = (a - hi.astype(F32)).astype(BF16)
    return hi, lo


def _dot(a, b):
    return jnp.dot(a, b, preferred_element_type=F32)


def _dot_nt(a, b):
    return lax.dot_general(a, b, (((1,), (1,)), ((), ())), preferred_element_type=F32)


def _dot_tn(a, b):
    return lax.dot_general(a, b, (((0,), (0,)), ((), ())), preferred_element_type=F32)


def _const_spec(shape):
    nd = len(shape)
    return pl.BlockSpec(shape, lambda *_: (0,) * nd, pipeline_mode=pl.Buffered(1))


MOD_COLUMNS = ("shift1", "scale1", "gate1", "shift2", "scale2", "gate2")


def _mod_spec(mod, name):
    d = mod.shape[1] // len(MOD_COLUMNS)
    j = MOD_COLUMNS.index(name)
    return pl.BlockSpec((mod.shape[0], d), lambda *_: (0, j), pipeline_mode=pl.Buffered(1))


def _load_as_bf16(src_ref, dst_ref, stage_ref, sem_ref, rows):
    n_rows, cols = src_ref.shape
    slots = stage_ref.shape[0]
    assert n_rows % rows == 0 and rows <= stage_ref.shape[1] and cols == stage_ref.shape[2]
    n = n_rows // rows

    def copy(j, slot):
        return pltpu.make_async_copy(src_ref.at[pl.ds(j * rows, rows), :],
                                     stage_ref.at[slot, pl.ds(0, rows), :], sem_ref.at[slot])

    for j in range(min(slots - 1, n)):
        copy(j, j).start()

    def body(j, carry):
        ahead = j + slots - 1

        @pl.when(ahead < n)
        def _():
            copy(ahead, ahead % slots).start()

        copy(j, j % slots).wait()
        first = pl.multiple_of(j * rows, rows)
        dst_ref[pl.ds(first, rows), :] = stage_ref[j % slots, pl.ds(0, rows), :].astype(BF16)
        return carry

    lax.fori_loop(0, n, body, 0)


def _stage_rows(n_rows):
    return max(r for r in range(BF16_SUBLANES, WEIGHT_STAGE_ROWS + 1, BF16_SUBLANES)
               if n_rows % r == 0)


def _mod_kernel(c_ref, cc_ref, w_ref, b_ref, o_ref):
    rows, bsz = o_ref.shape[0], c_ref.shape[0]
    row_id = lax.broadcasted_iota(jnp.int32, (rows, c_ref.shape[1]), 0)
    c = jnp.where(row_id == bsz, cc_ref[...], 0.0)
    for j in range(bsz):
        c = jnp.where(row_id == j, c_ref[j:j + 1, :], c)
    a_hi, a_lo = _split_bf16(c * _sigmoid(c))
    w_hi, w_lo = _split_bf16(w_ref[...])
    both = _dot(jnp.concatenate([a_hi, a_lo], axis=0), w_hi)
    o_ref[...] = both[:rows] + both[rows:] + _dot(a_hi, w_lo) + b_ref[...]


def _mod_call(c, c_ctx, w_ada, b_ada):
    d, n = w_ada.shape
    assert c.shape[0] < MOD_ROWS
    return pl.pallas_call(
        _mod_kernel,
        out_shape=jax.ShapeDtypeStruct((MOD_ROWS, n), F32),
        grid=(n // MOD_COL_TILE,),
        in_specs=[pl.BlockSpec(c.shape, lambda j: (0, 0)),
                  pl.BlockSpec((1, d), lambda j: (0, 0)),
                  pl.BlockSpec((d, MOD_COL_TILE), lambda j: (0, j)),
                  pl.BlockSpec((1, MOD_COL_TILE), lambda j: (0, j))],
        out_specs=pl.BlockSpec((MOD_ROWS, MOD_COL_TILE), lambda j: (0, j)),
        compiler_params=pltpu.CompilerParams(dimension_semantics=("arbitrary",)),
        name="adaln_mod",
    )(c, c_ctx.reshape(1, d), w_ada, b_ada.reshape(1, n))


def _load_tokens(x_ref, row_ref, col_ref):
    x = x_ref[0]
    half = x.shape[-1] // 2
    x = jnp.concatenate([x[..., :half] + row_ref[...], x[..., half:] + col_ref[...]], axis=-1)
    return x.reshape(x.shape[0] * x.shape[1], x.shape[2])


def _segment_rows(d, d_a, dk_tot, dv_tot, rank):
    names = ("u", "va", "q", "k", "v", "r", "dec", "ga", "gb")
    widths = (d_a, d_a, dk_tot, dk_tot, dv_tot, dv_tot, 2 * rank, d, d)
    offsets, r0 = {}, 0
    for name, w in zip(names, widths):
        offsets[name] = r0
        r0 += w
    return offsets


def _store_log_decay(dec, wdec2_ref, bdec_ref, di, out_ref):
    rank = wdec2_ref.shape[1]
    z = _dot(dec[:, di * rank:(di + 1) * rank].astype(BF16),
             wdec2_ref[di].astype(BF16)) + bdec_ref[di]
    out_ref[0] = (_log_sigmoid(z) * (1.0 / GATE_TAU)).astype(out_ref.dtype)


def _ctx_proj_kernel(x_ref, sh_ref, sc_ref, g1_ref, wk_ref, wv_ref, wd_ref, wdec2_ref, bdec_ref,
                     k_ref, v_ref, lgf_ref, lgb_ref, *, mod_row):
    hb = (_rms(x_ref[0], g1_ref[...]) * (1.0 + sc_ref[mod_row:mod_row + 1, :])
          + sh_ref[mod_row:mod_row + 1, :]).astype(BF16)
    proj = lambda w_ref: _dot_nt(hb, w_ref[...].astype(BF16))
    dec = proj(wd_ref)
    for di, out_ref in enumerate((lgf_ref, lgb_ref)):
        _store_log_decay(dec, wdec2_ref, bdec_ref, di, out_ref)
    k_ref[0] = proj(wk_ref).astype(BF16)
    v_ref[0] = proj(wv_ref).astype(BF16)


def _ctx_proj_call(ctx, mod, mod_row, g1, w_in_t, w_dec2, b_dec, seg, dk_tot, dv_tot):
    bsz, t, d = ctx.shape
    tm = min(TOKEN_TILE, t)
    assert t % tm == 0
    tok = lambda w: pl.BlockSpec((1, tm, w), lambda b, i: (b, i, 0))

    def rows_spec(r0, width):
        assert r0 % width == 0
        return pl.BlockSpec((width, d), lambda b, i: (r0 // width, 0),
                            pipeline_mode=pl.Buffered(1))

    rank = w_dec2.shape[1]
    tail_consts = [w_dec2, b_dec]
    widths = [dk_tot, dv_tot, dk_tot, dk_tot]
    dtypes = [BF16] * 4
    return pl.pallas_call(
        functools.partial(_ctx_proj_kernel, mod_row=mod_row),
        out_shape=[jax.ShapeDtypeStruct((bsz, t, w), dt) for w, dt in zip(widths, dtypes)],
        grid=(bsz, t // tm),
        in_specs=([tok(d), _mod_spec(mod, "shift1"), _mod_spec(mod, "scale1"),
                   _const_spec(g1.shape)]
                  + [rows_spec(seg["k"], dk_tot), rows_spec(seg["v"], dv_tot),
                     rows_spec(seg["dec"], 2 * rank)]
                  + [_const_spec(a.shape) for a in tail_consts]),
        out_specs=[tok(w) for w in widths],
        compiler_params=pltpu.CompilerParams(dimension_semantics=("arbitrary", "arbitrary"),
                                             vmem_limit_bytes=VMEM_LIMIT),
        name="ctx_proj",
    )(ctx, mod, mod, g1, w_in_t, w_in_t, w_in_t, *tail_consts)


def _inproj_kernel(x_ref, row_ref, col_ref, sh_ref, sc_ref, g1_ref, wdec2_ref, bdec_ref,
                   lng_ref, lnb_ref, ws_ref, bs_ref, wt_hbm, pa_hbm, *rest,
                   seg, d_a, dk_tot, dv_tot, n_later, nt):
    later_f32, rest = rest[:n_later], rest[n_later:]
    (ma_ref, gb_ref, q_ref, k_ref, v_ref, sr_ref, lgf_ref, lgb_ref), rest = rest[:8], rest[8:]
    later_bf16, (ya_scr, hb_scr, wt_ref, pa_ref, stage_ref, sem_ref) = rest[:n_later], rest[n_later:]
    step = pl.program_id(0)
    _, tm, d = hb_scr.shape

    grid_rows = x_ref.shape[1]

    def prepare_next(part):
        n = grid_rows // PREPARE_PARTS
        xp = _load_tokens(x_ref.at[:, pl.ds(part * n, n)], row_ref.at[pl.ds(part * n, n)], col_ref)
        batch = jnp.minimum(step, pl.num_programs(0) - 2) // nt
        mine = pl.ds(batch, 1)
        hb = _rms(xp, g1_ref[...]) * (1.0 + sc_ref[mine, :]) + sh_ref[mine, :]
        hb_scr[step % 2, pl.ds(part * n * GRID_W, n * GRID_W), :] = hb.astype(BF16)

    def warm_up():
        for src, dst in ((wt_hbm, wt_ref), (pa_hbm, pa_ref)):
            _load_as_bf16(src, dst, stage_ref, sem_ref, _stage_rows(src.shape[0]))
        for part in range(PREPARE_PARTS):
            prepare_next(part)

    lax.cond(step == 0, warm_up,
             functools.partial(_project_tile, prepare_next, rest_refs=(
                 wdec2_ref, bdec_ref, lng_ref, lnb_ref, ws_ref, bs_ref, later_f32, later_bf16,
                 ma_ref, gb_ref, q_ref, k_ref, v_ref, sr_ref, lgf_ref, lgb_ref,
                 ya_scr, hb_scr, wt_ref, pa_ref),
                 seg=seg, d_a=d_a, dk_tot=dk_tot, dv_tot=dv_tot))


def _project_tile(prepare_next, *, rest_refs, seg, d_a, dk_tot, dv_tot):
    (wdec2_ref, bdec_ref, lng_ref, lnb_ref, ws_ref, bs_ref, later_f32, later_bf16,
     ma_ref, gb_ref, q_ref, k_ref, v_ref, sr_ref, lgf_ref, lgb_ref,
     ya_scr, hb_scr, wt_ref, pa_ref) = rest_refs
    _, tm, d = hb_scr.shape
    ready = (pl.program_id(0) + 1) % 2

    for src, dst in zip(later_f32, later_bf16):
        dst[...] = src[...].astype(BF16)

    def proj(r0, width):
        return _dot_nt(hb_scr[ready], wt_ref[r0:r0 + width, :])

    r_u, r_va, r_q, r_k, r_v, r_r = (seg[n] for n in ("u", "va", "q", "k", "v", "r"))
    r_ga, r_gb = seg["ga"], seg["gb"]
    rank = wdec2_ref.shape[1]
    groups, chunk_a = ws_ref.shape[0], ws_ref.shape[1]
    dk = dk_tot // GLA_HEADS
    gw = d_a // groups
    blk = PROJ_COL_TILE
    cols = lambda j: slice(j * blk, (j + 1) * blk)
    gu, gv, vn = {}, {}, {}

    stage = {}

    def store(out_ref, j, fn):
        def ep(y):
            out_ref[0, :, cols(j)] = fn(y).astype(out_ref.dtype)
        return ep

    def keep(table, j, fn):
        def ep(y):
            table[j] = fn(y)
        return ep

    def add(name, matmul, epilogue):
        stage[name] = (matmul, epilogue)
        return name

    def ep_dec(dec):
        for di, out_ref in enumerate((lgf_ref, lgb_ref)):
            _store_log_decay(dec, wdec2_ref, bdec_ref, di, out_ref)
    s_dec = add("dec", lambda: proj(seg["dec"], 2 * rank), ep_dec)
    s_u = [add(f"u{j}", lambda j=j: proj(r_u + j * blk, blk), keep(gu, j, _gelu_tanh))
           for j in range(d_a // blk)]
    s_va = [add(f"va{j}", lambda j=j: proj(r_va + j * blk, blk), keep(gv, j, _gelu_tanh))
            for j in range(d_a // blk)]

    def ep_q_and_layernorm(y):
        q_ref[0] = (y * dk ** -0.5).astype(BF16)
        nb = d_a // blk
        mu = sum(jnp.sum(gv[j], axis=-1, keepdims=True) for j in range(nb)) * (1.0 / d_a)
        cen = [gv[j] - mu for j in range(nb)]
        var = sum(jnp.sum(c * c, axis=-1, keepdims=True) for c in cen) * (1.0 / d_a)
        inv = lax.rsqrt(var + EPS)
        for j in range(nb):
            vn[j] = (cen[j] * inv * lng_ref[:, cols(j)] + lnb_ref[:, cols(j)]).astype(BF16)
    s_q = add("q", lambda: proj(r_q, dk_tot), ep_q_and_layernorm)
    def ep_k(y):
        k_ref[0] = y.astype(BF16)
    s_k = add("k", lambda: proj(r_k, dk_tot), ep_k)
    s_r = [add(f"r{j}", lambda j=j: proj(r_r + j * blk, blk),
               store(sr_ref, j, lambda y: y * _sigmoid(y))) for j in range(dv_tot // blk)]
    s_v = [add(f"v{j}", lambda j=j: proj(r_v + j * blk, blk), store(v_ref, j, lambda y: y))
           for j in range(dv_tot // blk)]

    def mm_gate(ci, g):
        r0, c0 = ci * chunk_a, g * gw
        return _dot(ws_ref[g].astype(BF16),
                    vn[c0 // blk][r0:r0 + chunk_a, c0 % blk:c0 % blk + gw])

    def ep_gate(ci, g):
        r0, c0 = ci * chunk_a, g * gw

        def ep(s):
            u = gu[c0 // blk][r0:r0 + chunk_a, c0 % blk:c0 % blk + gw]
            ya_scr[r0:r0 + chunk_a, c0:c0 + gw] = (u * (s + bs_ref[g])).astype(BF16)
        return ep
    s_mix = [add(f"mix{ci}_{g}", lambda ci=ci, g=g: mm_gate(ci, g), ep_gate(ci, g))
             for ci in range(tm // chunk_a) for g in range(groups)]
    s_gb = [add(f"gb{j}", lambda j=j: proj(r_gb + j * blk, blk), store(gb_ref, j, _sigmoid))
            for j in range(d // blk)]
    s_ga = [add(f"ga{j}", lambda j=j: (proj(r_ga + j * blk, blk),
                                       _dot(ya_scr[...], pa_ref[:, cols(j)])),
                store(ma_ref, j, lambda y: _sigmoid(y[0]) * y[1])) for j in range(d // blk)]

    order = ([s_dec] + s_u + s_va + [s_q, s_k] + s_r + s_gb + s_mix + s_v[:1] + s_ga + s_v[1:])
    assert sorted(order) == sorted(stage)
    extra = {}
    for part, name in enumerate(order[-PREPARE_PARTS - 1:-1]):
        extra.setdefault(name, []).append(functools.partial(prepare_next, part))
    results = {}
    for i in range(len(order) + STAGE_LOOKAHEAD):
        if i < len(order):
            results[order[i]] = stage[order[i]][0]()
        if i >= STAGE_LOOKAHEAD:
            done = order[i - STAGE_LOOKAHEAD]
            stage[done][1](results.pop(done))
            for work in extra.get(done, ()):
                work()


def _inproj_call(x4, pos, mod, consts, weights, later_weights, seg, d_a, dk_tot, dv_tot):
    bsz, nrow, gw_, d = x4.shape
    t = nrow * gw_
    tm = TOKEN_TILE
    nt = t // tm
    rows_per_tile = tm // GRID_W
    rowemb, colemb = pos
    chunk_a = consts[5].shape[1]
    assert tm % chunk_a == 0 and tm % GRID_W == 0 and t % tm == 0

    n_tiles = bsz * nt
    preparing = lambda s: jnp.minimum(s, n_tiles - 1)
    projecting = lambda s: jnp.maximum(s - 1, 0)

    def slab(w):
        rows = w.shape[0] // n_tiles
        assert w.shape[0] % n_tiles == 0 and rows % BF16_SUBLANES == 0
        return pl.BlockSpec((rows, w.shape[1]), lambda s: (projecting(s), 0))

    tok = lambda w: pl.BlockSpec((1, tm, w), lambda s: (projecting(s) // nt, projecting(s) % nt, 0))
    in_specs = ([pl.BlockSpec((1, rows_per_tile, GRID_W, d),
                              lambda s: (preparing(s) // nt, preparing(s) % nt, 0, 0)),
                 pl.BlockSpec((rows_per_tile, 1, d // 2), lambda s: (preparing(s) % nt, 0, 0)),
                 _const_spec(colemb.shape), _mod_spec(mod, "shift1"), _mod_spec(mod, "scale1")]
                + [_const_spec(a.shape) for a in consts]
                + [pl.BlockSpec(memory_space=pl.ANY)] * len(weights)
                + [slab(w) for w in later_weights])
    widths = [d, d, dk_tot, dk_tot, dv_tot, dv_tot, dk_tot, dk_tot]
    dtypes = [BF16] * 8
    kern = functools.partial(_inproj_kernel, seg=seg, d_a=d_a, dk_tot=dk_tot, dv_tot=dv_tot,
                             n_later=len(later_weights), nt=nt)
    return pl.pallas_call(
        kern,
        out_shape=([jax.ShapeDtypeStruct((bsz, t, w), dt) for w, dt in zip(widths, dtypes)]
                   + [jax.ShapeDtypeStruct(w.shape, BF16) for w in later_weights]),
        grid=(n_tiles + 1,),
        in_specs=in_specs,
        out_specs=[tok(w) for w in widths] + [slab(w) for w in later_weights],
        scratch_shapes=([pltpu.VMEM((tm, d_a), BF16), pltpu.VMEM((2, tm, d), BF16)]
                        + [pltpu.VMEM(w.shape, BF16) for w in weights]
                        + [pltpu.VMEM((WEIGHT_STAGE_SLOTS, WEIGHT_STAGE_ROWS, d), F32),
                           pltpu.SemaphoreType.DMA((WEIGHT_STAGE_SLOTS,))]),
        compiler_params=pltpu.CompilerParams(dimension_semantics=("arbitrary",),
                                             vmem_limit_bytes=VMEM_LIMIT),
        name="inproj",
    )(x4, rowemb, colemb, mod, mod, *consts, *weights, *later_weights)


def _scan_masks(c):
    row = lax.broadcasted_iota(jnp.int32, (c, c), 0)
    col = lax.broadcasted_iota(jnp.int32, (c, c), 1)
    return ((row >= col, c - 1), (row <= col, 0))


def _chunk_cumsum(lg, keep, end):
    b = _dot(jnp.where(keep, 1.0, 0.0).astype(BF16), lg)
    return b, b[end:end + 1, :]


def _decay_rows(st, tot):
    dk, dv = st.shape
    dcol = jnp.transpose(jnp.broadcast_to(jnp.exp(tot), (dk, dk)))
    return st * jnp.concatenate([dcol] * (dv // dk), axis=1)


def _state_kernel(kf_ref, vf_ref, lgf_ref, kb_ref, vb_ref, lgb_ref, s_ref, st_scr, *, dk, dv):
    t = pl.program_id(0)
    bsz, c = kf_ref.shape[0], SCAN_CHUNK
    n_sub = kf_ref.shape[1] // c

    @pl.when(t == 0)
    def _():
        st_scr[...] = jnp.zeros_like(st_scr)

    masks = _scan_masks(c)
    dirs = ((kf_ref, vf_ref, lgf_ref), (kb_ref, vb_ref, lgb_ref))
    for sub in range(n_sub):
        for bi in range(bsz):
            for di, (k_ref, v_ref, lg_ref) in enumerate(dirs):
                first = (sub if di == 0 else n_sub - 1 - sub) * c
                b, tot = _chunk_cumsum(lg_ref[bi, first:first + c, :], *masks[di])
                for h in range(GLA_HEADS):
                    sl = slice(h * dk, (h + 1) * dk)
                    k = k_ref[bi, first:first + c, sl].astype(F32)
                    kd = (k * jnp.exp(tot[:, sl] - b[:, sl])).astype(BF16)
                    v = v_ref[bi, first:first + c, h * dv:(h + 1) * dv]
                    st_scr[bi, di, h] = _decay_rows(st_scr[bi, di, h], tot[:, sl]) + _dot_tn(kd, v)

    @pl.when(t == pl.num_programs(0) - 1)
    def _():
        s_ref[...] = st_scr[...]


def _state_call(k, v, lgf, lgb):
    bsz, t, dk_tot = k.shape
    dv_tot = v.shape[-1]
    dk, dv = dk_tot // GLA_HEADS, dv_tot // GLA_HEADS
    tile = min(SCAN_TILE, t)
    nt = t // tile
    assert t % tile == 0 and tile % SCAN_CHUNK == 0 and dv % dk == 0
    fwd = lambda w: pl.BlockSpec((bsz, tile, w), lambda i: (0, i, 0))
    bwd = lambda w: pl.BlockSpec((bsz, tile, w), lambda i: (0, nt - 1 - i, 0))
    shape = (bsz, 2, GLA_HEADS, dk, dv)
    return pl.pallas_call(
        functools.partial(_state_kernel, dk=dk, dv=dv),
        out_shape=jax.ShapeDtypeStruct(shape, F32),
        grid=(nt,),
        in_specs=[fwd(dk_tot), fwd(dv_tot), fwd(dk_tot), bwd(dk_tot), bwd(dv_tot), bwd(dk_tot)],
        out_specs=pl.BlockSpec(shape, lambda i: (0,) * len(shape)),
        scratch_shapes=[pltpu.VMEM(shape, F32)],
        compiler_params=pltpu.CompilerParams(dimension_semantics=("arbitrary",),
                                             vmem_limit_bytes=VMEM_LIMIT),
        name="gla_ctx_state",
    )(k, v, lgf, k, v, lgb)


def _scan_kernel(qf_ref, kf_ref, vf_ref, lgf_ref, qb_ref, kb_ref, vb_ref, lgb_ref, s0_ref,
                 of_ref, ob_ref, st_scr, q32_scr, k32_scr, v32_scr, b_scr, o_scr,
                 *, dk, dv):
    t = pl.program_id(0)
    bsz, c = qf_ref.shape[0], SCAN_CHUNK
    n_sub = qf_ref.shape[1] // c

    def rows(di, sub):
        first = (sub if di == 0 else n_sub - 1 - sub) * c
        return slice(first, first + c)

    @pl.when(t == 0)
    def _():
        st_scr[...] = s0_ref[...]

    masks = _scan_masks(c)
    dirs = ((qf_ref, kf_ref, vf_ref, lgf_ref, of_ref) + masks[0],
            (qb_ref, kb_ref, vb_ref, lgb_ref, ob_ref) + masks[1])

    worst = None
    for lg_ref in (lgf_ref, lgb_ref):
        for sub in range(n_sub):
            w = jnp.max(-jnp.sum(lg_ref[:, sub * c:(sub + 1) * c, :].astype(F32), axis=1))
            worst = w if worst is None else jnp.maximum(worst, w)

    cums = {}

    def cumulate(bi, di, sub):
        lg_ref, keep, end = dirs[di][3], dirs[di][5], dirs[di][6]
        cums[bi, di, sub] = _chunk_cumsum(lg_ref[bi, rows(di, sub), :], keep, end)

    def load(bi, di, h, sub):
        q_ref, k_ref, v_ref = dirs[di][:3]
        b, tot = cums[bi, di, sub]
        sl = slice(h * dk, (h + 1) * dk)
        return (q_ref[bi, rows(di, sub), sl].astype(F32), k_ref[bi, rows(di, sub), sl].astype(F32),
                v_ref[bi, rows(di, sub), h * dv:(h + 1) * dv], b[:, sl], tot[:, sl])

    def decayed_state(bi, di, h, tot):
        return _decay_rows(st_scr[bi, di, h], tot)

    def fast_step():
        chains = [(bi, di, h, sub) for sub in range(n_sub) for bi in range(bsz)
                  for di in range(2) for h in range(GLA_HEADS)]
        live = {}

        def factors(n):
            q, k, v, bh, tot = load(*chains[n])
            half = 0.5 * tot
            eh = jnp.exp(half)
            qe = q * jnp.exp(bh - half)
            ke = k * jnp.exp(half - bh)
            live[n] = dict(v=v, tot=tot, qe=qe.astype(BF16), ke=ke.astype(BF16),
                           qd=(qe * eh).astype(BF16), kd=(ke * eh).astype(BF16))

        def scores(n):
            live[n]["a"] = _dot_nt(live[n].pop("qe"), live[n].pop("ke"))

        def outputs(n):
            bi, di, h, _ = chains[n]
            c_ = live[n]
            a = jnp.where(dirs[di][5], c_.pop("a"), 0.0).astype(BF16)
            lhs = jnp.concatenate([a, c_.pop("qd")], axis=1)
            rhs = jnp.concatenate([c_["v"], st_scr[bi, di, h].astype(BF16)], axis=0)
            c_["o"] = _dot(lhs, rhs)
            c_["u"] = _dot_tn(c_.pop("kd"), c_.pop("v"))

        def finish(n):
            bi, di, h, sub = chains[n]
            c_ = live.pop(n)
            o_ref = dirs[di][4]
            o_ref[bi, rows(di, sub), h * dv:(h + 1) * dv] = c_["o"].astype(o_ref.dtype)
            st_scr[bi, di, h] = decayed_state(bi, di, h, c_["tot"]) + c_["u"]

        def cumulate_for(n):
            bi, di, _, sub = chains[n]
            cumulate(bi, di, sub)

        phases = ((factors, 0), (scores, 1), (outputs, 3), (finish, 5))
        cumulate_for(0)
        for i in range(len(chains) + phases[-1][1]):
            ahead = i + GLA_HEADS
            if i % GLA_HEADS == 0 and ahead < len(chains):
                cumulate_for(ahead)
            for phase, lag in phases:
                if 0 <= i - lag < len(chains):
                    phase(i - lag)

    def safe_step():
        ridx = lax.broadcasted_iota(jnp.int32, (c, 1), 0)
        groups = [(s, b, d) for s in range(n_sub) for b in range(bsz) for d in range(2)]
        for sub, bi, di in groups:
            o_ref = dirs[di][4]
            cumulate(bi, di, sub)
            for h in range(GLA_HEADS):
                q, k, v, bh, tot = load(bi, di, h, sub)
                q32_scr[...] = q
                k32_scr[...] = k
                v32_scr[...] = v.astype(F32)
                b_scr[...] = bh
                o_scr[...] = _dot((q * jnp.exp(bh)).astype(BF16), st_scr[bi, di, h].astype(BF16))
                st_scr[bi, di, h] = (decayed_state(bi, di, h, tot)
                                     + _dot_tn((k * jnp.exp(tot - bh)).astype(BF16), v))

                def body(j, carry):
                    bj = b_scr[pl.ds(j, 1), :]
                    valid = (ridx >= j) if di == 0 else (ridx <= j)
                    e = jnp.exp(jnp.where(valid, b_scr[...] - bj, -jnp.inf))
                    w = jnp.sum(q32_scr[...] * e * k32_scr[pl.ds(j, 1), :], axis=-1, keepdims=True)
                    o_scr[...] += w * v32_scr[pl.ds(j, 1), :]
                    return carry

                lax.fori_loop(0, c, body, 0)
                o_ref[bi, rows(di, sub), h * dv:(h + 1) * dv] = o_scr[...].astype(o_ref.dtype)

    lax.cond(worst <= SAFE_DECAY, fast_step, safe_step)


def _scan_call(q, k, v, lgf, lgb, s0):
    bsz, t, dk_tot = q.shape
    dv_tot = v.shape[-1]
    dk, dv = dk_tot // GLA_HEADS, dv_tot // GLA_HEADS
    c = SCAN_CHUNK
    tile = SCAN_TILE
    nt = t // tile
    assert t % tile == 0 and tile % c == 0 and dv % dk == 0
    fwd = lambda w: pl.BlockSpec((bsz, tile, w), lambda i: (0, i, 0))
    bwd = lambda w: pl.BlockSpec((bsz, tile, w), lambda i: (0, nt - 1 - i, 0))
    st_spec = pl.BlockSpec(s0.shape, lambda i: (0,) * s0.ndim)
    kern = functools.partial(_scan_kernel, dk=dk, dv=dv)
    return pl.pallas_call(
        kern,
        out_shape=[jax.ShapeDtypeStruct((bsz, t, dv_tot), BF16),
                   jax.ShapeDtypeStruct((bsz, t, dv_tot), BF16)],
        grid=(nt,),
        in_specs=[fwd(dk_tot), fwd(dk_tot), fwd(dv_tot), fwd(dk_tot),
                  bwd(dk_tot), bwd(dk_tot), bwd(dv_tot), bwd(dk_tot), st_spec],
        out_specs=[fwd(dv_tot), bwd(dv_tot)],
        scratch_shapes=[pltpu.VMEM(s0.shape, F32),
                        pltpu.VMEM((c, dk), F32), pltpu.VMEM((c, dk), F32),
                        pltpu.VMEM((c, dv), F32), pltpu.VMEM((c, dk), F32),
                        pltpu.VMEM((c, dv), F32)],
        compiler_params=pltpu.CompilerParams(dimension_semantics=("arbitrary",),
                                             vmem_limit_bytes=VMEM_LIMIT),
        name="gla_scan",
    )(q, k, v, lgf, q, k, v, lgb, s0)


def _tail_kernel(x_ref, row_ref, col_ref, ma_ref, gb_ref, of_ref, ob_ref, sr_ref,
                 g1_ref, sh2_ref, sc2_ref, g2_ref, og_ref, n2_ref, fg_ref,
                 pb_ref, wo_ref, w1_ref, w2_ref, out_ref, lhs_scr, *, dv, nt):
    step = pl.program_id(0)
    n_tiles = pl.num_programs(0) - 1
    tm = lhs_scr.shape[1]
    rows = tm // TAIL_PARTS
    mlp_in = 2

    def prepare(part):
        sl = pl.ds(part * rows, rows)
        o = of_ref[0, sl, :].astype(F32) + ob_ref[0, sl, :].astype(F32)
        heads = [_rms(o[:, h * dv:(h + 1) * dv], og_ref[:, h * dv:(h + 1) * dv])
                 for h in range(GLA_HEADS)]
        yb = jnp.concatenate(heads, axis=-1) * sr_ref[0, sl, :].astype(F32)
        lhs_scr[step % 2, sl, :] = yb.astype(BF16)

    def warm_up():
        for part in range(TAIL_PARTS):
            prepare(part)

    def chain():
        xp = _load_tokens(x_ref, row_ref, col_ref)
        mine = pl.ds(jnp.maximum(step - 1, 0) // nt, 1)
        yb = lhs_scr[(step + 1) % 2]
        merged = ma_ref[0].astype(F32) + gb_ref[0].astype(F32) * _dot(yb, pb_ref[...])
        x1 = xp + g1_ref[mine, :] * _dot(merged.astype(BF16), wo_ref[...])
        lhs_scr[mlp_in] = (_rms(x1, n2_ref[...]) * (1.0 + sc2_ref[mine, :])
                           + sh2_ref[mine, :]).astype(BF16)
        acc = jnp.zeros_like(x1)
        blocks = w1_ref.shape[1] // FF_COL_TILE
        for j in range(blocks):
            sl = slice(j * FF_COL_TILE, (j + 1) * FF_COL_TILE)
            f = jnp.maximum(_dot(lhs_scr[mlp_in], w1_ref[:, sl]), 0.0)
            acc = acc + _dot((f * f).astype(BF16), w2_ref[sl, :])
            for part in range(j * TAIL_PARTS // blocks, (j + 1) * TAIL_PARTS // blocks):
                prepare(part)
        x2 = x1 + g2_ref[mine, :] * acc
        out_ref[0] = _rms(x2, fg_ref[...])

    lax.cond(step == 0, warm_up, chain)


def _tail_call(x4, pos, acts, mod, consts):
    bsz, nrow, gw_, d = x4.shape
    t = nrow * gw_
    tm = TOKEN_TILE
    nt = t // tm
    n_tiles = bsz * nt
    rows_per_tile = tm // GRID_W
    rowemb, colemb = pos
    ma, gb, o_f, o_b, sr = acts
    dv = o_f.shape[-1] // GLA_HEADS
    assert tm % TAIL_PARTS == 0 and (tm // TAIL_PARTS) % BF16_SUBLANES == 0

    assert o_f.shape[-1] == d
    ahead = lambda s: jnp.minimum(s, n_tiles - 1)
    current = lambda s: jnp.maximum(s - 1, 0)
    tok = lambda a, tile: pl.BlockSpec((1, tm, a.shape[-1]),
                                       lambda s: (tile(s) // nt, tile(s) % nt, 0))
    mod_names = ("gate1", "shift2", "scale2", "gate2")
    in_specs = ([pl.BlockSpec((1, rows_per_tile, GRID_W, d),
                              lambda s: (current(s) // nt, current(s) % nt, 0, 0)),
                 pl.BlockSpec((rows_per_tile, 1, d // 2), lambda s: (current(s) % nt, 0, 0)),
                 _const_spec(colemb.shape),
                 tok(ma, current), tok(gb, current), tok(o_f, ahead), tok(o_b, ahead),
                 tok(sr, ahead)]
                + [_mod_spec(mod, name) for name in mod_names]
                + [_const_spec(a.shape) for a in consts])
    return pl.pallas_call(
        functools.partial(_tail_kernel, dv=dv, nt=nt),
        out_shape=jax.ShapeDtypeStruct((bsz, t, d), F32),
        grid=(n_tiles + 1,),
        in_specs=in_specs,
        out_specs=pl.BlockSpec((1, tm, d), lambda s: (current(s) // nt, current(s) % nt, 0)),
        scratch_shapes=[pltpu.VMEM((3, tm, d), BF16)],
        compiler_params=pltpu.CompilerParams(dimension_semantics=("arbitrary",),
                                             vmem_limit_bytes=VMEM_LIMIT),
        name="tail",
    )(x4, rowemb, colemb, *acts, *[mod] * len(mod_names), *consts)


def _pos_tables(t, d):
    quarter = d // 4
    one, base = np.float32(1.0), np.float32(10000.0)
    freqs = one / (base ** (np.arange(quarter, dtype=np.float32) / np.float32(quarter)))

    def emb(n):
        a = np.arange(n, dtype=np.float32)[:, None] * freqs[None, :]
        return np.concatenate([np.sin(a), np.cos(a)], -1).astype(np.float32)

    return jnp.asarray(emb(t // GRID_W)[:, None, :]), jnp.asarray(emb(GRID_W)[None, :, :])


def kernel(x, c, ctx, c_ctx, w_ada, b_ada, norm1_g, w_in, lnv_g, lnv_b, w_s, b_s, w_dec2, b_dec,
           onorm_g, proj_a, proj_b, w_out, norm2_g, w_ff1, w_ff2, final_g):
    bsz, t, d = x.shape
    depth = w_in.shape[0]
    assert depth == 1, "single-layer problem: the context stream only feeds the scan states"
    d_a = proj_a.shape[1]
    dv_tot = proj_b.shape[1]
    dk_tot = w_dec2.shape[-1]
    seg = _segment_rows(d, d_a, dk_tot, dv_tot, w_dec2.shape[2])

    mod = _mod_call(c, c_ctx, w_ada[0], b_ada[0])

    row = lambda a: a.reshape(1, -1)
    g1n = row(norm1_g[0])
    w_in_t = jnp.swapaxes(w_in[0], 0, 1)
    wd2, bd = w_dec2[0], b_dec[0][:, None, :]

    ck, cv, clgf, clgb = _ctx_proj_call(ctx, mod, bsz, g1n, w_in_t, wd2, bd, seg, dk_tot, dv_tot)
    s_ctx = _state_call(ck, cv, clgf, clgb)

    pos = _pos_tables(t, d)
    x4 = x.reshape(bsz, t // GRID_W, GRID_W, d)
    consts = (g1n, wd2, bd, row(lnv_g[0]), row(lnv_b[0]), w_s[0], b_s[0][:, :, None])
    ma, gb, q, k, v, sr, lgf, lgb, *tail_weights = _inproj_call(
        x4, pos, mod, consts, (w_in_t, proj_a[0]),
        (proj_b[0], w_out[0], w_ff1[0], w_ff2[0]), seg, d_a, dk_tot, dv_tot)
    o_f, o_b = _scan_call(q, k, v, lgf, lgb, s_ctx)

    return _tail_call(x4, pos, (ma, gb, o_f, o_b, sr), mod,
                      (row(onorm_g[0]), row(norm2_g[0]), row(final_g), *tail_weights))
```

```python
import functools
import math

import jax
import jax.numpy as jnp
import numpy as np
from jax import lax
from jax.experimental import pallas as pl
from jax.experimental.pallas import tpu as pltpu

F32 = jnp.float32
BF16 = jnp.bfloat16

GRID_W = 64
GLA_HEADS = 4
GATE_TAU = 16.0
EPS = 1e-6

TOKEN_TILE = 512
SCAN_CHUNK = 128
SCAN_TILE = 512
MOD_ROWS = 8
MOD_COL_TILE = 1024
PROJ_COL_TILE = 512
PREPARE_PARTS = 4
STAGE_LOOKAHEAD = 1
WEIGHT_STAGE_ROWS = 512
WEIGHT_STAGE_SLOTS = 2
BF16_SUBLANES = 16
FF_COL_TILE = 1024
SAFE_DECAY = 60.0
VMEM_LIMIT = 56 * 1024 * 1024


def _sigmoid(x):
    return 1.0 / (1.0 + jnp.exp(-x))


def _gelu_tanh(x):
    c = math.sqrt(2.0 / math.pi)
    return 0.5 * x * (1.0 + jnp.tanh(c * (x + 0.044715 * (x * x * x))))


def _log_sigmoid(z):
    return jnp.minimum(z, 0.0) - jnp.log(1.0 + jnp.exp(-jnp.abs(z)))


def _rms(x, g):
    return x * lax.rsqrt(jnp.mean(x * x, axis=-1, keepdims=True) + EPS) * g


def _split_bf16(a):
    hi = a.astype(BF16)
    lo = (a - hi.astype(F32)).astype(BF16)
    return hi, lo


def _dot(a, b):
    return jnp.dot(a, b, preferred_element_type=F32)


def _dot_nt(a, b):
    return lax.dot_general(a, b, (((1,), (1,)), ((), ())), preferred_element_type=F32)


def _dot_tn(a, b):
    return lax.dot_general(a, b, (((0,), (0,)), ((), ())), preferred_element_type=F32)


def _const_spec(shape):
    nd = len(shape)
    return pl.BlockSpec(shape, lambda *_: (0,) * nd, pipeline_mode=pl.Buffered(1))


MOD_COLUMNS = ("shift1", "scale1", "gate1", "shift2", "scale2", "gate2")


def _mod_spec(mod, name):
    d = mod.shape[1] // len(MOD_COLUMNS)
    j = MOD_COLUMNS.index(name)
    return pl.BlockSpec((mod.shape[0], d), lambda *_: (0, j), pipeline_mode=pl.Buffered(1))


def _load_as_bf16(src_ref, dst_ref, stage_ref, sem_ref, rows):
    n_rows, cols = src_ref.shape
    slots = stage_ref.shape[0]
    assert n_rows % rows == 0 and rows <= stage_ref.shape[1] and cols == stage_ref.shape[2]
    n = n_rows // rows

    def copy(j, slot):
        return pltpu.make_async_copy(src_ref.at[pl.ds(j * rows, rows), :],
                                     stage_ref.at[slot, pl.ds(0, rows), :], sem_ref.at[slot])

    for j in range(min(slots - 1, n)):
        copy(j, j).start()

    def body(j, carry):
        ahead = j + slots - 1

        @pl.when(ahead < n)
        def _():
            copy(ahead, ahead % slots).start()

        copy(j, j % slots).wait()
        first = pl.multiple_of(j * rows, rows)
        dst_ref[pl.ds(first, rows), :] = stage_ref[j % slots, pl.ds(0, rows), :].astype(BF16)
        return carry

    lax.fori_loop(0, n, body, 0)


def _stage_rows(n_rows):
    return max(r for r in range(BF16_SUBLANES, WEIGHT_STAGE_ROWS + 1, BF16_SUBLANES)
               if n_rows % r == 0)


def _mod_kernel(c_ref, cc_ref, w_ref, b_ref, o_ref):
    rows, bsz = o_ref.shape[0], c_ref.shape[0]
    row_id = lax.broadcasted_iota(jnp.int32, (rows, c_ref.shape[1]), 0)
    c = jnp.where(row_id == bsz, cc_ref[...], 0.0)
    for j in range(bsz):
        c = jnp.where(row_id == j, c_ref[j:j + 1, :], c)
    a_hi, a_lo = _split_bf16(c * _sigmoid(c))
    w_hi, w_lo = _split_bf16(w_ref[...])
    both = _dot(jnp.concatenate([a_hi, a_lo], axis=0), w_hi)
    o_ref[...] = both[:rows] + both[rows:] + _dot(a_hi, w_lo) + b_ref[...]


def _mod_call(c, c_ctx, w_ada, b_ada):
    d, n = w_ada.shape
    assert c.shape[0] < MOD_ROWS
    return pl.pallas_call(
        _mod_kernel,
        out_shape=jax.ShapeDtypeStruct((MOD_ROWS, n), F32),
        grid=(n // MOD_COL_TILE,),
        in_specs=[pl.BlockSpec(c.shape, lambda j: (0, 0)),
                  pl.BlockSpec((1, d), lambda j: (0, 0)),
                  pl.BlockSpec((d, MOD_COL_TILE), lambda j: (0, j)),
                  pl.BlockSpec((1, MOD_COL_TILE), lambda j: (0, j))],
        out_specs=pl.BlockSpec((MOD_ROWS, MOD_COL_TILE), lambda j: (0, j)),
        compiler_params=pltpu.CompilerParams(dimension_semantics=("arbitrary",)),
        name="adaln_mod",
    )(c, c_ctx.reshape(1, d), w_ada, b_ada.reshape(1, n))


def _load_tokens(x_ref, row_ref, col_ref):
    x = x_ref[0]
    half = x.shape[-1] // 2
    x = jnp.concatenate([x[..., :half] + row_ref[...], x[..., half:] + col_ref[...]], axis=-1)
    return x.reshape(x.shape[0] * x.shape[1], x.shape[2])


def _segment_rows(d, d_a, dk_tot, dv_tot, rank):
    names = ("u", "va", "q", "k", "v", "r", "dec", "ga", "gb")
    widths = (d_a, d_a, dk_tot, dk_tot, dv_tot, dv_tot, 2 * rank, d, d)
    offsets, r0 = {}, 0
    for name, w in zip(names, widths):
        offsets[name] = r0
        r0 += w
    return offsets


def _store_log_decay(dec, wdec2_ref, bdec_ref, di, out_ref):
    rank = wdec2_ref.shape[1]
    z = _dot(dec[:, di * rank:(di + 1) * rank].astype(BF16),
             wdec2_ref[di].astype(BF16)) + bdec_ref[di]
    out_ref[0] = (_log_sigmoid(z) * (1.0 / GATE_TAU)).astype(out_ref.dtype)


def _ctx_state_kernel(x_ref, sh_ref, sc_ref, g1_ref, wk_ref, wv_ref, wd_ref, wdec2_ref, bdec_ref,
                      s_ref, k_scr, v_scr, lgf_scr, lgb_scr, *, mod_row):
    bsz, t, d = x_ref.shape
    c = SCAN_CHUNK
    n_sub = t // c
    dk, dv = s_ref.shape[-2:]
    hb = (_rms(x_ref[...].reshape(bsz * t, d), g1_ref[...]) * (1.0 + sc_ref[mod_row:mod_row + 1, :])
          + sh_ref[mod_row:mod_row + 1, :]).astype(BF16)
    proj = lambda w_ref: _dot_nt(hb, w_ref[...].astype(BF16))
    dec = proj(wd_ref)
    for di, lg_scr in enumerate((lgf_scr, lgb_scr)):
        _store_log_decay(dec, wdec2_ref, bdec_ref, di, lg_scr)
    k_scr[...] = proj(wk_ref).astype(BF16)
    v_scr[...] = proj(wv_ref).astype(BF16)

    s_ref[...] = jnp.zeros_like(s_ref)
    masks = _scan_masks(c)
    for sub in range(n_sub):
        for bi in range(bsz):
            for di, lg_scr in enumerate((lgf_scr, lgb_scr)):
                first = bi * t + (sub if di == 0 else n_sub - 1 - sub) * c
                b, tot = _chunk_cumsum(lg_scr[0, first:first + c, :], *masks[di])
                for h in range(GLA_HEADS):
                    sl = slice(h * dk, (h + 1) * dk)
                    k = k_scr[first:first + c, sl].astype(F32)
                    kd = (k * jnp.exp(tot[:, sl] - b[:, sl])).astype(BF16)
                    v = v_scr[first:first + c, h * dv:(h + 1) * dv]
                    s_ref[bi, di, h] = _decay_rows(s_ref[bi, di, h], tot[:, sl]) + _dot_tn(kd, v)


def _ctx_state_call(ctx, mod, mod_row, g1, w_in_t, w_dec2, b_dec, seg, dk_tot, dv_tot):
    bsz, t, d = ctx.shape
    dk, dv = dk_tot // GLA_HEADS, dv_tot // GLA_HEADS
    assert t % SCAN_CHUNK == 0 and dv % dk == 0

    def rows_spec(r0, width):
        assert r0 % width == 0
        return pl.BlockSpec((width, d), lambda i: (r0 // width, 0))

    full = lambda a: pl.BlockSpec(a.shape, lambda i: (0,) * a.ndim)
    rank = w_dec2.shape[1]
    shape = (bsz, 2, GLA_HEADS, dk, dv)
    return pl.pallas_call(
        functools.partial(_ctx_state_kernel, mod_row=mod_row),
        out_shape=jax.ShapeDtypeStruct(shape, F32),
        grid=(1,),
        in_specs=[full(ctx), _mod_spec(mod, "shift1"), _mod_spec(mod, "scale1"), full(g1),
                  rows_spec(seg["k"], dk_tot), rows_spec(seg["v"], dv_tot),
                  rows_spec(seg["dec"], 2 * rank), full(w_dec2), full(b_dec)],
        out_specs=pl.BlockSpec(shape, lambda i: (0,) * len(shape)),
        scratch_shapes=[pltpu.VMEM((bsz * t, dk_tot), BF16), pltpu.VMEM((bsz * t, dv_tot), BF16),
                        pltpu.VMEM((1, bsz * t, dk_tot), BF16),
                        pltpu.VMEM((1, bsz * t, dk_tot), BF16)],
        compiler_params=pltpu.CompilerParams(dimension_semantics=("arbitrary",),
                                             vmem_limit_bytes=VMEM_LIMIT),
        name="ctx_state",
    )(ctx, mod, mod, g1, w_in_t, w_in_t, w_in_t, w_dec2, b_dec)


def _inproj_kernel(x_ref, row_ref, col_ref, sh_ref, sc_ref, g1_ref, wdec2_ref, bdec_ref,
                   lng_ref, lnb_ref, ws_ref, bs_ref, wt_hbm, pa_hbm, *rest,
                   seg, d_a, dk_tot, dv_tot, n_later, nt):
    later_f32, rest = rest[:n_later], rest[n_later:]
    (ma_ref, gb_ref, q_ref, k_ref, v_ref, sr_ref, lgf_ref, lgb_ref), rest = rest[:8], rest[8:]
    later_bf16, (ya_scr, hb_scr, wt_ref, pa_ref, stage_ref, sem_ref) = rest[:n_later], rest[n_later:]
    step = pl.program_id(0)
    _, tm, d = hb_scr.shape

    grid_rows = x_ref.shape[1]

    def prepare_next(part):
        n = grid_rows // PREPARE_PARTS
        xp = _load_tokens(x_ref.at[:, pl.ds(part * n, n)], row_ref.at[pl.ds(part * n, n)], col_ref)
        batch = jnp.minimum(step, pl.num_programs(0) - 2) // nt
        mine = pl.ds(batch, 1)
        hb = _rms(xp, g1_ref[...]) * (1.0 + sc_ref[mine, :]) + sh_ref[mine, :]
        hb_scr[step % 2, pl.ds(part * n * GRID_W, n * GRID_W), :] = hb.astype(BF16)

    def warm_up():
        for src, dst in ((wt_hbm, wt_ref), (pa_hbm, pa_ref)):
            _load_as_bf16(src, dst, stage_ref, sem_ref, _stage_rows(src.shape[0]))
        for part in range(PREPARE_PARTS):
            prepare_next(part)

    lax.cond(step == 0, warm_up,
             functools.partial(_project_tile, prepare_next, rest_refs=(
                 wdec2_ref, bdec_ref, lng_ref, lnb_ref, ws_ref, bs_ref, later_f32, later_bf16,
                 ma_ref, gb_ref, q_ref, k_ref, v_ref, sr_ref, lgf_ref, lgb_ref,
                 ya_scr, hb_scr, wt_ref, pa_ref),
                 seg=seg, d_a=d_a, dk_tot=dk_tot, dv_tot=dv_tot))


def _project_tile(prepare_next, *, rest_refs, seg, d_a, dk_tot, dv_tot):
    (wdec2_ref, bdec_ref, lng_ref, lnb_ref, ws_ref, bs_ref, later_f32, later_bf16,
     ma_ref, gb_ref, q_ref, k_ref, v_ref, sr_ref, lgf_ref, lgb_ref,
     ya_scr, hb_scr, wt_ref, pa_ref) = rest_refs
    _, tm, d = hb_scr.shape
    ready = (pl.program_id(0) + 1) % 2

    for src, dst in zip(later_f32, later_bf16):
        dst[...] = src[...].astype(BF16)

    def proj(r0, width):
        return _dot_nt(hb_scr[ready], wt_ref[r0:r0 + width, :])

    r_u, r_va, r_q, r_k, r_v, r_r = (seg[n] for n in ("u", "va", "q", "k", "v", "r"))
    r_ga, r_gb = seg["ga"], seg["gb"]
    rank = wdec2_ref.shape[1]
    groups, chunk_a = ws_ref.shape[0], ws_ref.shape[1]
    dk = dk_tot // GLA_HEADS
    gw = d_a // groups
    blk = PROJ_COL_TILE
    cols = lambda j: slice(j * blk, (j + 1) * blk)
    gu, gv, vn = {}, {}, {}

    stage = {}

    def store(out_ref, j, fn):
        def ep(y):
            out_ref[0, :, cols(j)] = fn(y).astype(out_ref.dtype)
        return ep

    def keep(table, j, fn):
        def ep(y):
            table[j] = fn(y)
        return ep

    def add(name, matmul, epilogue):
        stage[name] = (matmul, epilogue)
        return name

    def ep_dec(dec):
        for di, out_ref in enumerate((lgf_ref, lgb_ref)):
            _store_log_decay(dec, wdec2_ref, bdec_ref, di, out_ref)
    s_dec = add("dec", lambda: proj(seg["dec"], 2 * rank), ep_dec)
    s_u = [add(f"u{j}", lambda j=j: proj(r_u + j * blk, blk), keep(gu, j, _gelu_tanh))
           for j in range(d_a // blk)]
    s_va = [add(f"va{j}", lambda j=j: proj(r_va + j * blk, blk), keep(gv, j, _gelu_tanh))
            for j in range(d_a // blk)]

    def ep_q_and_layernorm(y):
        q_ref[0] = (y * dk ** -0.5).astype(BF16)
        nb = d_a // blk
        mu = sum(jnp.sum(gv[j], axis=-1, keepdims=True) for j in range(nb)) * (1.0 / d_a)
        cen = [gv[j] - mu for j in range(nb)]
        var = sum(jnp.sum(c * c, axis=-1, keepdims=True) for c in cen) * (1.0 / d_a)
        inv = lax.rsqrt(var + EPS)
        for j in range(nb):
            vn[j] = (cen[j] * inv * lng_ref[:, cols(j)] + lnb_ref[:, cols(j)]).astype(BF16)
    s_q = add("q", lambda: proj(r_q, dk_tot), ep_q_and_layernorm)
    def ep_k(y):
        k_ref[0] = y.astype(BF16)
    s_k = add("k", lambda: proj(r_k, dk_tot), ep_k)
    s_r = [add(f"r{j}", lambda j=j: proj(r_r + j * blk, blk),
               store(sr_ref, j, lambda y: y * _sigmoid(y))) for j in range(dv_tot // blk)]
    s_v = [add(f"v{j}", lambda j=j: proj(r_v + j * blk, blk), store(v_ref, j, lambda y: y))
           for j in range(dv_tot // blk)]

    def mm_gate(ci, g):
        r0, c0 = ci * chunk_a, g * gw
        return _dot(ws_ref[g].astype(BF16),
                    vn[c0 // blk][r0:r0 + chunk_a, c0 % blk:c0 % blk + gw])

    def ep_gate(ci, g):
        r0, c0 = ci * chunk_a, g * gw

        def ep(s):
            u = gu[c0 // blk][r0:r0 + chunk_a, c0 % blk:c0 % blk + gw]
            ya_scr[r0:r0 + chunk_a, c0:c0 + gw] = (u * (s + bs_ref[g])).astype(BF16)
        return ep
    s_mix = [add(f"mix{ci}_{g}", lambda ci=ci, g=g: mm_gate(ci, g), ep_gate(ci, g))
             for ci in range(tm // chunk_a) for g in range(groups)]
    s_gb = [add(f"gb{j}", lambda j=j: proj(r_gb + j * blk, blk), store(gb_ref, j, _sigmoid))
            for j in range(d // blk)]
    s_ga = [add(f"ga{j}", lambda j=j: (proj(r_ga + j * blk, blk),
                                       _dot(ya_scr[...], pa_ref[:, cols(j)])),
                store(ma_ref, j, lambda y: _sigmoid(y[0]) * y[1])) for j in range(d // blk)]

    order = ([s_dec] + s_u + s_va + [s_q, s_k] + s_r + s_gb + s_mix + s_v[:1] + s_ga + s_v[1:])
    assert sorted(order) == sorted(stage)
    extra = {}
    for part, name in enumerate(order[-PREPARE_PARTS - 1:-1]):
        extra.setdefault(name, []).append(functools.partial(prepare_next, part))
    results = {}
    for i in range(len(order) + STAGE_LOOKAHEAD):
        if i < len(order):
            results[order[i]] = stage[order[i]][0]()
        if i >= STAGE_LOOKAHEAD:
            done = order[i - STAGE_LOOKAHEAD]
            stage[done][1](results.pop(done))
            for work in extra.get(done, ()):
                work()


def _inproj_call(x4, pos, mod, consts, weights, later_weights, seg, d_a, dk_tot, dv_tot):
    bsz, nrow, gw_, d = x4.shape
    t = nrow * gw_
    tm = TOKEN_TILE
    nt = t // tm
    rows_per_tile = tm // GRID_W
    rowemb, colemb = pos
    chunk_a = consts[5].shape[1]
    assert tm % chunk_a == 0 and tm % GRID_W == 0 and t % tm == 0

    n_tiles = bsz * nt
    preparing = lambda s: jnp.minimum(s, n_tiles - 1)
    projecting = lambda s: jnp.maximum(s - 1, 0)

    def slab(w):
        rows = w.shape[0] // n_tiles
        assert w.shape[0] % n_tiles == 0 and rows % BF16_SUBLANES == 0
        return pl.BlockSpec((rows, w.shape[1]), lambda s: (projecting(s), 0))

    tok = lambda w: pl.BlockSpec((1, tm, w), lambda s: (projecting(s) // nt, projecting(s) % nt, 0))
    in_specs = ([pl.BlockSpec((1, rows_per_tile, GRID_W, d),
                              lambda s: (preparing(s) // nt, preparing(s) % nt, 0, 0)),
                 pl.BlockSpec((rows_per_tile, 1, d // 2), lambda s: (preparing(s) % nt, 0, 0)),
                 _const_spec(colemb.shape), _mod_spec(mod, "shift1"), _mod_spec(mod, "scale1")]
                + [_const_spec(a.shape) for a in consts]
                + [pl.BlockSpec(memory_space=pl.ANY)] * len(weights)
                + [slab(w) for w in later_weights])
    widths = [d, d, dk_tot, dk_tot, dv_tot, dv_tot, dk_tot, dk_tot]
    dtypes = [BF16] * 8
    kern = functools.partial(_inproj_kernel, seg=seg, d_a=d_a, dk_tot=dk_tot, dv_tot=dv_tot,
                             n_later=len(later_weights), nt=nt)
    return pl.pallas_call(
        kern,
        out_shape=([jax.ShapeDtypeStruct((bsz, t, w), dt) for w, dt in zip(widths, dtypes)]
                   + [jax.ShapeDtypeStruct(w.shape, BF16) for w in later_weights]),
        grid=(n_tiles + 1,),
        in_specs=in_specs,
        out_specs=[tok(w) for w in widths] + [slab(w) for w in later_weights],
        scratch_shapes=([pltpu.VMEM((tm, d_a), BF16), pltpu.VMEM((2, tm, d), BF16)]
                        + [pltpu.VMEM(w.shape, BF16) for w in weights]
                        + [pltpu.VMEM((WEIGHT_STAGE_SLOTS, WEIGHT_STAGE_ROWS, d), F32),
                           pltpu.SemaphoreType.DMA((WEIGHT_STAGE_SLOTS,))]),
        compiler_params=pltpu.CompilerParams(dimension_semantics=("arbitrary",),
                                             vmem_limit_bytes=VMEM_LIMIT),
        name="inproj",
    )(x4, rowemb, colemb, mod, mod, *consts, *weights, *later_weights)


def _scan_masks(c):
    row = lax.broadcasted_iota(jnp.int32, (c, c), 0)
    col = lax.broadcasted_iota(jnp.int32, (c, c), 1)
    return ((row >= col, c - 1), (row <= col, 0))


def _chunk_cumsum(lg, keep, end):
    b = _dot(jnp.where(keep, 1.0, 0.0).astype(BF16), lg)
    return b, b[end:end + 1, :]


def _decay_rows(st, tot):
    dk, dv = st.shape
    dcol = jnp.transpose(jnp.broadcast_to(jnp.exp(tot), (dk, dk)))
    return st * jnp.concatenate([dcol] * (dv // dk), axis=1)


def _scan_kernel(qf_ref, kf_ref, vf_ref, lgf_ref, qb_ref, kb_ref, vb_ref, lgb_ref, s0_ref,
                 of_ref, ob_ref, st_scr, q32_scr, k32_scr, v32_scr, b_scr, o_scr,
                 *, dk, dv):
    t = pl.program_id(0)
    bsz, c = qf_ref.shape[0], SCAN_CHUNK
    n_sub = qf_ref.shape[1] // c

    def rows(di, sub):
        first = (sub if di == 0 else n_sub - 1 - sub) * c
        return slice(first, first + c)

    @pl.when(t == 0)
    def _():
        st_scr[...] = s0_ref[...]

    masks = _scan_masks(c)
    dirs = ((qf_ref, kf_ref, vf_ref, lgf_ref, of_ref) + masks[0],
            (qb_ref, kb_ref, vb_ref, lgb_ref, ob_ref) + masks[1])

    worst = None
    for lg_ref in (lgf_ref, lgb_ref):
        for sub in range(n_sub):
            w = jnp.max(-jnp.sum(lg_ref[:, sub * c:(sub + 1) * c, :].astype(F32), axis=1))
            worst = w if worst is None else jnp.maximum(worst, w)

    cums = {}

    def cumulate(bi, di, sub):
        lg_ref, keep, end = dirs[di][3], dirs[di][5], dirs[di][6]
        cums[bi, di, sub] = _chunk_cumsum(lg_ref[bi, rows(di, sub), :], keep, end)

    def load(bi, di, h, sub):
        q_ref, k_ref, v_ref = dirs[di][:3]
        b, tot = cums[bi, di, sub]
        sl = slice(h * dk, (h + 1) * dk)
        return (q_ref[bi, rows(di, sub), sl].astype(F32), k_ref[bi, rows(di, sub), sl].astype(F32),
                v_ref[bi, rows(di, sub), h * dv:(h + 1) * dv], b[:, sl], tot[:, sl])

    def decayed_state(bi, di, h, tot):
        return _decay_rows(st_scr[bi, di, h], tot)

    def fast_step():
        chains = [(bi, di, h, sub) for sub in range(n_sub) for bi in range(bsz)
                  for di in range(2) for h in range(GLA_HEADS)]
        live = {}

        def factors(n):
            q, k, v, bh, tot = load(*chains[n])
            half = 0.5 * tot
            eh = jnp.exp(half)
            qe = q * jnp.exp(bh - half)
            ke = k * jnp.exp(half - bh)
            live[n] = dict(v=v, tot=tot, qe=qe.astype(BF16), ke=ke.astype(BF16),
                           qd=(qe * eh).astype(BF16), kd=(ke * eh).astype(BF16))

        def scores(n):
            live[n]["a"] = _dot_nt(live[n].pop("qe"), live[n].pop("ke"))

        def outputs(n):
            bi, di, h, _ = chains[n]
            c_ = live[n]
            a = jnp.where(dirs[di][5], c_.pop("a"), 0.0).astype(BF16)
            lhs = jnp.concatenate([a, c_.pop("qd")], axis=1)
            rhs = jnp.concatenate([c_["v"], st_scr[bi, di, h].astype(BF16)], axis=0)
            c_["o"] = _dot(lhs, rhs)
            c_["u"] = _dot_tn(c_.pop("kd"), c_.pop("v"))

        def finish(n):
            bi, di, h, sub = chains[n]
            c_ = live.pop(n)
            o_ref = dirs[di][4]
            o_ref[bi, rows(di, sub), h * dv:(h + 1) * dv] = c_["o"].astype(o_ref.dtype)
            st_scr[bi, di, h] = decayed_state(bi, di, h, c_["tot"]) + c_["u"]

        def cumulate_for(n):
            bi, di, _, sub = chains[n]
            cumulate(bi, di, sub)

        phases = ((factors, 0), (scores, 1), (outputs, 3), (finish, 5))
        cumulate_for(0)
        for i in range(len(chains) + phases[-1][1]):
            ahead = i + GLA_HEADS
            if i % GLA_HEADS == 0 and ahead < len(chains):
                cumulate_for(ahead)
            for phase, lag in phases:
                if 0 <= i - lag < len(chains):
                    phase(i - lag)

    def safe_step():
        ridx = lax.broadcasted_iota(jnp.int32, (c, 1), 0)
        groups = [(s, b, d) for s in range(n_sub) for b in range(bsz) for d in range(2)]
        for sub, bi, di in groups:
            o_ref = dirs[di][4]
            cumulate(bi, di, sub)
            for h in range(GLA_HEADS):
                q, k, v, bh, tot = load(bi, di, h, sub)
                q32_scr[...] = q
                k32_scr[...] = k
                v32_scr[...] = v.astype(F32)
                b_scr[...] = bh
                o_scr[...] = _dot((q * jnp.exp(bh)).astype(BF16), st_scr[bi, di, h].astype(BF16))
                st_scr[bi, di, h] = (decayed_state(bi, di, h, tot)
                                     + _dot_tn((k * jnp.exp(tot - bh)).astype(BF16), v))

                def body(j, carry):
                    bj = b_scr[pl.ds(j, 1), :]
                    valid = (ridx >= j) if di == 0 else (ridx <= j)
                    e = jnp.exp(jnp.where(valid, b_scr[...] - bj, -jnp.inf))
                    w = jnp.sum(q32_scr[...] * e * k32_scr[pl.ds(j, 1), :], axis=-1, keepdims=True)
                    o_scr[...] += w * v32_scr[pl.ds(j, 1), :]
                    return carry

                lax.fori_loop(0, c, body, 0)
                o_ref[bi, rows(di, sub), h * dv:(h + 1) * dv] = o_scr[...].astype(o_ref.dtype)

    lax.cond(worst <= SAFE_DECAY, fast_step, safe_step)


def _scan_call(q, k, v, lgf, lgb, s0):
    bsz, t, dk_tot = q.shape
    dv_tot = v.shape[-1]
    dk, dv = dk_tot // GLA_HEADS, dv_tot // GLA_HEADS
    c = SCAN_CHUNK
    tile = SCAN_TILE
    nt = t // tile
    assert t % tile == 0 and tile % c == 0 and dv % dk == 0
    fwd = lambda w: pl.BlockSpec((bsz, tile, w), lambda i: (0, i, 0))
    bwd = lambda w: pl.BlockSpec((bsz, tile, w), lambda i: (0, nt - 1 - i, 0))
    st_spec = pl.BlockSpec(s0.shape, lambda i: (0,) * s0.ndim)
    kern = functools.partial(_scan_kernel, dk=dk, dv=dv)
    return pl.pallas_call(
        kern,
        out_shape=[jax.ShapeDtypeStruct((bsz, t, dv_tot), BF16),
                   jax.ShapeDtypeStruct((bsz, t, dv_tot), BF16)],
        grid=(nt,),
        in_specs=[fwd(dk_tot), fwd(dk_tot), fwd(dv_tot), fwd(dk_tot),
                  bwd(dk_tot), bwd(dk_tot), bwd(dv_tot), bwd(dk_tot), st_spec],
        out_specs=[fwd(dv_tot), bwd(dv_tot)],
        scratch_shapes=[pltpu.VMEM(s0.shape, F32),
                        pltpu.VMEM((c, dk), F32), pltpu.VMEM((c, dk), F32),
                        pltpu.VMEM((c, dv), F32), pltpu.VMEM((c, dk), F32),
                        pltpu.VMEM((c, dv), F32)],
        compiler_params=pltpu.CompilerParams(dimension_semantics=("arbitrary",),
                                             vmem_limit_bytes=VMEM_LIMIT),
        name="gla_scan",
    )(q, k, v, lgf, q, k, v, lgb, s0)


def _tail_kernel(x_ref, row_ref, col_ref, ma_ref, gb_ref, of_ref, ob_ref, sr_ref,
                 g1_ref, sh2_ref, sc2_ref, g2_ref, og_ref, n2_ref, fg_ref,
                 pb_ref, wo_ref, w1_ref, w2_ref, out_ref, *, dv):
    xp = _load_tokens(x_ref, row_ref, col_ref)
    mine = pl.ds(pl.program_id(0), 1)
    o = of_ref[0].astype(F32) + ob_ref[0].astype(F32)
    heads = []
    for h in range(GLA_HEADS):
        sl = slice(h * dv, (h + 1) * dv)
        heads.append(_rms(o[:, sl], og_ref[:, sl]))
    yb = (jnp.concatenate(heads, axis=-1) * sr_ref[0].astype(F32)).astype(BF16)
    merged = ma_ref[0].astype(F32) + gb_ref[0].astype(F32) * _dot(yb, pb_ref[...])
    x1 = xp + g1_ref[mine, :] * _dot(merged.astype(BF16), wo_ref[...])
    h2 = (_rms(x1, n2_ref[...]) * (1.0 + sc2_ref[mine, :]) + sh2_ref[mine, :]).astype(BF16)
    acc = jnp.zeros_like(x1)
    d_ff = w1_ref.shape[1]
    for j in range(d_ff // FF_COL_TILE):
        sl = slice(j * FF_COL_TILE, (j + 1) * FF_COL_TILE)
        f = jnp.maximum(_dot(h2, w1_ref[:, sl]), 0.0)
        acc = acc + _dot((f * f).astype(BF16), w2_ref[sl, :])
    x2 = x1 + g2_ref[mine, :] * acc
    out_ref[0] = _rms(x2, fg_ref[...])


def _tail_call(x4, pos, acts, mod, consts):
    bsz, nrow, gw_, d = x4.shape
    t = nrow * gw_
    tm = TOKEN_TILE
    rows_per_tile = tm // GRID_W
    rowemb, colemb = pos
    dv = acts[2].shape[-1] // GLA_HEADS
    tok = lambda w: pl.BlockSpec((1, tm, w), lambda b, i: (b, i, 0))
    mod_names = ("gate1", "shift2", "scale2", "gate2")
    in_specs = ([pl.BlockSpec((1, rows_per_tile, GRID_W, d), lambda b, i: (b, i, 0, 0)),
                 pl.BlockSpec((rows_per_tile, 1, d // 2), lambda b, i: (i, 0, 0)),
                 _const_spec(colemb.shape)]
                + [tok(a.shape[-1]) for a in acts]
                + [_mod_spec(mod, name) for name in mod_names]
                + [_const_spec(a.shape) for a in consts])
    return pl.pallas_call(
        functools.partial(_tail_kernel, dv=dv),
        out_shape=jax.ShapeDtypeStruct((bsz, t, d), F32),
        grid=(bsz, t // tm),
        in_specs=in_specs,
        out_specs=tok(d),
        compiler_params=pltpu.CompilerParams(dimension_semantics=("arbitrary", "arbitrary"),
                                             vmem_limit_bytes=VMEM_LIMIT),
        name="tail",
    )(x4, rowemb, colemb, *acts, *[mod] * len(mod_names), *consts)


def _pos_tables(t, d):
    quarter = d // 4
    one, base = np.float32(1.0), np.float32(10000.0)
    freqs = one / (base ** (np.arange(quarter, dtype=np.float32) / np.float32(quarter)))

    def emb(n):
        a = np.arange(n, dtype=np.float32)[:, None] * freqs[None, :]
        return np.concatenate([np.sin(a), np.cos(a)], -1).astype(np.float32)

    return jnp.asarray(emb(t // GRID_W)[:, None, :]), jnp.asarray(emb(GRID_W)[None, :, :])


def kernel(x, c, ctx, c_ctx, w_ada, b_ada, norm1_g, w_in, lnv_g, lnv_b, w_s, b_s, w_dec2, b_dec,
           onorm_g, proj_a, proj_b, w_out, norm2_g, w_ff1, w_ff2, final_g):
    bsz, t, d = x.shape
    depth = w_in.shape[0]
    assert depth == 1, "single-layer problem: the context stream only feeds the scan states"
    d_a = proj_a.shape[1]
    dv_tot = proj_b.shape[1]
    dk_tot = w_dec2.shape[-1]
    seg = _segment_rows(d, d_a, dk_tot, dv_tot, w_dec2.shape[2])

    mod = _mod_call(c, c_ctx, w_ada[0], b_ada[0])

    row = lambda a: a.reshape(1, -1)
    g1n = row(norm1_g[0])
    w_in_t = jnp.swapaxes(w_in[0], 0, 1)
    wd2, bd = w_dec2[0], b_dec[0][:, None, :]

    s_ctx = _ctx_state_call(ctx, mod, bsz, g1n, w_in_t, wd2, bd, seg, dk_tot, dv_tot)

    pos = _pos_tables(t, d)
    x4 = x.reshape(bsz, t // GRID_W, GRID_W, d)
    consts = (g1n, wd2, bd, row(lnv_g[0]), row(lnv_b[0]), w_s[0], b_s[0][:, :, None])
    ma, gb, q, k, v, sr, lgf, lgb, *tail_weights = _inproj_call(
        x4, pos, mod, consts, (w_in_t, proj_a[0]),
        (proj_b[0], w_out[0], w_ff1[0], w_ff2[0]), seg, d_a, dk_tot, dv_tot)
    o_f, o_b = _scan_call(q, k, v, lgf, lgb, s_ctx)

    return _tail_call(x4, pos, (ma, gb, o_f, o_b, sr), mod,
                      (row(onorm_g[0]), row(norm2_g[0]), row(final_g), *tail_weights))
```

```python
import functools
import math

import jax
import jax.numpy as jnp
import numpy as np
from jax import lax
from jax.experimental import pallas as pl
from jax.experimental.pallas import tpu as pltpu

F32 = jnp.float32
BF16 = jnp.bfloat16

GRID_W = 64
GLA_HEADS = 4
GATE_TAU = 16.0
EPS = 1e-6

TOKEN_TILE = 512
SCAN_CHUNK = 128
SCAN_TILE = 512
MOD_ROWS = 8
MOD_COL_TILE = 1024
PROJ_COL_TILE = 512
PREPARE_PARTS = 4
STAGE_LOOKAHEAD = 1
WEIGHT_STAGE_ROWS = 512
WEIGHT_STAGE_SLOTS = 2
BF16_SUBLANES = 16
FF_COL_TILE = 1024
SAFE_DECAY = 60.0
VMEM_LIMIT = 56 * 1024 * 1024


def _sigmoid(x):
    return 1.0 / (1.0 + jnp.exp(-x))


def _gelu_tanh(x):
    c = math.sqrt(2.0 / math.pi)
    return 0.5 * x * (1.0 + jnp.tanh(c * (x + 0.044715 * (x * x * x))))


def _log_sigmoid(z):
    return jnp.minimum(z, 0.0) - jnp.log(1.0 + jnp.exp(-jnp.abs(z)))


def _rms(x, g):
    return x * lax.rsqrt(jnp.mean(x * x, axis=-1, keepdims=True) + EPS) * g


def _split_bf16(a):
    hi = a.astype(BF16)
    lo = (a - hi.astype(F32)).astype(BF16)
    return hi, lo


def _dot(a, b):
    return jnp.dot(a, b, preferred_element_type=F32)


def _dot_nt(a, b):
    return lax.dot_general(a, b, (((1,), (1,)), ((), ())), preferred_element_type=F32)


def _dot_tn(a, b):
    return lax.dot_general(a, b, (((0,), (0,)), ((), ())), preferred_element_type=F32)


def _const_spec(shape):
    nd = len(shape)
    return pl.BlockSpec(shape, lambda *_: (0,) * nd, pipeline_mode=pl.Buffered(1))


MOD_COLUMNS = ("shift1", "scale1", "gate1", "shift2", "scale2", "gate2")


def _mod_spec(mod, name):
    d = mod.shape[1] // len(MOD_COLUMNS)
    j = MOD_COLUMNS.index(name)
    return pl.BlockSpec((mod.shape[0], d), lambda *_: (0, j), pipeline_mode=pl.Buffered(1))


def _load_as_bf16(src_ref, dst_ref, stage_ref, sem_ref, rows):
    n_rows, cols = src_ref.shape
    slots = stage_ref.shape[0]
    assert n_rows % rows == 0 and rows <= stage_ref.shape[1] and cols == stage_ref.shape[2]
    n = n_rows // rows

    def copy(j, slot):
        return pltpu.make_async_copy(src_ref.at[pl.ds(j * rows, rows), :],
                                     stage_ref.at[slot, pl.ds(0, rows), :], sem_ref.at[slot])

    for j in range(min(slots - 1, n)):
        copy(j, j).start()

    def body(j, carry):
        ahead = j + slots - 1

        @pl.when(ahead < n)
        def _():
            copy(ahead, ahead % slots).start()

        copy(j, j % slots).wait()
        first = pl.multiple_of(j * rows, rows)
        dst_ref[pl.ds(first, rows), :] = stage_ref[j % slots, pl.ds(0, rows), :].astype(BF16)
        return carry

    lax.fori_loop(0, n, body, 0)


def _stage_rows(n_rows):
    return max(r for r in range(BF16_SUBLANES, WEIGHT_STAGE_ROWS + 1, BF16_SUBLANES)
               if n_rows % r == 0)


def _mod_kernel(c_ref, cc_ref, w_ref, b_ref, o_ref):
    rows, bsz = o_ref.shape[0], c_ref.shape[0]
    row_id = lax.broadcasted_iota(jnp.int32, (rows, c_ref.shape[1]), 0)
    c = jnp.where(row_id == bsz, cc_ref[...], 0.0)
    for j in range(bsz):
        c = jnp.where(row_id == j, c_ref[j:j + 1, :], c)
    a_hi, a_lo = _split_bf16(c * _sigmoid(c))
    w_hi, w_lo = _split_bf16(w_ref[...])
    both = _dot(jnp.concatenate([a_hi, a_lo], axis=0), w_hi)
    o_ref[...] = both[:rows] + both[rows:] + _dot(a_hi, w_lo) + b_ref[...]


def _mod_call(c, c_ctx, w_ada, b_ada):
    d, n = w_ada.shape
    assert c.shape[0] < MOD_ROWS
    return pl.pallas_call(
        _mod_kernel,
        out_shape=jax.ShapeDtypeStruct((MOD_ROWS, n), F32),
        grid=(n // MOD_COL_TILE,),
        in_specs=[pl.BlockSpec(c.shape, lambda j: (0, 0)),
                  pl.BlockSpec((1, d), lambda j: (0, 0)),
                  pl.BlockSpec((d, MOD_COL_TILE), lambda j: (0, j)),
                  pl.BlockSpec((1, MOD_COL_TILE), lambda j: (0, j))],
        out_specs=pl.BlockSpec((MOD_ROWS, MOD_COL_TILE), lambda j: (0, j)),
        compiler_params=pltpu.CompilerParams(dimension_semantics=("arbitrary",)),
        name="adaln_mod",
    )(c, c_ctx.reshape(1, d), w_ada, b_ada.reshape(1, n))


def _load_tokens(x_ref, row_ref, col_ref):
    x = x_ref[0]
    half = x.shape[-1] // 2
    x = jnp.concatenate([x[..., :half] + row_ref[...], x[..., half:] + col_ref[...]], axis=-1)
    return x.reshape(x.shape[0] * x.shape[1], x.shape[2])


def _act_columns(d, dk_tot, dv_tot):
    widths = dict(ma=d, gb=d, sr=dv_tot, v=dv_tot, q=dk_tot, k=dk_tot, lgf=dk_tot, lgb=dk_tot)
    names = sorted(widths, key=lambda n: -widths[n])
    cols, c0 = {}, 0
    for name in names:
        assert c0 % widths[name] == 0
        cols[name] = (c0, widths[name])
        c0 += widths[name]
    return cols


def _act_spec(cols, name, rows, index_map):
    c0, width = cols[name]
    return pl.BlockSpec((*rows, width), lambda *i: (*index_map(*i), c0 // width))


def _segment_rows(d, d_a, dk_tot, dv_tot, rank):
    names = ("u", "va", "q", "k", "v", "r", "dec", "ga", "gb")
    widths = (d_a, d_a, dk_tot, dk_tot, dv_tot, dv_tot, 2 * rank, d, d)
    offsets, r0 = {}, 0
    for name, w in zip(names, widths):
        offsets[name] = r0
        r0 += w
    return offsets


def _store_log_decay(dec, wdec2_ref, bdec_ref, di, out_ref):
    rank = wdec2_ref.shape[1]
    z = _dot(dec[:, di * rank:(di + 1) * rank].astype(BF16),
             wdec2_ref[di].astype(BF16)) + bdec_ref[di]
    out_ref[0] = (_log_sigmoid(z) * (1.0 / GATE_TAU)).astype(out_ref.dtype)


def _ctx_state_kernel(x_ref, sh_ref, sc_ref, g1_ref, wk_ref, wv_ref, wd_ref, wdec2_ref, bdec_ref,
                      s_ref, k_scr, v_scr, lgf_scr, lgb_scr, *, mod_row):
    bsz, t, d = x_ref.shape
    c = SCAN_CHUNK
    n_sub = t // c
    dk, dv = s_ref.shape[-2:]
    hb = (_rms(x_ref[...].reshape(bsz * t, d), g1_ref[...]) * (1.0 + sc_ref[mod_row:mod_row + 1, :])
          + sh_ref[mod_row:mod_row + 1, :]).astype(BF16)
    proj = lambda w_ref: _dot_nt(hb, w_ref[...].astype(BF16))
    dec = proj(wd_ref)
    for di, lg_scr in enumerate((lgf_scr, lgb_scr)):
        _store_log_decay(dec, wdec2_ref, bdec_ref, di, lg_scr)
    k_scr[...] = proj(wk_ref).astype(BF16)
    v_scr[...] = proj(wv_ref).astype(BF16)

    s_ref[...] = jnp.zeros_like(s_ref)
    masks = _scan_masks(c)
    for sub in range(n_sub):
        for bi in range(bsz):
            for di, lg_scr in enumerate((lgf_scr, lgb_scr)):
                first = bi * t + (sub if di == 0 else n_sub - 1 - sub) * c
                b, tot = _chunk_cumsum(lg_scr[0, first:first + c, :], *masks[di])
                for h in range(GLA_HEADS):
                    sl = slice(h * dk, (h + 1) * dk)
                    k = k_scr[first:first + c, sl].astype(F32)
                    kd = (k * jnp.exp(tot[:, sl] - b[:, sl])).astype(BF16)
                    v = v_scr[first:first + c, h * dv:(h + 1) * dv]
                    s_ref[bi, di, h] = _decay_rows(s_ref[bi, di, h], tot[:, sl]) + _dot_tn(kd, v)


def _ctx_state_call(ctx, mod, mod_row, g1, w_in_t, w_dec2, b_dec, seg, dk_tot, dv_tot):
    bsz, t, d = ctx.shape
    dk, dv = dk_tot // GLA_HEADS, dv_tot // GLA_HEADS
    assert t % SCAN_CHUNK == 0 and dv % dk == 0

    def rows_spec(r0, width):
        assert r0 % width == 0
        return pl.BlockSpec((width, d), lambda i: (r0 // width, 0))

    full = lambda a: pl.BlockSpec(a.shape, lambda i: (0,) * a.ndim)
    rank = w_dec2.shape[1]
    shape = (bsz, 2, GLA_HEADS, dk, dv)
    return pl.pallas_call(
        functools.partial(_ctx_state_kernel, mod_row=mod_row),
        out_shape=jax.ShapeDtypeStruct(shape, F32),
        grid=(1,),
        in_specs=[full(ctx), _mod_spec(mod, "shift1"), _mod_spec(mod, "scale1"), full(g1),
                  rows_spec(seg["k"], dk_tot), rows_spec(seg["v"], dv_tot),
                  rows_spec(seg["dec"], 2 * rank), full(w_dec2), full(b_dec)],
        out_specs=pl.BlockSpec(shape, lambda i: (0,) * len(shape)),
        scratch_shapes=[pltpu.VMEM((bsz * t, dk_tot), BF16), pltpu.VMEM((bsz * t, dv_tot), BF16),
                        pltpu.VMEM((1, bsz * t, dk_tot), BF16),
                        pltpu.VMEM((1, bsz * t, dk_tot), BF16)],
        compiler_params=pltpu.CompilerParams(dimension_semantics=("arbitrary",),
                                             vmem_limit_bytes=VMEM_LIMIT),
        name="ctx_state",
    )(ctx, mod, mod, g1, w_in_t, w_in_t, w_in_t, w_dec2, b_dec)


def _inproj_kernel(x_ref, row_ref, col_ref, sh_ref, sc_ref, g1_ref, wdec2_ref, bdec_ref,
                   lng_ref, lnb_ref, ws_ref, bs_ref, wt_hbm, pa_hbm, *rest,
                   seg, d_a, dk_tot, dv_tot, n_later, nt):
    later_f32, (act_ref, *rest) = rest[:n_later], rest[n_later:]
    later_bf16, (ya_scr, hb_scr, wt_ref, pa_ref, stage_ref, sem_ref) = rest[:n_later], rest[n_later:]
    step = pl.program_id(0)
    _, tm, d = hb_scr.shape
    cols = _act_columns(d, dk_tot, dv_tot)
    ma_ref, gb_ref, q_ref, k_ref, v_ref, sr_ref, lgf_ref, lgb_ref = (
        act_ref.at[:, :, pl.ds(*cols[name])]
        for name in ("ma", "gb", "q", "k", "v", "sr", "lgf", "lgb"))

    grid_rows = x_ref.shape[1]

    def prepare_next(part):
        n = grid_rows // PREPARE_PARTS
        xp = _load_tokens(x_ref.at[:, pl.ds(part * n, n)], row_ref.at[pl.ds(part * n, n)], col_ref)
        batch = jnp.minimum(step, pl.num_programs(0) - 2) // nt
        mine = pl.ds(batch, 1)
        hb = _rms(xp, g1_ref[...]) * (1.0 + sc_ref[mine, :]) + sh_ref[mine, :]
        hb_scr[step % 2, pl.ds(part * n * GRID_W, n * GRID_W), :] = hb.astype(BF16)

    def warm_up():
        for src, dst in ((wt_hbm, wt_ref), (pa_hbm, pa_ref)):
            _load_as_bf16(src, dst, stage_ref, sem_ref, _stage_rows(src.shape[0]))
        for part in range(PREPARE_PARTS):
            prepare_next(part)

    lax.cond(step == 0, warm_up,
             functools.partial(_project_tile, prepare_next, rest_refs=(
                 wdec2_ref, bdec_ref, lng_ref, lnb_ref, ws_ref, bs_ref, later_f32, later_bf16,
                 ma_ref, gb_ref, q_ref, k_ref, v_ref, sr_ref, lgf_ref, lgb_ref,
                 ya_scr, hb_scr, wt_ref, pa_ref),
                 seg=seg, d_a=d_a, dk_tot=dk_tot, dv_tot=dv_tot))


def _project_tile(prepare_next, *, rest_refs, seg, d_a, dk_tot, dv_tot):
    (wdec2_ref, bdec_ref, lng_ref, lnb_ref, ws_ref, bs_ref, later_f32, later_bf16,
     ma_ref, gb_ref, q_ref, k_ref, v_ref, sr_ref, lgf_ref, lgb_ref,
     ya_scr, hb_scr, wt_ref, pa_ref) = rest_refs
    _, tm, d = hb_scr.shape
    ready = (pl.program_id(0) + 1) % 2

    for src, dst in zip(later_f32, later_bf16):
        dst[...] = src[...].astype(BF16)

    def proj(r0, width):
        return _dot_nt(hb_scr[ready], wt_ref[r0:r0 + width, :])

    r_u, r_va, r_q, r_k, r_v, r_r = (seg[n] for n in ("u", "va", "q", "k", "v", "r"))
    r_ga, r_gb = seg["ga"], seg["gb"]
    rank = wdec2_ref.shape[1]
    groups, chunk_a = ws_ref.shape[0], ws_ref.shape[1]
    dk = dk_tot // GLA_HEADS
    gw = d_a // groups
    blk = PROJ_COL_TILE
    cols = lambda j: slice(j * blk, (j + 1) * blk)
    gu, gv, vn = {}, {}, {}

    stage = {}

    def store(out_ref, j, fn):
        def ep(y):
            out_ref[0, :, cols(j)] = fn(y).astype(out_ref.dtype)
        return ep

    def keep(table, j, fn):
        def ep(y):
            table[j] = fn(y)
        return ep

    def add(name, matmul, epilogue):
        stage[name] = (matmul, epilogue)
        return name

    def ep_dec(dec):
        for di, out_ref in enumerate((lgf_ref, lgb_ref)):
            _store_log_decay(dec, wdec2_ref, bdec_ref, di, out_ref)
    s_dec = add("dec", lambda: proj(seg["dec"], 2 * rank), ep_dec)
    s_u = [add(f"u{j}", lambda j=j: proj(r_u + j * blk, blk), keep(gu, j, _gelu_tanh))
           for j in range(d_a // blk)]
    s_va = [add(f"va{j}", lambda j=j: proj(r_va + j * blk, blk), keep(gv, j, _gelu_tanh))
            for j in range(d_a // blk)]

    def ep_q_and_layernorm(y):
        q_ref[0] = (y * dk ** -0.5).astype(BF16)
        nb = d_a // blk
        mu = sum(jnp.sum(gv[j], axis=-1, keepdims=True) for j in range(nb)) * (1.0 / d_a)
        cen = [gv[j] - mu for j in range(nb)]
        var = sum(jnp.sum(c * c, axis=-1, keepdims=True) for c in cen) * (1.0 / d_a)
        inv = lax.rsqrt(var + EPS)
        for j in range(nb):
            vn[j] = (cen[j] * inv * lng_ref[:, cols(j)] + lnb_ref[:, cols(j)]).astype(BF16)
    s_q = add("q", lambda: proj(r_q, dk_tot), ep_q_and_layernorm)
    def ep_k(y):
        k_ref[0] = y.astype(BF16)
    s_k = add("k", lambda: proj(r_k, dk_tot), ep_k)
    s_r = [add(f"r{j}", lambda j=j: proj(r_r + j * blk, blk),
               store(sr_ref, j, lambda y: y * _sigmoid(y))) for j in range(dv_tot // blk)]
    s_v = [add(f"v{j}", lambda j=j: proj(r_v + j * blk, blk), store(v_ref, j, lambda y: y))
           for j in range(dv_tot // blk)]

    def mm_gate(ci, g):
        r0, c0 = ci * chunk_a, g * gw
        return _dot(ws_ref[g].astype(BF16),
                    vn[c0 // blk][r0:r0 + chunk_a, c0 % blk:c0 % blk + gw])

    def ep_gate(ci, g):
        r0, c0 = ci * chunk_a, g * gw

        def ep(s):
            u = gu[c0 // blk][r0:r0 + chunk_a, c0 % blk:c0 % blk + gw]
            ya_scr[r0:r0 + chunk_a, c0:c0 + gw] = (u * (s + bs_ref[g])).astype(BF16)
        return ep
    s_mix = [add(f"mix{ci}_{g}", lambda ci=ci, g=g: mm_gate(ci, g), ep_gate(ci, g))
             for ci in range(tm // chunk_a) for g in range(groups)]
    s_gb = [add(f"gb{j}", lambda j=j: proj(r_gb + j * blk, blk), store(gb_ref, j, _sigmoid))
            for j in range(d // blk)]
    s_ga = [add(f"ga{j}", lambda j=j: (proj(r_ga + j * blk, blk),
                                       _dot(ya_scr[...], pa_ref[:, cols(j)])),
                store(ma_ref, j, lambda y: _sigmoid(y[0]) * y[1])) for j in range(d // blk)]

    order = ([s_dec] + s_u + s_va + [s_q, s_k] + s_r + s_gb + s_mix + s_v[:1] + s_ga + s_v[1:])
    assert sorted(order) == sorted(stage)
    extra = {}
    for part, name in enumerate(order[-PREPARE_PARTS - 1:-1]):
        extra.setdefault(name, []).append(functools.partial(prepare_next, part))
    results = {}
    for i in range(len(order) + STAGE_LOOKAHEAD):
        if i < len(order):
            results[order[i]] = stage[order[i]][0]()
        if i >= STAGE_LOOKAHEAD:
            done = order[i - STAGE_LOOKAHEAD]
            stage[done][1](results.pop(done))
            for work in extra.get(done, ()):
                work()


def _inproj_call(x4, pos, mod, consts, weights, later_weights, seg, d_a, dk_tot, dv_tot):
    bsz, nrow, gw_, d = x4.shape
    t = nrow * gw_
    tm = TOKEN_TILE
    nt = t // tm
    rows_per_tile = tm // GRID_W
    rowemb, colemb = pos
    chunk_a = consts[5].shape[1]
    assert tm % chunk_a == 0 and tm % GRID_W == 0 and t % tm == 0

    n_tiles = bsz * nt
    preparing = lambda s: jnp.minimum(s, n_tiles - 1)
    projecting = lambda s: jnp.maximum(s - 1, 0)

    def slab(w):
        rows = w.shape[0] // n_tiles
        assert w.shape[0] % n_tiles == 0 and rows % BF16_SUBLANES == 0
        return pl.BlockSpec((rows, w.shape[1]), lambda s: (projecting(s), 0))

    tok = lambda w: pl.BlockSpec((1, tm, w), lambda s: (projecting(s) // nt, projecting(s) % nt, 0))
    in_specs = ([pl.BlockSpec((1, rows_per_tile, GRID_W, d),
                              lambda s: (preparing(s) // nt, preparing(s) % nt, 0, 0)),
                 pl.BlockSpec((rows_per_tile, 1, d // 2), lambda s: (preparing(s) % nt, 0, 0)),
                 _const_spec(colemb.shape), _mod_spec(mod, "shift1"), _mod_spec(mod, "scale1")]
                + [_const_spec(a.shape) for a in consts]
                + [pl.BlockSpec(memory_space=pl.ANY)] * len(weights)
                + [slab(w) for w in later_weights])
    act_width = sum(w for _, w in _act_columns(d, dk_tot, dv_tot).values())
    kern = functools.partial(_inproj_kernel, seg=seg, d_a=d_a, dk_tot=dk_tot, dv_tot=dv_tot,
                             n_later=len(later_weights), nt=nt)
    return pl.pallas_call(
        kern,
        out_shape=([jax.ShapeDtypeStruct((bsz, t, act_width), BF16)]
                   + [jax.ShapeDtypeStruct(w.shape, BF16) for w in later_weights]),
        grid=(n_tiles + 1,),
        in_specs=in_specs,
        out_specs=[tok(act_width)] + [slab(w) for w in later_weights],
        scratch_shapes=([pltpu.VMEM((tm, d_a), BF16), pltpu.VMEM((2, tm, d), BF16)]
                        + [pltpu.VMEM(w.shape, BF16) for w in weights]
                        + [pltpu.VMEM((WEIGHT_STAGE_SLOTS, WEIGHT_STAGE_ROWS, d), F32),
                           pltpu.SemaphoreType.DMA((WEIGHT_STAGE_SLOTS,))]),
        compiler_params=pltpu.CompilerParams(dimension_semantics=("arbitrary",),
                                             vmem_limit_bytes=VMEM_LIMIT),
        name="inproj",
    )(x4, rowemb, colemb, mod, mod, *consts, *weights, *later_weights)


def _scan_masks(c):
    row = lax.broadcasted_iota(jnp.int32, (c, c), 0)
    col = lax.broadcasted_iota(jnp.int32, (c, c), 1)
    return ((row >= col, c - 1), (row <= col, 0))


def _chunk_cumsum(lg, keep, end):
    b = _dot(jnp.where(keep, 1.0, 0.0).astype(BF16), lg)
    return b, b[end:end + 1, :]


def _decay_rows(st, tot):
    dk, dv = st.shape
    dcol = jnp.transpose(jnp.broadcast_to(jnp.exp(tot), (dk, dk)))
    return st * jnp.concatenate([dcol] * (dv // dk), axis=1)


def _scan_kernel(qf_ref, kf_ref, vf_ref, lgf_ref, qb_ref, kb_ref, vb_ref, lgb_ref, s0_ref,
                 of_ref, ob_ref, st_scr, q32_scr, k32_scr, v32_scr, b_scr, o_scr,
                 *, dk, dv):
    t = pl.program_id(0)
    bsz, c = qf_ref.shape[0], SCAN_CHUNK
    n_sub = qf_ref.shape[1] // c

    def rows(di, sub):
        first = (sub if di == 0 else n_sub - 1 - sub) * c
        return slice(first, first + c)

    @pl.when(t == 0)
    def _():
        st_scr[...] = s0_ref[...]

    masks = _scan_masks(c)
    dirs = ((qf_ref, kf_ref, vf_ref, lgf_ref, of_ref) + masks[0],
            (qb_ref, kb_ref, vb_ref, lgb_ref, ob_ref) + masks[1])

    worst = None
    for lg_ref in (lgf_ref, lgb_ref):
        for sub in range(n_sub):
            w = jnp.max(-jnp.sum(lg_ref[:, sub * c:(sub + 1) * c, :].astype(F32), axis=1))
            worst = w if worst is None else jnp.maximum(worst, w)

    cums = {}

    def cumulate(bi, di, sub):
        lg_ref, keep, end = dirs[di][3], dirs[di][5], dirs[di][6]
        cums[bi, di, sub] = _chunk_cumsum(lg_ref[bi, rows(di, sub), :], keep, end)

    def load(bi, di, h, sub):
        q_ref, k_ref, v_ref = dirs[di][:3]
        b, tot = cums[bi, di, sub]
        sl = slice(h * dk, (h + 1) * dk)
        return (q_ref[bi, rows(di, sub), sl].astype(F32), k_ref[bi, rows(di, sub), sl].astype(F32),
                v_ref[bi, rows(di, sub), h * dv:(h + 1) * dv], b[:, sl], tot[:, sl])

    def decayed_state(bi, di, h, tot):
        return _decay_rows(st_scr[bi, di, h], tot)

    def fast_step():
        chains = [(bi, di, h, sub) for sub in range(n_sub) for bi in range(bsz)
                  for di in range(2) for h in range(GLA_HEADS)]
        live = {}

        def factors(n):
            q, k, v, bh, tot = load(*chains[n])
            half = 0.5 * tot
            eh = jnp.exp(half)
            qe = q * jnp.exp(bh - half)
            ke = k * jnp.exp(half - bh)
            live[n] = dict(v=v, tot=tot, qe=qe.astype(BF16), ke=ke.astype(BF16),
                           qd=(qe * eh).astype(BF16), kd=(ke * eh).astype(BF16))

        def scores(n):
            live[n]["a"] = _dot_nt(live[n].pop("qe"), live[n].pop("ke"))

        def outputs(n):
            bi, di, h, _ = chains[n]
            c_ = live[n]
            a = jnp.where(dirs[di][5], c_.pop("a"), 0.0).astype(BF16)
            lhs = jnp.concatenate([a, c_.pop("qd")], axis=1)
            rhs = jnp.concatenate([c_["v"], st_scr[bi, di, h].astype(BF16)], axis=0)
            c_["o"] = _dot(lhs, rhs)
            c_["u"] = _dot_tn(c_.pop("kd"), c_.pop("v"))

        def finish(n):
            bi, di, h, sub = chains[n]
            c_ = live.pop(n)
            o_ref = dirs[di][4]
            o_ref[bi, rows(di, sub), h * dv:(h + 1) * dv] = c_["o"].astype(o_ref.dtype)
            st_scr[bi, di, h] = decayed_state(bi, di, h, c_["tot"]) + c_["u"]

        def cumulate_for(n):
            bi, di, _, sub = chains[n]
            cumulate(bi, di, sub)

        phases = ((factors, 0), (scores, 1), (outputs, 3), (finish, 5))
        cumulate_for(0)
        for i in range(len(chains) + phases[-1][1]):
            ahead = i + GLA_HEADS
            if i % GLA_HEADS == 0 and ahead < len(chains):
                cumulate_for(ahead)
            for phase, lag in phases:
                if 0 <= i - lag < len(chains):
                    phase(i - lag)

    def safe_step():
        ridx = lax.broadcasted_iota(jnp.int32, (c, 1), 0)
        groups = [(s, b, d) for s in range(n_sub) for b in range(bsz) for d in range(2)]
        for sub, bi, di in groups:
            o_ref = dirs[di][4]
            cumulate(bi, di, sub)
            for h in range(GLA_HEADS):
                q, k, v, bh, tot = load(bi, di, h, sub)
                q32_scr[...] = q
                k32_scr[...] = k
                v32_scr[...] = v.astype(F32)
                b_scr[...] = bh
                o_scr[...] = _dot((q * jnp.exp(bh)).astype(BF16), st_scr[bi, di, h].astype(BF16))
                st_scr[bi, di, h] = (decayed_state(bi, di, h, tot)
                                     + _dot_tn((k * jnp.exp(tot - bh)).astype(BF16), v))

                def body(j, carry):
                    bj = b_scr[pl.ds(j, 1), :]
                    valid = (ridx >= j) if di == 0 else (ridx <= j)
                    e = jnp.exp(jnp.where(valid, b_scr[...] - bj, -jnp.inf))
                    w = jnp.sum(q32_scr[...] * e * k32_scr[pl.ds(j, 1), :], axis=-1, keepdims=True)
                    o_scr[...] += w * v32_scr[pl.ds(j, 1), :]
                    return carry

                lax.fori_loop(0, c, body, 0)
                o_ref[bi, rows(di, sub), h * dv:(h + 1) * dv] = o_scr[...].astype(o_ref.dtype)

    lax.cond(worst <= SAFE_DECAY, fast_step, safe_step)


def _scan_call(acts, cols, s0):
    bsz, t, _ = acts.shape
    dk_tot, dv_tot = cols["q"][1], cols["v"][1]
    dk, dv = dk_tot // GLA_HEADS, dv_tot // GLA_HEADS
    c = SCAN_CHUNK
    tile = SCAN_TILE
    nt = t // tile
    assert t % tile == 0 and tile % c == 0 and dv % dk == 0
    fwd = lambda w: pl.BlockSpec((bsz, tile, w), lambda i: (0, i, 0))
    bwd = lambda w: pl.BlockSpec((bsz, tile, w), lambda i: (0, nt - 1 - i, 0))
    fwd_in = lambda name: _act_spec(cols, name, (bsz, tile), lambda i: (0, i))
    bwd_in = lambda name: _act_spec(cols, name, (bsz, tile), lambda i: (0, nt - 1 - i))
    kern = functools.partial(_scan_kernel, dk=dk, dv=dv)
    return pl.pallas_call(
        kern,
        out_shape=[jax.ShapeDtypeStruct((bsz, t, dv_tot), BF16),
                   jax.ShapeDtypeStruct((bsz, t, dv_tot), BF16)],
        grid=(nt,),
        in_specs=[fwd_in("q"), fwd_in("k"), fwd_in("v"), fwd_in("lgf"),
                  bwd_in("q"), bwd_in("k"), bwd_in("v"), bwd_in("lgb"), _const_spec(s0.shape)],
        out_specs=[fwd(dv_tot), bwd(dv_tot)],
        scratch_shapes=[pltpu.VMEM(s0.shape, F32),
                        pltpu.VMEM((c, dk), F32), pltpu.VMEM((c, dk), F32),
                        pltpu.VMEM((c, dv), F32), pltpu.VMEM((c, dk), F32),
                        pltpu.VMEM((c, dv), F32)],
        compiler_params=pltpu.CompilerParams(dimension_semantics=("arbitrary",),
                                             vmem_limit_bytes=VMEM_LIMIT),
        name="gla_scan",
    )(*[acts] * 8, s0)


def _tail_kernel(x_ref, row_ref, col_ref, ma_ref, gb_ref, of_ref, ob_ref, sr_ref,
                 g1_ref, sh2_ref, sc2_ref, g2_ref, og_ref, n2_ref, fg_ref,
                 pb_ref, wo_ref, w1_ref, w2_ref, out_ref, *, dv):
    xp = _load_tokens(x_ref, row_ref, col_ref)
    mine = pl.ds(pl.program_id(0), 1)
    o = of_ref[0].astype(F32) + ob_ref[0].astype(F32)
    heads = []
    for h in range(GLA_HEADS):
        sl = slice(h * dv, (h + 1) * dv)
        heads.append(_rms(o[:, sl], og_ref[:, sl]))
    yb = (jnp.concatenate(heads, axis=-1) * sr_ref[0].astype(F32)).astype(BF16)
    merged = ma_ref[0].astype(F32) + gb_ref[0].astype(F32) * _dot(yb, pb_ref[...])
    x1 = xp + g1_ref[mine, :] * _dot(merged.astype(BF16), wo_ref[...])
    h2 = (_rms(x1, n2_ref[...]) * (1.0 + sc2_ref[mine, :]) + sh2_ref[mine, :]).astype(BF16)
    acc = jnp.zeros_like(x1)
    d_ff = w1_ref.shape[1]
    for j in range(d_ff // FF_COL_TILE):
        sl = slice(j * FF_COL_TILE, (j + 1) * FF_COL_TILE)
        f = jnp.maximum(_dot(h2, w1_ref[:, sl]), 0.0)
        acc = acc + _dot((f * f).astype(BF16), w2_ref[sl, :])
    x2 = x1 + g2_ref[mine, :] * acc
    out_ref[0] = _rms(x2, fg_ref[...])


def _tail_call(x4, pos, acts, cols, o_f, o_b, mod, consts):
    bsz, nrow, gw_, d = x4.shape
    t = nrow * gw_
    tm = TOKEN_TILE
    rows_per_tile = tm // GRID_W
    rowemb, colemb = pos
    dv = o_f.shape[-1] // GLA_HEADS
    tok = lambda w: pl.BlockSpec((1, tm, w), lambda b, i: (b, i, 0))
    act = lambda name: _act_spec(cols, name, (1, tm), lambda b, i: (b, i))
    mod_names = ("gate1", "shift2", "scale2", "gate2")
    in_specs = ([pl.BlockSpec((1, rows_per_tile, GRID_W, d), lambda b, i: (b, i, 0, 0)),
                 pl.BlockSpec((rows_per_tile, 1, d // 2), lambda b, i: (i, 0, 0)),
                 _const_spec(colemb.shape),
                 act("ma"), act("gb"), tok(o_f.shape[-1]), tok(o_b.shape[-1]), act("sr")]
                + [_mod_spec(mod, name) for name in mod_names]
                + [_const_spec(a.shape) for a in consts])
    return pl.pallas_call(
        functools.partial(_tail_kernel, dv=dv),
        out_shape=jax.ShapeDtypeStruct((bsz, t, d), F32),
        grid=(bsz, t // tm),
        in_specs=in_specs,
        out_specs=tok(d),
        compiler_params=pltpu.CompilerParams(dimension_semantics=("arbitrary", "arbitrary"),
                                             vmem_limit_bytes=VMEM_LIMIT),
        name="tail",
    )(x4, rowemb, colemb, acts, acts, o_f, o_b, acts, *[mod] * len(mod_names), *consts)


def _pos_tables(t, d):
    quarter = d // 4
    one, base = np.float32(1.0), np.float32(10000.0)
    freqs = one / (base ** (np.arange(quarter, dtype=np.float32) / np.float32(quarter)))

    def emb(n):
        a = np.arange(n, dtype=np.float32)[:, None] * freqs[None, :]
        return np.concatenate([np.sin(a), np.cos(a)], -1).astype(np.float32)

    return jnp.asarray(emb(t // GRID_W)[:, None, :]), jnp.asarray(emb(GRID_W)[None, :, :])


def kernel(x, c, ctx, c_ctx, w_ada, b_ada, norm1_g, w_in, lnv_g, lnv_b, w_s, b_s, w_dec2, b_dec,
           onorm_g, proj_a, proj_b, w_out, norm2_g, w_ff1, w_ff2, final_g):
    bsz, t, d = x.shape
    depth = w_in.shape[0]
    assert depth == 1, "single-layer problem: the context stream only feeds the scan states"
    d_a = proj_a.shape[1]
    dv_tot = proj_b.shape[1]
    dk_tot = w_dec2.shape[-1]
    seg = _segment_rows(d, d_a, dk_tot, dv_tot, w_dec2.shape[2])

    mod = _mod_call(c, c_ctx, w_ada[0], b_ada[0])

    row = lambda a: a.reshape(1, -1)
    g1n = row(norm1_g[0])
    w_in_t = jnp.swapaxes(w_in[0], 0, 1)
    wd2, bd = w_dec2[0], b_dec[0][:, None, :]

    s_ctx = _ctx_state_call(ctx, mod, bsz, g1n, w_in_t, wd2, bd, seg, dk_tot, dv_tot)

    pos = _pos_tables(t, d)
    x4 = x.reshape(bsz, t // GRID_W, GRID_W, d)
    consts = (g1n, wd2, bd, row(lnv_g[0]), row(lnv_b[0]), w_s[0], b_s[0][:, :, None])
    acts, *tail_weights = _inproj_call(
        x4, pos, mod, consts, (w_in_t, proj_a[0]),
        (proj_b[0], w_out[0], w_ff1[0], w_ff2[0]), seg, d_a, dk_tot, dv_tot)
    cols = _act_columns(d, dk_tot, dv_tot)
    o_f, o_b = _scan_call(acts, cols, s_ctx)

    return _tail_call(x4, pos, acts, cols, o_f, o_b, mod,
                      (row(onorm_g[0]), row(norm2_g[0]), row(final_g), *tail_weights))
```

```python
import functools
import math

import jax
import jax.numpy as jnp
import numpy as np
from jax import lax
from jax.experimental import pallas as pl
from jax.experimental.pallas import tpu as pltpu

F32 = jnp.float32
BF16 = jnp.bfloat16

GRID_W = 64
GLA_HEADS = 4
GATE_TAU = 16.0
EPS = 1e-6

TOKEN_TILE = 512
SCAN_CHUNK = 128
SCAN_TILE = 512
MOD_ROWS = 8
MOD_COL_TILE = 1024
PROJ_COL_TILE = 512
PREPARE_PARTS = 4
STAGE_LOOKAHEAD = 1
WEIGHT_STAGE_ROWS = 512
WEIGHT_STAGE_SLOTS = 2
BF16_SUBLANES = 16
FF_COL_TILE = 1024
SAFE_DECAY = 60.0
VMEM_LIMIT = 56 * 1024 * 1024


def _sigmoid(x):
    return 1.0 / (1.0 + jnp.exp(-x))


def _gelu_tanh(x):
    c = math.sqrt(2.0 / math.pi)
    return 0.5 * x * (1.0 + jnp.tanh(c * (x + 0.044715 * (x * x * x))))


def _log_sigmoid(z):
    return jnp.minimum(z, 0.0) - jnp.log(1.0 + jnp.exp(-jnp.abs(z)))


def _rms(x, g):
    return x * lax.rsqrt(jnp.mean(x * x, axis=-1, keepdims=True) + EPS) * g


def _split_bf16(a):
    hi = a.astype(BF16)
    lo = (a - hi.astype(F32)).astype(BF16)
    return hi, lo


def _dot(a, b):
    return jnp.dot(a, b, preferred_element_type=F32)


def _dot_nt(a, b):
    return lax.dot_general(a, b, (((1,), (1,)), ((), ())), preferred_element_type=F32)


def _dot_tn(a, b):
    return lax.dot_general(a, b, (((0,), (0,)), ((), ())), preferred_element_type=F32)


def _const_spec(shape):
    nd = len(shape)
    return pl.BlockSpec(shape, lambda *_: (0,) * nd, pipeline_mode=pl.Buffered(1))


MOD_COLUMNS = ("shift1", "scale1", "gate1", "shift2", "scale2", "gate2")


def _mod_spec(mod, name):
    d = mod.shape[1] // len(MOD_COLUMNS)
    j = MOD_COLUMNS.index(name)
    return pl.BlockSpec((mod.shape[0], d), lambda *_: (0, j), pipeline_mode=pl.Buffered(1))


def _load_as_bf16(src_ref, dst_ref, stage_ref, sem_ref, rows):
    n_rows, cols = src_ref.shape
    slots = stage_ref.shape[0]
    assert n_rows % rows == 0 and rows <= stage_ref.shape[1] and cols == stage_ref.shape[2]
    n = n_rows // rows

    def copy(j, slot):
        return pltpu.make_async_copy(src_ref.at[pl.ds(j * rows, rows), :],
                                     stage_ref.at[slot, pl.ds(0, rows), :], sem_ref.at[slot])

    for j in range(min(slots - 1, n)):
        copy(j, j).start()

    def body(j, carry):
        ahead = j + slots - 1

        @pl.when(ahead < n)
        def _():
            copy(ahead, ahead % slots).start()

        copy(j, j % slots).wait()
        first = pl.multiple_of(j * rows, rows)
        dst_ref[pl.ds(first, rows), :] = stage_ref[j % slots, pl.ds(0, rows), :].astype(BF16)
        return carry

    lax.fori_loop(0, n, body, 0)


def _stage_rows(n_rows):
    return max(r for r in range(BF16_SUBLANES, WEIGHT_STAGE_ROWS + 1, BF16_SUBLANES)
               if n_rows % r == 0)


def _mod_kernel(c_ref, cc_ref, w_ref, b_ref, o_ref):
    rows, bsz = o_ref.shape[0], c_ref.shape[0]
    row_id = lax.broadcasted_iota(jnp.int32, (rows, c_ref.shape[1]), 0)
    c = jnp.where(row_id == bsz, cc_ref[...], 0.0)
    for j in range(bsz):
        c = jnp.where(row_id == j, c_ref[j:j + 1, :], c)
    a_hi, a_lo = _split_bf16(c * _sigmoid(c))
    w_hi, w_lo = _split_bf16(w_ref[...])
    both = _dot(jnp.concatenate([a_hi, a_lo], axis=0), w_hi)
    o_ref[...] = both[:rows] + both[rows:] + _dot(a_hi, w_lo) + b_ref[...]


def _mod_call(c, c_ctx, w_ada, b_ada):
    d, n = w_ada.shape
    assert c.shape[0] < MOD_ROWS
    return pl.pallas_call(
        _mod_kernel,
        out_shape=jax.ShapeDtypeStruct((MOD_ROWS, n), F32),
        grid=(n // MOD_COL_TILE,),
        in_specs=[pl.BlockSpec(c.shape, lambda j: (0, 0)),
                  pl.BlockSpec((1, d), lambda j: (0, 0)),
                  pl.BlockSpec((d, MOD_COL_TILE), lambda j: (0, j)),
                  pl.BlockSpec((1, MOD_COL_TILE), lambda j: (0, j))],
        out_specs=pl.BlockSpec((MOD_ROWS, MOD_COL_TILE), lambda j: (0, j)),
        compiler_params=pltpu.CompilerParams(dimension_semantics=("arbitrary",)),
        name="adaln_mod",
    )(c, c_ctx.reshape(1, d), w_ada, b_ada.reshape(1, n))


def _load_tokens(x_ref, row_ref, col_ref):
    x = x_ref[0]
    half = x.shape[-1] // 2
    x = jnp.concatenate([x[..., :half] + row_ref[...], x[..., half:] + col_ref[...]], axis=-1)
    return x.reshape(x.shape[0] * x.shape[1], x.shape[2])


def _act_columns(d, dk_tot, dv_tot):
    widths = dict(ma=d, gb=d, sr=dv_tot, v=dv_tot, q=dk_tot, k=dk_tot, lgf=dk_tot, lgb=dk_tot)
    names = sorted(widths, key=lambda n: -widths[n])
    cols, c0 = {}, 0
    for name in names:
        assert c0 % widths[name] == 0
        cols[name] = (c0, widths[name])
        c0 += widths[name]
    return cols


def _act_spec(cols, name, rows, index_map):
    c0, width = cols[name]
    return pl.BlockSpec((*rows, width), lambda *i: (*index_map(*i), c0 // width))


def _segment_rows(d, d_a, dk_tot, dv_tot, rank):
    names = ("u", "va", "q", "k", "v", "r", "dec", "ga", "gb")
    widths = (d_a, d_a, dk_tot, dk_tot, dv_tot, dv_tot, 2 * rank, d, d)
    offsets, r0 = {}, 0
    for name, w in zip(names, widths):
        offsets[name] = r0
        r0 += w
    return offsets


def _store_log_decay(dec, wdec2_ref, bdec_ref, di, out_ref):
    rank = wdec2_ref.shape[1]
    z = _dot(dec[:, di * rank:(di + 1) * rank].astype(BF16),
             wdec2_ref[di].astype(BF16)) + bdec_ref[di]
    out_ref[0] = (_log_sigmoid(z) * (1.0 / GATE_TAU)).astype(out_ref.dtype)


def _ctx_state_kernel(x_ref, sh_ref, sc_ref, g1_ref, wk_ref, wv_ref, wd_ref, wdec2_ref, bdec_ref,
                      s_ref, k_scr, v_scr, lgf_scr, lgb_scr, *, mod_row):
    bsz, t, d = x_ref.shape
    c = SCAN_CHUNK
    n_sub = t // c
    dk, dv = s_ref.shape[-2:]
    hb = (_rms(x_ref[...].reshape(bsz * t, d), g1_ref[...]) * (1.0 + sc_ref[mod_row:mod_row + 1, :])
          + sh_ref[mod_row:mod_row + 1, :]).astype(BF16)
    proj = lambda w_ref: _dot_nt(hb, w_ref[...].astype(BF16))
    dec = proj(wd_ref)
    for di, lg_scr in enumerate((lgf_scr, lgb_scr)):
        _store_log_decay(dec, wdec2_ref, bdec_ref, di, lg_scr)
    k_scr[...] = proj(wk_ref).astype(BF16)
    v_scr[...] = proj(wv_ref).astype(BF16)

    s_ref[...] = jnp.zeros_like(s_ref)
    masks = _scan_masks(c)
    for sub in range(n_sub):
        for bi in range(bsz):
            for di, lg_scr in enumerate((lgf_scr, lgb_scr)):
                first = bi * t + (sub if di == 0 else n_sub - 1 - sub) * c
                b, tot = _chunk_cumsum(lg_scr[0, first:first + c, :], *masks[di])
                for h in range(GLA_HEADS):
                    sl = slice(h * dk, (h + 1) * dk)
                    k = k_scr[first:first + c, sl].astype(F32)
                    kd = (k * jnp.exp(tot[:, sl] - b[:, sl])).astype(BF16)
                    v = v_scr[first:first + c, h * dv:(h + 1) * dv]
                    s_ref[bi, di, h] = _decay_rows(s_ref[bi, di, h], tot[:, sl]) + _dot_tn(kd, v)


def _ctx_state_call(ctx, mod, mod_row, g1, w_in_t, w_dec2, b_dec, seg, dk_tot, dv_tot):
    bsz, t, d = ctx.shape
    dk, dv = dk_tot // GLA_HEADS, dv_tot // GLA_HEADS
    assert t % SCAN_CHUNK == 0 and dv % dk == 0

    def rows_spec(r0, width):
        assert r0 % width == 0
        return pl.BlockSpec((width, d), lambda i: (r0 // width, 0))

    full = lambda a: pl.BlockSpec(a.shape, lambda i: (0,) * a.ndim)
    rank = w_dec2.shape[1]
    shape = (bsz, 2, GLA_HEADS, dk, dv)
    return pl.pallas_call(
        functools.partial(_ctx_state_kernel, mod_row=mod_row),
        out_shape=jax.ShapeDtypeStruct(shape, F32),
        grid=(1,),
        in_specs=[full(ctx), _mod_spec(mod, "shift1"), _mod_spec(mod, "scale1"), full(g1),
                  rows_spec(seg["k"], dk_tot), rows_spec(seg["v"], dv_tot),
                  rows_spec(seg["dec"], 2 * rank), full(w_dec2), full(b_dec)],
        out_specs=pl.BlockSpec(shape, lambda i: (0,) * len(shape)),
        scratch_shapes=[pltpu.VMEM((bsz * t, dk_tot), BF16), pltpu.VMEM((bsz * t, dv_tot), BF16),
                        pltpu.VMEM((1, bsz * t, dk_tot), BF16),
                        pltpu.VMEM((1, bsz * t, dk_tot), BF16)],
        compiler_params=pltpu.CompilerParams(dimension_semantics=("arbitrary",),
                                             vmem_limit_bytes=VMEM_LIMIT),
        name="ctx_state",
    )(ctx, mod, mod, g1, w_in_t, w_in_t, w_in_t, w_dec2, b_dec)


def _inproj_kernel(x_ref, row_ref, col_ref, sh_ref, sc_ref, g1_ref, wdec2_ref, bdec_ref,
                   lng_ref, lnb_ref, ws_ref, bs_ref, wt_hbm, pa_hbm, *rest,
                   seg, d_a, dk_tot, dv_tot, n_later, nt):
    later_f32, (act_ref, *rest) = rest[:n_later], rest[n_later:]
    later_bf16, (ya_scr, hb_scr, wt_ref, pa_ref, stage_ref, sem_ref) = rest[:n_later], rest[n_later:]
    step = pl.program_id(0)
    _, tm, d = hb_scr.shape
    cols = _act_columns(d, dk_tot, dv_tot)
    ma_ref, gb_ref, q_ref, k_ref, v_ref, sr_ref, lgf_ref, lgb_ref = (
        act_ref.at[:, :, pl.ds(*cols[name])]
        for name in ("ma", "gb", "q", "k", "v", "sr", "lgf", "lgb"))

    grid_rows = x_ref.shape[1]

    def prepare_next(part):
        n = grid_rows // PREPARE_PARTS
        xp = _load_tokens(x_ref.at[:, pl.ds(part * n, n)], row_ref.at[pl.ds(part * n, n)], col_ref)
        batch = jnp.minimum(step, pl.num_programs(0) - 2) // nt
        mine = pl.ds(batch, 1)
        hb = _rms(xp, g1_ref[...]) * (1.0 + sc_ref[mine, :]) + sh_ref[mine, :]
        hb_scr[step % 2, pl.ds(part * n * GRID_W, n * GRID_W), :] = hb.astype(BF16)

    def warm_up():
        for src, dst in ((wt_hbm, wt_ref), (pa_hbm, pa_ref)):
            _load_as_bf16(src, dst, stage_ref, sem_ref, _stage_rows(src.shape[0]))
        for part in range(PREPARE_PARTS):
            prepare_next(part)

    lax.cond(step == 0, warm_up,
             functools.partial(_project_tile, prepare_next, rest_refs=(
                 wdec2_ref, bdec_ref, lng_ref, lnb_ref, ws_ref, bs_ref, later_f32, later_bf16,
                 ma_ref, gb_ref, q_ref, k_ref, v_ref, sr_ref, lgf_ref, lgb_ref,
                 ya_scr, hb_scr, wt_ref, pa_ref),
                 seg=seg, d_a=d_a, dk_tot=dk_tot, dv_tot=dv_tot))


def _project_tile(prepare_next, *, rest_refs, seg, d_a, dk_tot, dv_tot):
    (wdec2_ref, bdec_ref, lng_ref, lnb_ref, ws_ref, bs_ref, later_f32, later_bf16,
     ma_ref, gb_ref, q_ref, k_ref, v_ref, sr_ref, lgf_ref, lgb_ref,
     ya_scr, hb_scr, wt_ref, pa_ref) = rest_refs
    _, tm, d = hb_scr.shape
    ready = (pl.program_id(0) + 1) % 2

    for src, dst in zip(later_f32, later_bf16):
        dst[...] = src[...].astype(BF16)

    def proj(r0, width):
        return _dot_nt(hb_scr[ready], wt_ref[r0:r0 + width, :])

    r_u, r_va, r_q, r_k, r_v, r_r = (seg[n] for n in ("u", "va", "q", "k", "v", "r"))
    r_ga, r_gb = seg["ga"], seg["gb"]
    rank = wdec2_ref.shape[1]
    groups, chunk_a = ws_ref.shape[0], ws_ref.shape[1]
    dk = dk_tot // GLA_HEADS
    gw = d_a // groups
    blk = PROJ_COL_TILE
    cols = lambda j: slice(j * blk, (j + 1) * blk)
    gu, gv, vn = {}, {}, {}

    stage = {}

    def store(out_ref, j, fn):
        def ep(y):
            out_ref[0, :, cols(j)] = fn(y).astype(out_ref.dtype)
        return ep

    def keep(table, j, fn):
        def ep(y):
            table[j] = fn(y)
        return ep

    def add(name, matmul, epilogue):
        stage[name] = (matmul, epilogue)
        return name

    def ep_dec(dec):
        for di, out_ref in enumerate((lgf_ref, lgb_ref)):
            _store_log_decay(dec, wdec2_ref, bdec_ref, di, out_ref)
    s_dec = add("dec", lambda: proj(seg["dec"], 2 * rank), ep_dec)
    s_u = [add(f"u{j}", lambda j=j: proj(r_u + j * blk, blk), keep(gu, j, _gelu_tanh))
           for j in range(d_a // blk)]
    s_va = [add(f"va{j}", lambda j=j: proj(r_va + j * blk, blk), keep(gv, j, _gelu_tanh))
            for j in range(d_a // blk)]

    def ep_q_and_layernorm(y):
        q_ref[0] = (y * dk ** -0.5).astype(BF16)
        nb = d_a // blk
        mu = sum(jnp.sum(gv[j], axis=-1, keepdims=True) for j in range(nb)) * (1.0 / d_a)
        cen = [gv[j] - mu for j in range(nb)]
        var = sum(jnp.sum(c * c, axis=-1, keepdims=True) for c in cen) * (1.0 / d_a)
        inv = lax.rsqrt(var + EPS)
        for j in range(nb):
            vn[j] = (cen[j] * inv * lng_ref[:, cols(j)] + lnb_ref[:, cols(j)]).astype(BF16)
    s_q = add("q", lambda: proj(r_q, dk_tot), ep_q_and_layernorm)

    def ep_k(y):
        k_ref[0] = y.astype(BF16)
    s_k = add("k", lambda: proj(r_k, dk_tot), ep_k)
    s_r = [add(f"r{j}", lambda j=j: proj(r_r + j * blk, blk),
               store(sr_ref, j, lambda y: y * _sigmoid(y))) for j in range(dv_tot // blk)]
    s_v = [add(f"v{j}", lambda j=j: proj(r_v + j * blk, blk), store(v_ref, j, lambda y: y))
           for j in range(dv_tot // blk)]

    def mm_gate(ci, g):
        r0, c0 = ci * chunk_a, g * gw
        return _dot(ws_ref[g].astype(BF16),
                    vn[c0 // blk][r0:r0 + chunk_a, c0 % blk:c0 % blk + gw])

    def ep_gate(ci, g):
        r0, c0 = ci * chunk_a, g * gw

        def ep(s):
            u = gu[c0 // blk][r0:r0 + chunk_a, c0 % blk:c0 % blk + gw]
            ya_scr[r0:r0 + chunk_a, c0:c0 + gw] = (u * (s + bs_ref[g])).astype(BF16)
        return ep
    s_mix = [add(f"mix{ci}_{g}", lambda ci=ci, g=g: mm_gate(ci, g), ep_gate(ci, g))
             for ci in range(tm // chunk_a) for g in range(groups)]
    s_gb = [add(f"gb{j}", lambda j=j: proj(r_gb + j * blk, blk), store(gb_ref, j, _sigmoid))
            for j in range(d // blk)]
    s_ga = [add(f"ga{j}", lambda j=j: (proj(r_ga + j * blk, blk),
                                       _dot(ya_scr[...], pa_ref[:, cols(j)])),
                store(ma_ref, j, lambda y: _sigmoid(y[0]) * y[1])) for j in range(d // blk)]

    order = ([s_dec] + s_u + s_va + [s_q, s_k] + s_r + s_gb + s_mix + s_v[:1] + s_ga + s_v[1:])
    assert sorted(order) == sorted(stage)
    extra = {}
    for part, name in enumerate(order[-PREPARE_PARTS - 1:-1]):
        extra.setdefault(name, []).append(functools.partial(prepare_next, part))
    results = {}
    for i in range(len(order) + STAGE_LOOKAHEAD):
        if i < len(order):
            results[order[i]] = stage[order[i]][0]()
        if i >= STAGE_LOOKAHEAD:
            done = order[i - STAGE_LOOKAHEAD]
            stage[done][1](results.pop(done))
            for work in extra.get(done, ()):
                work()


def _inproj_call(x4, pos, mod, consts, weights, later_weights, seg, d_a, dk_tot, dv_tot):
    bsz, nrow, gw_, d = x4.shape
    t = nrow * gw_
    tm = TOKEN_TILE
    nt = t // tm
    rows_per_tile = tm // GRID_W
    rowemb, colemb = pos
    chunk_a = consts[5].shape[1]
    assert tm % chunk_a == 0 and tm % GRID_W == 0 and t % tm == 0

    n_tiles = bsz * nt
    preparing = lambda s: jnp.minimum(s, n_tiles - 1)
    projecting = lambda s: jnp.maximum(s - 1, 0)

    def slab(w):
        rows = w.shape[0] // n_tiles
        assert w.shape[0] % n_tiles == 0 and rows % BF16_SUBLANES == 0
        return pl.BlockSpec((rows, w.shape[1]), lambda s: (projecting(s), 0))

    tok = lambda w: pl.BlockSpec((1, tm, w), lambda s: (projecting(s) // nt, projecting(s) % nt, 0))
    in_specs = ([pl.BlockSpec((1, rows_per_tile, GRID_W, d),
                              lambda s: (preparing(s) // nt, preparing(s) % nt, 0, 0)),
                 pl.BlockSpec((rows_per_tile, 1, d // 2), lambda s: (preparing(s) % nt, 0, 0)),
                 _const_spec(colemb.shape), _mod_spec(mod, "shift1"), _mod_spec(mod, "scale1")]
                + [_const_spec(a.shape) for a in consts]
                + [pl.BlockSpec(memory_space=pl.ANY)] * len(weights)
                + [slab(w) for w in later_weights])
    act_width = sum(w for _, w in _act_columns(d, dk_tot, dv_tot).values())
    kern = functools.partial(_inproj_kernel, seg=seg, d_a=d_a, dk_tot=dk_tot, dv_tot=dv_tot,
                             n_later=len(later_weights), nt=nt)
    return pl.pallas_call(
        kern,
        out_shape=([jax.ShapeDtypeStruct((bsz, t, act_width), BF16)]
                   + [jax.ShapeDtypeStruct(w.shape, BF16) for w in later_weights]),
        grid=(n_tiles + 1,),
        in_specs=in_specs,
        out_specs=[tok(act_width)] + [slab(w) for w in later_weights],
        scratch_shapes=([pltpu.VMEM((tm, d_a), BF16), pltpu.VMEM((2, tm, d), BF16)]
                        + [pltpu.VMEM(w.shape, BF16) for w in weights]
                        + [pltpu.VMEM((WEIGHT_STAGE_SLOTS, WEIGHT_STAGE_ROWS, d), F32),
                           pltpu.SemaphoreType.DMA((WEIGHT_STAGE_SLOTS,))]),
        compiler_params=pltpu.CompilerParams(dimension_semantics=("arbitrary",),
                                             vmem_limit_bytes=VMEM_LIMIT),
        name="inproj",
    )(x4, rowemb, colemb, mod, mod, *consts, *weights, *later_weights)


def _scan_masks(c):
    row = lax.broadcasted_iota(jnp.int32, (c, c), 0)
    col = lax.broadcasted_iota(jnp.int32, (c, c), 1)
    return ((row >= col, c - 1), (row <= col, 0))


def _chunk_cumsum(lg, keep, end):
    b = _dot(jnp.where(keep, 1.0, 0.0).astype(BF16), lg)
    return b, b[end:end + 1, :]


def _decay_rows(st, tot):
    dk, dv = st.shape
    dcol = jnp.transpose(jnp.broadcast_to(jnp.exp(tot), (dk, dk)))
    return st * jnp.concatenate([dcol] * (dv // dk), axis=1)


def _scan_kernel(qf_ref, kf_ref, vf_ref, lgf_ref, qb_ref, kb_ref, vb_ref, lgb_ref, s0_ref,
                 of_ref, ob_ref, st_scr, q32_scr, k32_scr, v32_scr, b_scr, o_scr,
                 *, dk, dv):
    t = pl.program_id(0)
    bsz, c = qf_ref.shape[0], SCAN_CHUNK
    n_sub = qf_ref.shape[1] // c

    def rows(di, sub):
        first = (sub if di == 0 else n_sub - 1 - sub) * c
        return slice(first, first + c)

    @pl.when(t == 0)
    def _():
        st_scr[...] = s0_ref[...]

    masks = _scan_masks(c)
    dirs = ((qf_ref, kf_ref, vf_ref, lgf_ref, of_ref) + masks[0],
            (qb_ref, kb_ref, vb_ref, lgb_ref, ob_ref) + masks[1])

    worst = None
    for lg_ref in (lgf_ref, lgb_ref):
        for sub in range(n_sub):
            w = jnp.max(-jnp.sum(lg_ref[:, sub * c:(sub + 1) * c, :].astype(F32), axis=1))
            worst = w if worst is None else jnp.maximum(worst, w)

    cums = {}

    def cumulate(bi, di, sub):
        lg_ref, keep, end = dirs[di][3], dirs[di][5], dirs[di][6]
        cums[bi, di, sub] = _chunk_cumsum(lg_ref[bi, rows(di, sub), :], keep, end)

    def load(bi, di, h, sub):
        q_ref, k_ref, v_ref = dirs[di][:3]
        b, tot = cums[bi, di, sub]
        sl = slice(h * dk, (h + 1) * dk)
        return (q_ref[bi, rows(di, sub), sl].astype(F32), k_ref[bi, rows(di, sub), sl].astype(F32),
                v_ref[bi, rows(di, sub), h * dv:(h + 1) * dv], b[:, sl], tot[:, sl])

    def decayed_state(bi, di, h, tot):
        return _decay_rows(st_scr[bi, di, h], tot)

    def fast_step():
        chains = [(bi, di, h, sub) for sub in range(n_sub) for bi in range(bsz)
                  for di in range(2) for h in range(GLA_HEADS)]
        live = {}

        def factors(n):
            q, k, v, bh, tot = load(*chains[n])
            half = 0.5 * tot
            eh = jnp.exp(half)
            qe = q * jnp.exp(bh - half)
            ke = k * jnp.exp(half - bh)
            live[n] = dict(v=v, tot=tot, qe=qe.astype(BF16), ke=ke.astype(BF16),
                           qd=(qe * eh).astype(BF16), kd=(ke * eh).astype(BF16))

        def scores(n):
            live[n]["a"] = _dot_nt(live[n].pop("qe"), live[n].pop("ke"))

        def outputs(n):
            bi, di, h, _ = chains[n]
            c_ = live[n]
            a = jnp.where(dirs[di][5], c_.pop("a"), 0.0).astype(BF16)
            lhs = jnp.concatenate([a, c_.pop("qd")], axis=1)
            rhs = jnp.concatenate([c_["v"], st_scr[bi, di, h].astype(BF16)], axis=0)
            c_["o"] = _dot(lhs, rhs)
            c_["u"] = _dot_tn(c_.pop("kd"), c_.pop("v"))

        def finish(n):
            bi, di, h, sub = chains[n]
            c_ = live.pop(n)
            o_ref = dirs[di][4]
            o_ref[bi, rows(di, sub), h * dv:(h + 1) * dv] = c_["o"].astype(o_ref.dtype)
            st_scr[bi, di, h] = decayed_state(bi, di, h, c_["tot"]) + c_["u"]

        def cumulate_for(n):
            bi, di, _, sub = chains[n]
            cumulate(bi, di, sub)

        phases = ((factors, 0), (scores, 1), (outputs, 3), (finish, 5))
        cumulate_for(0)
        for i in range(len(chains) + phases[-1][1]):
            ahead = i + GLA_HEADS
            if i % GLA_HEADS == 0 and ahead < len(chains):
                cumulate_for(ahead)
            for phase, lag in phases:
                if 0 <= i - lag < len(chains):
                    phase(i - lag)

    def safe_step():
        ridx = lax.broadcasted_iota(jnp.int32, (c, 1), 0)
        groups = [(s, b, d) for s in range(n_sub) for b in range(bsz) for d in range(2)]
        for sub, bi, di in groups:
            o_ref = dirs[di][4]
            cumulate(bi, di, sub)
            for h in range(GLA_HEADS):
                q, k, v, bh, tot = load(bi, di, h, sub)
                q32_scr[...] = q
                k32_scr[...] = k
                v32_scr[...] = v.astype(F32)
                b_scr[...] = bh
                o_scr[...] = _dot((q * jnp.exp(bh)).astype(BF16), st_scr[bi, di, h].astype(BF16))
                st_scr[bi, di, h] = (decayed_state(bi, di, h, tot)
                                     + _dot_tn((k * jnp.exp(tot - bh)).astype(BF16), v))

                def body(j, carry):
                    bj = b_scr[pl.ds(j, 1), :]
                    valid = (ridx >= j) if di == 0 else (ridx <= j)
                    e = jnp.exp(jnp.where(valid, b_scr[...] - bj, -jnp.inf))
                    w = jnp.sum(q32_scr[...] * e * k32_scr[pl.ds(j, 1), :], axis=-1, keepdims=True)
                    o_scr[...] += w * v32_scr[pl.ds(j, 1), :]
                    return carry

                lax.fori_loop(0, c, body, 0)
                o_ref[bi, rows(di, sub), h * dv:(h + 1) * dv] = o_scr[...].astype(o_ref.dtype)

    lax.cond(worst <= SAFE_DECAY, fast_step, safe_step)


def _scan_call(acts, cols, s0):
    bsz, t, _ = acts.shape
    dk_tot, dv_tot = cols["q"][1], cols["v"][1]
    dk, dv = dk_tot // GLA_HEADS, dv_tot // GLA_HEADS
    c = SCAN_CHUNK
    tile = SCAN_TILE
    nt = t // tile
    assert t % tile == 0 and tile % c == 0 and dv % dk == 0
    fwd = lambda w: pl.BlockSpec((bsz, tile, w), lambda i: (0, i, 0))
    bwd = lambda w: pl.BlockSpec((bsz, tile, w), lambda i: (0, nt - 1 - i, 0))
    fwd_in = lambda name: _act_spec(cols, name, (bsz, tile), lambda i: (0, i))
    bwd_in = lambda name: _act_spec(cols, name, (bsz, tile), lambda i: (0, nt - 1 - i))
    kern = functools.partial(_scan_kernel, dk=dk, dv=dv)
    return pl.pallas_call(
        kern,
        out_shape=[jax.ShapeDtypeStruct((bsz, t, dv_tot), BF16),
                   jax.ShapeDtypeStruct((bsz, t, dv_tot), BF16)],
        grid=(nt,),
        in_specs=[fwd_in("q"), fwd_in("k"), fwd_in("v"), fwd_in("lgf"),
                  bwd_in("q"), bwd_in("k"), bwd_in("v"), bwd_in("lgb"), _const_spec(s0.shape)],
        out_specs=[fwd(dv_tot), bwd(dv_tot)],
        scratch_shapes=[pltpu.VMEM(s0.shape, F32),
                        pltpu.VMEM((c, dk), F32), pltpu.VMEM((c, dk), F32),
                        pltpu.VMEM((c, dv), F32), pltpu.VMEM((c, dk), F32),
                        pltpu.VMEM((c, dv), F32)],
        compiler_params=pltpu.CompilerParams(dimension_semantics=("arbitrary",),
                                             vmem_limit_bytes=VMEM_LIMIT),
        name="gla_scan",
    )(*[acts] * 8, s0)


def _tail_kernel(x_ref, row_ref, col_ref, ma_ref, gb_ref, of_ref, ob_ref, sr_ref,
                 g1_ref, sh2_ref, sc2_ref, g2_ref, og_ref, n2_ref, fg_ref,
                 pb_ref, wo_ref, w1_ref, w2_ref, out_ref, *, dv):
    xp = _load_tokens(x_ref, row_ref, col_ref)
    mine = pl.ds(pl.program_id(0), 1)
    o = of_ref[0].astype(F32) + ob_ref[0].astype(F32)
    heads = []
    for h in range(GLA_HEADS):
        sl = slice(h * dv, (h + 1) * dv)
        heads.append(_rms(o[:, sl], og_ref[:, sl]))
    yb = (jnp.concatenate(heads, axis=-1) * sr_ref[0].astype(F32)).astype(BF16)
    merged = ma_ref[0].astype(F32) + gb_ref[0].astype(F32) * _dot(yb, pb_ref[...])
    x1 = xp + g1_ref[mine, :] * _dot(merged.astype(BF16), wo_ref[...])
    h2 = (_rms(x1, n2_ref[...]) * (1.0 + sc2_ref[mine, :]) + sh2_ref[mine, :]).astype(BF16)
    acc = jnp.zeros_like(x1)
    d_ff = w1_ref.shape[1]
    for j in range(d_ff // FF_COL_TILE):
        sl = slice(j * FF_COL_TILE, (j + 1) * FF_COL_TILE)
        f = jnp.maximum(_dot(h2, w1_ref[:, sl]), 0.0)
        acc = acc + _dot((f * f).astype(BF16), w2_ref[sl, :])
    x2 = x1 + g2_ref[mine, :] * acc
    out_ref[0] = _rms(x2, fg_ref[...])


def _tail_call(x4, pos, acts, cols, o_f, o_b, mod, consts):
    bsz, nrow, gw_, d = x4.shape
    t = nrow * gw_
    tm = TOKEN_TILE
    rows_per_tile = tm // GRID_W
    rowemb, colemb = pos
    dv = o_f.shape[-1] // GLA_HEADS
    tok = lambda w: pl.BlockSpec((1, tm, w), lambda b, i: (b, i, 0))
    act = lambda name: _act_spec(cols, name, (1, tm), lambda b, i: (b, i))
    mod_names = ("gate1", "shift2", "scale2", "gate2")
    in_specs = ([pl.BlockSpec((1, rows_per_tile, GRID_W, d), lambda b, i: (b, i, 0, 0)),
                 pl.BlockSpec((rows_per_tile, 1, d // 2), lambda b, i: (i, 0, 0)),
                 _const_spec(colemb.shape),
                 act("ma"), act("gb"), tok(o_f.shape[-1]), tok(o_b.shape[-1]), act("sr")]
                + [_mod_spec(mod, name) for name in mod_names]
                + [_const_spec(a.shape) for a in consts])
    return pl.pallas_call(
        functools.partial(_tail_kernel, dv=dv),
        out_shape=jax.ShapeDtypeStruct((bsz, t, d), F32),
        grid=(bsz, t // tm),
        in_specs=in_specs,
        out_specs=tok(d),
        compiler_params=pltpu.CompilerParams(dimension_semantics=("arbitrary", "arbitrary"),
                                             vmem_limit_bytes=VMEM_LIMIT),
        name="tail",
    )(x4, rowemb, colemb, acts, acts, o_f, o_b, acts, *[mod] * len(mod_names), *consts)


def _pos_tables(t, d):
    quarter = d // 4
    one, base = np.float32(1.0), np.float32(10000.0)
    freqs = one / (base ** (np.arange(quarter, dtype=np.float32) / np.float32(quarter)))

    def emb(n):
        a = np.arange(n, dtype=np.float32)[:, None] * freqs[None, :]
        return np.concatenate([np.sin(a), np.cos(a)], -1).astype(np.float32)

    return jnp.asarray(emb(t // GRID_W)[:, None, :]), jnp.asarray(emb(GRID_W)[None, :, :])


def kernel(x, c, ctx, c_ctx, w_ada, b_ada, norm1_g, w_in, lnv_g, lnv_b, w_s, b_s, w_dec2, b_dec,
           onorm_g, proj_a, proj_b, w_out, norm2_g, w_ff1, w_ff2, final_g):
    bsz, t, d = x.shape
    depth = w_in.shape[0]
    assert depth == 1, "single-layer problem: the context stream only feeds the scan states"
    d_a = proj_a.shape[1]
    dv_tot = proj_b.shape[1]
    dk_tot = w_dec2.shape[-1]
    seg = _segment_rows(d, d_a, dk_tot, dv_tot, w_dec2.shape[2])

    mod = _mod_call(c, c_ctx, w_ada[0], b_ada[0])

    row = lambda a: a.reshape(1, -1)
    g1n = row(norm1_g[0])
    w_in_t = jnp.swapaxes(w_in[0], 0, 1)
    wd2, bd = w_dec2[0], b_dec[0][:, None, :]

    s_ctx = _ctx_state_call(ctx, mod, bsz, g1n, w_in_t, wd2, bd, seg, dk_tot, dv_tot)

    pos = _pos_tables(t, d)
    x4 = x.reshape(bsz, t // GRID_W, GRID_W, d)
    consts = (g1n, wd2, bd, row(lnv_g[0]), row(lnv_b[0]), w_s[0], b_s[0][:, :, None])
    acts, *tail_weights = _inproj_call(
        x4, pos, mod, consts, (w_in_t, proj_a[0]),
        (proj_b[0], w_out[0], w_ff1[0], w_ff2[0]), seg, d_a, dk_tot, dv_tot)
    cols = _act_columns(d, dk_tot, dv_tot)
    o_f, o_b = _scan_call(acts, cols, s_ctx)

    return _tail_call(x4, pos, acts, cols, o_f, o_b, mod,
                      (row(onorm_g[0]), row(norm2_g[0]), row(final_g), *tail_weights))
```
